```python
import math
import jax, jax.numpy as jnp
from jax import lax
import numpy as np

D_MODEL = 1024
BATCH = 8
SEQ = 2048
DEPTH = 4
DEC_BATCH = 128
DEC_SEQ = 1
PAST_LEN = 16384
PAGE_SIZE = 128

A_HEADS = 6
A_DK = 64
A_DV = 64
A_QK = A_HEADS * A_DK
A_WIDTH = A_HEADS * A_DV
B_HEADS = 6
B_DK = 32
B_DV = 64
B_QK = B_HEADS * B_DK
B_WIDTH = B_HEADS * B_DV
B_RANK = 16
B_TAU = 16.0
C_GROUPS = 16
C_CH = 16
C_STATE = 64
C_WIDTH = C_GROUPS * C_CH
MIX_WIDTH = A_WIDTH + B_WIDTH + C_WIDTH
IN_SIZES = (A_QK, A_QK, A_WIDTH, A_WIDTH, B_QK, B_QK, B_WIDTH, B_RANK, B_WIDTH, C_WIDTH)
IN_COLS = sum(IN_SIZES)
IN_SPLITS = tuple(int(s) for s in np.cumsum(IN_SIZES)[:-1])
CHUNK = 64
MOE_GROUPS = 4
EXPERTS_PER_GROUP = 4
N_EXPERTS = MOE_GROUPS * EXPERTS_PER_GROUP
TOP_K_INNER = 2
D_EXPERT = 256
EPS = 1e-6

kernel_name = 'hybrid_hgrn2_gla_s5_hmoe_step'


def rmsnorm(x, g):
    xf = x.astype(jnp.float32)
    y = xf * lax.rsqrt(jnp.mean(xf * xf, axis=-1, keepdims=True) + EPS)
    return (y * g.astype(jnp.float32)).astype(x.dtype)


def modulate(h, shift, scale):
    return h * (1 + scale[:, None, :]) + shift[:, None, :]


def chunk_gated_linear(q, k, v, log_a, s0):
    bsz, seq, nh, _ = q.shape
    dv = v.shape[-1]
    c = min(CHUNK, seq)
    pad = (-seq) % c
    lp = seq + pad
    n = lp // c

    def blocks(t):
        t = jnp.pad(t.astype(jnp.float32), ((0, 0), (0, pad), (0, 0), (0, 0)))
        return t.reshape(bsz, n, c, nh, t.shape[-1]).transpose(1, 0, 3, 2, 4)

    mask = jnp.tril(jnp.ones((c, c), dtype=bool))[:, :, None]

    def step(S, blk):
        qc, kc, vc, gc = blk
        b = jnp.cumsum(gc, axis=2)
        decay = jnp.exp(jnp.where(mask, b[:, :, :, None, :] - b[:, :, None, :, :], -jnp.inf))
        scores = jnp.einsum('bhtk,bhsk,bhtsk->bhts', qc, kc, decay)
        o = (jnp.einsum('bhts,bhsv->bhtv', scores, vc)
             + jnp.einsum('bhtk,bhkv->bhtv', qc * jnp.exp(b), S))
        b_end = b[:, :, -1, :]
        S = (jnp.exp(b_end)[..., None] * S
             + jnp.einsum('bhsk,bhsv->bhkv', kc * jnp.exp(b_end[:, :, None, :] - b), vc))
        return S, o

    S, o = lax.scan(step, s0.astype(jnp.float32), (blocks(q), blocks(k), blocks(v), blocks(log_a)))
    o = o.transpose(1, 0, 3, 2, 4).reshape(bsz, lp, nh, dv)[:, :seq]
    return o, S


def s5_ssm(u, a_re, a_im, b_re, b_im, c_re, c_im, d_skip, log_dt, s0_re, s0_im):
    f32 = jnp.float32
    bsz, seq, _ = u.shape
    a_re, a_im = a_re.astype(f32), a_im.astype(f32)
    dt = jnp.exp(log_dt.astype(f32))[:, None]
    mag = jnp.exp(a_re * dt)
    abar_re = mag * jnp.cos(a_im * dt)
    abar_im = mag * jnp.sin(a_im * dt)
    den = a_re * a_re + a_im * a_im
    nr, ni = abar_re - 1.0, abar_im
    coef_re = ((nr * a_re + ni * a_im) / den)[..., None]
    coef_im = ((ni * a_re - nr * a_im) / den)[..., None]
    b_re, b_im = b_re.astype(f32), b_im.astype(f32)
    bb_re = coef_re * b_re - coef_im * b_im
    bb_im = coef_re * b_im + coef_im * b_re
    ug = u.reshape(bsz, seq, C_GROUPS, C_CH)
    xr = jnp.einsum('blgh,gph->blgp', ug, bb_re)
    xi = jnp.einsum('blgh,gph->blgp', ug, bb_im)
    s0_re, s0_im = s0_re.astype(f32), s0_im.astype(f32)
    xr = xr.at[:, 0].add(abar_re * s0_re - abar_im * s0_im)
    xi = xi.at[:, 0].add(abar_re * s0_im + abar_im * s0_re)
    ar = jnp.broadcast_to(abar_re, xr.shape)
    ai = jnp.broadcast_to(abar_im, xr.shape)

    def combine(e1, e2):
        a1r, a1i, b1r, b1i = e1
        a2r, a2i, b2r, b2i = e2
        return (a2r * a1r - a2i * a1i, a2r * a1i + a2i * a1r,
                a2r * b1r - a2i * b1i + b2r, a2r * b1i + a2i * b1r + b2i)

    _, _, sr, si = lax.associative_scan(combine, (ar, ai, xr, xi), axis=1)
    y = (jnp.einsum('blgp,ghp->blgh', sr, c_re.astype(f32))
         - jnp.einsum('blgp,ghp->blgh', si, c_im.astype(f32)))
    y = y.reshape(bsz, seq, C_WIDTH) + d_skip.astype(f32) * u
    return y, sr[:, -1], si[:, -1]


def token_mixers(h, s_hgrn, s_gla, s5_re, s5_im, w_in, lb, hgrn_norm_g, gla_w_alpha, gla_b_alpha,
                 gla_norm_g, a_re, a_im, b_re, b_im, c_re, c_im, d_skip, log_dt, w_glu, b_glu, w_out):
    f32 = jnp.float32
    bsz, seq, _ = h.shape
    proj = (h @ w_in).astype(f32)
    aq, af, ai, ag, bq, bk, bv, blr, br, cu = jnp.split(proj, IN_SPLITS, axis=-1)

    def heads(t, nh):
        return t.reshape(bsz, seq, nh, -1)

    lb = lb.astype(f32)
    log_f = jnp.logaddexp(jnp.log(lb), jnp.log1p(-lb) + jax.nn.log_sigmoid(af))
    k_a = (1.0 - lb) * jax.nn.sigmoid(-af)
    o_a, s_hgrn = chunk_gated_linear(heads(jax.nn.silu(aq), A_HEADS), heads(k_a, A_HEADS),
                                     heads(ai, A_HEADS), heads(log_f, A_HEADS), s_hgrn)
    o_a = (rmsnorm(o_a, hgrn_norm_g) * jax.nn.silu(heads(ag, A_HEADS))).reshape(bsz, seq, A_WIDTH)

    log_alpha = jax.nn.log_sigmoid(blr @ gla_w_alpha.astype(f32) + gla_b_alpha.astype(f32)) / B_TAU
    o_b, s_gla = chunk_gated_linear(heads(bq, B_HEADS) * (B_DK ** -0.5), heads(bk, B_HEADS),
                                    heads(bv, B_HEADS), heads(log_alpha, B_HEADS), s_gla)
    o_b = (rmsnorm(o_b, gla_norm_g) * jax.nn.silu(heads(br, B_HEADS))).reshape(bsz, seq, B_WIDTH)

    y_c, s5_re, s5_im = s5_ssm(cu, a_re, a_im, b_re, b_im, c_re, c_im, d_skip, log_dt, s5_re, s5_im)
    z = jax.nn.gelu(y_c)
    o_c = z * jax.nn.sigmoid(z @ w_glu.astype(f32) + b_glu.astype(f32))

    o = jnp.concatenate([o_a, o_b, o_c], axis=-1).astype(h.dtype) @ w_out
    return o, s_hgrn, s_gla, s5_re, s5_im


def hier_moe(h, w_group, b_group, w_expert, b_expert, w_gate, w_up, w_down):
    f32 = jnp.float32
    bsz, seq, d = h.shape
    t = h.reshape(-1, d)
    g_logits = (t @ w_group + b_group).astype(f32)
    g_idx = jnp.argmax(g_logits, axis=-1)
    p_group = jnp.max(jax.nn.softmax(g_logits, axis=-1), axis=-1, keepdims=True)
    e_logits = (t @ w_expert + b_expert).astype(f32).reshape(-1, MOE_GROUPS, EXPERTS_PER_GROUP)
    e_in_group = jnp.einsum('tg,tge->te', jax.nn.one_hot(g_idx, MOE_GROUPS, dtype=f32), e_logits)
    top_v, top_i = lax.top_k(e_in_group, TOP_K_INNER)
    e_idx = g_idx[:, None] * EXPERTS_PER_GROUP + top_i
    gates = p_group * jnp.einsum('tk,tke->te', jax.nn.softmax(top_v, axis=-1),
                                 jax.nn.one_hot(e_idx, N_EXPERTS, dtype=f32))
    y = jnp.zeros_like(t)
    for e in range(N_EXPERTS):
        hid = jax.nn.silu(t @ w_gate[e]) * (t @ w_up[e])
        y = y + gates[:, e:e + 1].astype(t.dtype) * (hid @ w_down[e])
    return y.reshape(bsz, seq, d)


def setup_inputs(seed: int = 0) -> dict:
    key = jax.random.key(seed)
    ks = iter(jax.random.split(key, 48))
    f32 = jnp.float32

    def nrm(shape, scale):
        return jax.random.normal(next(ks), shape, f32) * scale

    n_idx = jnp.arange(C_STATE, dtype=f32)
    return {
        'x_prompt': nrm((BATCH, SEQ, D_MODEL), 1.0),
        'x_sample': nrm((DEC_BATCH, DEC_SEQ, D_MODEL), 1.0),
        'state_hgrn': nrm((DEPTH, DEC_BATCH, A_HEADS, A_DK, A_DV), 0.5),
        'state_gla': nrm((DEPTH, DEC_BATCH, B_HEADS, B_DK, B_DV), 0.5),
        'state_s5_re': nrm((DEPTH, DEC_BATCH, C_GROUPS, C_STATE), 0.3),
        'state_s5_im': nrm((DEPTH, DEC_BATCH, C_GROUPS, C_STATE), 0.3),
        'c_prompt': nrm((BATCH, D_MODEL), 1.0),
        'c_sample': nrm((DEC_BATCH, D_MODEL), 1.0),
        'ada_w': nrm((DEPTH, D_MODEL, 6 * D_MODEL), 0.5 * D_MODEL ** -0.5),
        'ada_b': nrm((DEPTH, 6 * D_MODEL), 0.02),
        'norm1_g': 1.0 + nrm((DEPTH, D_MODEL), 0.05),
        'norm2_g': 1.0 + nrm((DEPTH, D_MODEL), 0.05),
        'w_in': nrm((DEPTH, D_MODEL, IN_COLS), D_MODEL ** -0.5),
        'hgrn_lb_logits': nrm((DEPTH, A_QK), 0.1),
        'hgrn_norm_g': 1.0 + nrm((DEPTH, A_DV), 0.05),
        'gla_w_alpha': nrm((DEPTH, B_RANK, B_QK), B_RANK ** -0.5),
        'gla_b_alpha': nrm((DEPTH, B_QK), 0.1),
        'gla_norm_g': 1.0 + nrm((DEPTH, B_DV), 0.05),
        's5_a_re': -0.5 + nrm((DEPTH, C_GROUPS, C_STATE), 0.01),
        's5_a_im': math.pi * n_idx + nrm((DEPTH, C_GROUPS, C_STATE), 0.01),
        's5_b_re': nrm((DEPTH, C_GROUPS, C_STATE, C_CH), (2 * C_CH) ** -0.5),
        's5_b_im': nrm((DEPTH, C_GROUPS, C_STATE, C_CH), (2 * C_CH) ** -0.5),
        's5_c_re': nrm((DEPTH, C_GROUPS, C_CH, C_STATE), C_STATE ** -0.5),
        's5_c_im': nrm((DEPTH, C_GROUPS, C_CH, C_STATE), C_STATE ** -0.5),
        's5_d': nrm((DEPTH, C_WIDTH), 1.0),
        's5_log_dt': jax.random.uniform(next(ks), (DEPTH, C_GROUPS), f32, math.log(1e-3), math.log(1e-1)),
        's5_w_glu': nrm((DEPTH, C_WIDTH, C_WIDTH), C_WIDTH ** -0.5),
        's5_b_glu': nrm((DEPTH, C_WIDTH), 0.02),
        'w_out': nrm((DEPTH, MIX_WIDTH, D_MODEL), MIX_WIDTH ** -0.5),
        'moe_w_group': nrm((DEPTH, D_MODEL, MOE_GROUPS), D_MODEL ** -0.5),
        'moe_b_group': nrm((DEPTH, MOE_GROUPS), 0.01),
        'moe_w_expert': nrm((DEPTH, D_MODEL, N_EXPERTS), D_MODEL ** -0.5),
        'moe_b_expert': nrm((DEPTH, N_EXPERTS), 0.01),
        'moe_w_gate': nrm((DEPTH, N_EXPERTS, D_MODEL, D_EXPERT), D_MODEL ** -0.5),
        'moe_w_up': nrm((DEPTH, N_EXPERTS, D_MODEL, D_EXPERT), D_MODEL ** -0.5),
        'moe_w_down': nrm((DEPTH, N_EXPERTS, D_EXPERT, D_MODEL), D_EXPERT ** -0.5),
        'final_norm_g': 1.0 + nrm((D_MODEL,), 0.05),
    }


def reference(x_prompt, x_sample, state_hgrn, state_gla, state_s5_re, state_s5_im, c_prompt, c_sample,
              ada_w, ada_b, norm1_g, norm2_g, w_in, hgrn_lb_logits, hgrn_norm_g, gla_w_alpha, gla_b_alpha,
              gla_norm_g, s5_a_re, s5_a_im, s5_b_re, s5_b_im, s5_c_re, s5_c_im, s5_d, s5_log_dt, s5_w_glu,
              s5_b_glu, w_out, moe_w_group, moe_b_group, moe_w_expert, moe_b_expert, moe_w_gate, moe_w_up,
              moe_w_down, final_norm_g):
    f32 = jnp.float32
    lb_cum = jnp.cumsum(jax.nn.softmax(hgrn_lb_logits.astype(f32), axis=0), axis=0)
    lower_bounds = lb_cum - lb_cum[:1]

    def run_layer(l, x, c, s_hgrn, s_gla, s5_re, s5_im):
        mod = jax.nn.silu(c) @ ada_w[l] + ada_b[l]
        sh1, sc1, g1, sh2, sc2, g2 = jnp.split(mod, 6, axis=-1)
        h = modulate(rmsnorm(x, norm1_g[l]), sh1, sc1)
        o, s_hgrn, s_gla, s5_re, s5_im = token_mixers(
            h, s_hgrn, s_gla, s5_re, s5_im, w_in[l], lower_bounds[l], hgrn_norm_g[l], gla_w_alpha[l],
            gla_b_alpha[l], gla_norm_g[l], s5_a_re[l], s5_a_im[l], s5_b_re[l], s5_b_im[l], s5_c_re[l],
            s5_c_im[l], s5_d[l], s5_log_dt[l], s5_w_glu[l], s5_b_glu[l], w_out[l])
        x = x + g1[:, None, :] * o
        h = modulate(rmsnorm(x, norm2_g[l]), sh2, sc2)
        x = x + g2[:, None, :] * hier_moe(h, moe_w_group[l], moe_b_group[l], moe_w_expert[l],
                                          moe_b_expert[l], moe_w_gate[l], moe_w_up[l], moe_w_down[l])
        return x, s_hgrn, s_gla, s5_re, s5_im

    bp = x_prompt.shape[0]
    xp, xs = x_prompt, x_sample
    hg_p, gl_p, sr_p, si_p = [], [], [], []
    hg_s, gl_s, sr_s, si_s = [], [], [], []
    for l in range(DEPTH):
        xp, a, b, cr, ci = run_layer(
            l, xp, c_prompt, jnp.zeros((bp, A_HEADS, A_DK, A_DV), f32), jnp.zeros((bp, B_HEADS, B_DK, B_DV), f32),
            jnp.zeros((bp, C_GROUPS, C_STATE), f32), jnp.zeros((bp, C_GROUPS, C_STATE), f32))
        hg_p.append(a); gl_p.append(b); sr_p.append(cr); si_p.append(ci)
        xs, a, b, cr, ci = run_layer(l, xs, c_sample, state_hgrn[l], state_gla[l], state_s5_re[l], state_s5_im[l])
        hg_s.append(a); gl_s.append(b); sr_s.append(cr); si_s.append(ci)

    y_prompt = rmsnorm(xp, final_norm_g)
    y_sample = rmsnorm(xs, final_norm_g)
    return (y_prompt, y_sample,
            jnp.stack(hg_p), jnp.stack(gl_p), jnp.stack(sr_p), jnp.stack(si_p),
            jnp.stack(hg_s), jnp.stack(gl_s), jnp.stack(sr_s), jnp.stack(si_s))
```

```python
import functools
import math

import numpy as np
import jax
import jax.numpy as jnp
from jax import lax
from jax.experimental import pallas as pl
from jax.experimental.pallas import tpu as pltpu

F32 = jnp.float32
BF16 = jnp.bfloat16

D = 1024
DEPTH = 4
A_H, A_DK, A_DV = 6, 64, 64
B_H, B_DK, B_DV = 6, 32, 64
B_RANK = 16
B_TAU = 16.0
C_G, C_CH, C_P = 16, 16, 64
C_W = C_G * C_CH
C_S = C_G * C_P
N_GROUPS, N_PER_GROUP, D_EXP = 4, 4, 256
N_EXP = N_GROUPS * N_PER_GROUP
EPS = 1e-6
CHUNK = 64
LANES = 128
EXP_CLAMP = 80.0

QA, FA, VA, GA = 0, 384, 768, 1152
QB, KB, VB, RB = 1536, 1792, 2048, 2432
CU, LR = 2816, 3072
PACK = 3200
LB, KA = 3072, 3328
PW = 3712

VMEM_LIMIT = 56 * 1024 * 1024


def _sigmoid(x):
    return 1.0 / (1.0 + jnp.exp(-x))


def _silu(x):
    return x * _sigmoid(x)


def _log_sigmoid(x):
    return jnp.minimum(x, 0.0) - jnp.log1p(jnp.exp(-jnp.abs(x)))


def _gelu_tanh(x):
    return 0.5 * x * (1.0 + jnp.tanh(math.sqrt(2.0 / math.pi) * (x + 0.044715 * x * x * x)))


def _split3(x):
    hi = x.astype(BF16)
    r = x - hi.astype(F32)
    mid = r.astype(BF16)
    lo = (r - mid.astype(F32)).astype(BF16)
    return hi, mid, lo


def _dot(a, b):
    return jnp.dot(a, b, preferred_element_type=F32)


def _dot_nt(a, b):
    return lax.dot_general(a, b, (((1,), (1,)), ((), ())), preferred_element_type=F32)


def _dot_tn(a, b):
    return lax.dot_general(a, b, (((0,), (0,)), ((), ())), preferred_element_type=F32)


def _exact_01_dot(m01, x):
    hi, mid, lo = _split3(x)
    return _dot(m01, hi) + _dot(m01, mid) + _dot(m01, lo)


def _rms_rows(x):
    return x * lax.rsqrt(jnp.mean(x * x, axis=-1, keepdims=True) + EPS)


def _forget_gate(af, lbv):
    e = jnp.exp(-jnp.abs(af))
    log_sig = jnp.minimum(af, 0.0) - jnp.log1p(e)
    a = lbv[0:1]
    b = lbv[1:2] + log_sig
    log_f = jnp.maximum(a, b) + jnp.log1p(jnp.exp(-jnp.abs(a - b)))
    k = lbv[2:3] * (jnp.where(af >= 0.0, e, 1.0) / (1.0 + e))
    return log_f, k


def _chunk_block(q, k, v, bcum, st, nh, dk, dv):
    c = q.shape[0]
    vw = nh * dv
    lane = lax.broadcasted_iota(jnp.int32, (c, LANES), 1)
    vlane = lax.broadcasted_iota(jnp.int32, (c, vw), 1)
    b_mid = bcum[c // 2 - 1:c // 2, :]
    b_end = bcum[c - 1:c, :]
    qt = (q * jnp.exp(jnp.minimum(bcum - b_mid, EXP_CLAMP))).astype(BF16)
    kt = k * jnp.exp(jnp.minimum(b_mid - bcum, EXP_CLAMP))
    kcat = jnp.concatenate(
        [jnp.where(lane // dk == h, kt, 0.0) for h in range(nh)], axis=0).astype(BF16)
    sc = _dot_nt(qt, kcat)
    row = lax.broadcasted_iota(jnp.int32, (c, nh * c), 0)
    col = lax.broadcasted_iota(jnp.int32, (c, nh * c), 1)
    sc = jnp.where(col % c <= row, sc, 0.0).astype(BF16)
    vcat = jnp.concatenate(
        [jnp.where(vlane // dv == h, v, 0.0) for h in range(nh)], axis=0).astype(BF16)
    o = _dot(sc, vcat)
    qd = (q * jnp.exp(bcum)).astype(BF16)
    o = o + _dot_nt(qd, st.astype(BF16))
    kd = (k * jnp.exp(b_end - bcum)).astype(BF16)
    upd = _dot_tn(v.astype(BF16), kd)
    srow = lax.broadcasted_iota(jnp.int32, (vw, LANES), 0)
    scol = lax.broadcasted_iota(jnp.int32, (vw, LANES), 1)
    st_new = st * jnp.exp(b_end) + jnp.where(srow // dv == scol // dk, upd, 0.0)
    return o, st_new


def _head_rms(o, dv):
    w = o.shape[1]
    r = lax.broadcasted_iota(jnp.int32, (w, w), 0)
    cidx = lax.broadcasted_iota(jnp.int32, (w, w), 1)
    ones_blk = jnp.where(r // dv == cidx // dv, 1.0, 0.0).astype(BF16)
    hi, mid, lo = _split3(o * o)
    tot = _dot(hi, ones_blk) + _dot(mid, ones_blk) + _dot(lo, ones_blk)
    return o * lax.rsqrt(tot * (1.0 / dv) + EPS)


def _ada_kernel(c_ref, w_ref, b_ref, o_ref):
    c = c_ref[...]
    sc = _silu(c).astype(BF16)
    o_ref[0] = _dot(sc, w_ref[0].astype(BF16)) + b_ref[0]


def _ada_call(c_all, ada_w, ada_b):
    n = c_all.shape[0]
    tn = 1536
    return pl.pallas_call(
        _ada_kernel,
        grid=(DEPTH, 6 * D // tn),
        in_specs=[
            pl.BlockSpec((n, D), lambda l, j: (0, 0)),
            pl.BlockSpec((1, D, tn), lambda l, j: (l, 0, j)),
            pl.BlockSpec((1, 1, tn), lambda l, j: (l, 0, j)),
        ],
        out_specs=pl.BlockSpec((1, n, tn), lambda l, j: (l, 0, j)),
        out_shape=jax.ShapeDtypeStruct((DEPTH, n, 6 * D), F32),
        compiler_params=pltpu.CompilerParams(
            dimension_semantics=("arbitrary", "arbitrary"), vmem_limit_bytes=VMEM_LIMIT),
        name="ada_mod",
    )(c_all, ada_w, ada_b.reshape(DEPTH, 1, 6 * D))


def _project_all(h, win_ref, lbv_ref, walpha_ref, balpha_ref, tri, p_scr):
    def proj(c0, w):
        return _dot(h, win_ref[:, c0:c0 + w])

    def cum(x):
        return x if tri is None else _exact_01_dot(tri, x)

    p_scr[:, QA:QA + 384] = _silu(proj(QA, 384))
    log_f, ka = _forget_gate(proj(FA, 384), lbv_ref[...])
    p_scr[:, FA:FA + 384] = cum(log_f)
    p_scr[:, KA:KA + 384] = ka
    p_scr[:, VA:VA + 384] = proj(VA, 384)
    p_scr[:, GA:GA + 384] = _silu(proj(GA, 384))
    p_scr[:, QB:QB + 256] = proj(QB, 256) * (B_DK ** -0.5)
    p_scr[:, KB:KB + 256] = proj(KB, 256)
    p_scr[:, VB:VB + 384] = proj(VB, 384)
    p_scr[:, RB:RB + 384] = _silu(proj(RB, 384))
    z = _dot(proj(LR, LANES).astype(BF16), walpha_ref[...]) + balpha_ref[...]
    p_scr[:, LB:LB + 256] = cum(_log_sigmoid(z) * (1.0 / B_TAU))
    p_scr[:, CU:CU + 256] = proj(CU, 256)


def _mixer_prompt_kernel(x_ref, mod_ref, g1_ref, win_ref, lbv_ref, walpha_ref, balpha_ref,
                         hg_ref, gg_ref, abar_ref, bb_ref, cc_ref, dsk_ref, wglu_ref, bglu_ref,
                         wout_ref, tri_ref,
                         xo_ref, sta_ref, stb_ref, s5r_ref, s5i_ref,
                         h_scr, p_scr, o_scr, xr_scr, xi_scr, t_scr):
    nb = x_ref.shape[0]
    c = CHUNK
    rows = nb * c
    i = pl.program_id(0)

    @pl.when(i == 0)
    def _():
        sta_ref[...] = jnp.zeros_like(sta_ref)
        stb_ref[...] = jnp.zeros_like(stb_ref)
        s5r_ref[...] = jnp.zeros_like(s5r_ref)
        s5i_ref[...] = jnp.zeros_like(s5i_ref)

    for b in range(nb):
        hb = _rms_rows(x_ref[b]) * g1_ref[...]
        hb = hb * (1.0 + mod_ref[b:b + 1, D:2 * D]) + mod_ref[b:b + 1, 0:D]
        h_scr[b * c:(b + 1) * c, :] = hb.astype(BF16)

    _project_all(h_scr[...], win_ref, lbv_ref, walpha_ref, balpha_ref, tri_ref[...], p_scr)

    def attn_b(b, carry):
        rs = pl.ds(pl.multiple_of(b * c, c), c)
        for p in range(A_H // 2):
            l0 = LANES * p
            o, st = _chunk_block(p_scr[rs, QA + l0:QA + l0 + LANES], p_scr[rs, KA + l0:KA + l0 + LANES],
                                 p_scr[rs, VA + l0:VA + l0 + LANES], p_scr[rs, FA + l0:FA + l0 + LANES],
                                 sta_ref[b, l0:l0 + LANES, :], 2, A_DK, A_DV)
            sta_ref[b, l0:l0 + LANES, :] = st
            o = _head_rms(o, A_DV) * hg_ref[...] * p_scr[rs, GA + l0:GA + l0 + LANES]
            o_scr[rs, l0:l0 + LANES] = o.astype(BF16)
        for j, nh in ((0, 4), (1, 2)):
            l0 = LANES * j
            v0 = 256 * j
            vw = nh * B_DV
            o, st = _chunk_block(p_scr[rs, QB + l0:QB + l0 + LANES], p_scr[rs, KB + l0:KB + l0 + LANES],
                                 p_scr[rs, VB + v0:VB + v0 + vw], p_scr[rs, LB + l0:LB + l0 + LANES],
                                 stb_ref[b, v0:v0 + vw, :], nh, B_DK, B_DV)
            stb_ref[b, v0:v0 + vw, :] = st
            o = _head_rms(o, B_DV) * gg_ref[:, 0:vw] * p_scr[rs, RB + v0:RB + v0 + vw]
            o_scr[rs, 384 + v0:384 + v0 + vw] = o.astype(BF16)
        return carry

    lax.fori_loop(0, nb, attn_b, 0)

    halves = C_W // LANES
    for b in range(nb):
        for j in range(halves):
            t_scr[j, b:b + nb * c:nb, :] = p_scr[b * c:(b + 1) * c, CU + LANES * j:CU + LANES * (j + 1)]
    u = jnp.concatenate([t_scr[j] for j in range(halves)], axis=1)
    xri = _dot(u.astype(BF16), bb_ref[...])
    xr_scr[...] = xri[:, 0:C_S]
    xi_scr[...] = xri[:, C_S:2 * C_S]
    ar = abar_ref[0:1, :]
    ai = abar_ref[1:2, :]

    def scan_t(t, carry):
        sr, si = carry
        ts = pl.ds(pl.multiple_of(t * nb, nb), nb)
        nr = ar * sr - ai * si + xr_scr[ts, :]
        ni = ar * si + ai * sr + xi_scr[ts, :]
        xr_scr[ts, :] = nr
        xi_scr[ts, :] = ni
        return nr, ni

    sr, si = lax.fori_loop(0, c, scan_t, (s5r_ref[...], s5i_ref[...]))
    s5r_ref[...] = sr
    s5i_ref[...] = si
    y = (_dot(xr_scr[...].astype(BF16), cc_ref[0:C_S, :])
         + _dot(xi_scr[...].astype(BF16), cc_ref[C_S:2 * C_S, :]) + dsk_ref[...] * u)
    z = _gelu_tanh(y)
    oc = z * _sigmoid(_dot(z.astype(BF16), wglu_ref[...]) + bglu_ref[...])
    for j in range(halves):
        t_scr[j] = oc[:, LANES * j:LANES * (j + 1)]
    for b in range(nb):
        for j in range(halves):
            o_scr[b * c:(b + 1) * c, 768 + LANES * j:768 + LANES * (j + 1)] = (
                t_scr[j, b:b + nb * c:nb, :].astype(BF16))

    out = _dot(o_scr[...], wout_ref[...])
    for b in range(nb):
        xo_ref[b] = x_ref[b] + mod_ref[b:b + 1, 2 * D:3 * D] * out[b * c:(b + 1) * c, :]


def _const_spec(shape):
    nd = len(shape)
    return pl.BlockSpec(shape, lambda i, _n=nd: (0,) * _n, pipeline_mode=pl.Buffered(1))


def _mixer_prompt_call(x, mod_p, lp):
    nb, seq, _ = x.shape
    c = CHUNK
    rows = nb * c
    tri = np.kron(np.eye(nb, dtype=np.float32), np.tril(np.ones((c, c), np.float32)))
    consts = [lp["g1"], lp["w_in"], lp["lbv"], lp["w_alpha"], lp["b_alpha"], lp["hgrn_g"], lp["gla_g"],
              lp["abar"], lp["bb"], lp["cc"], lp["d_skip"], lp["w_glu"], lp["b_glu"], lp["w_out"],
              jnp.asarray(tri, BF16)]
    in_specs = ([pl.BlockSpec((nb, c, D), lambda i: (0, i, 0)), _const_spec(mod_p.shape)]
                + [_const_spec(a.shape) for a in consts])
    out_shape = (jax.ShapeDtypeStruct((nb, seq, D), F32),
                 jax.ShapeDtypeStruct((nb, 384, LANES), F32),
                 jax.ShapeDtypeStruct((nb, 384, LANES), F32),
                 jax.ShapeDtypeStruct((nb, C_S), F32),
                 jax.ShapeDtypeStruct((nb, C_S), F32))
    out_specs = (pl.BlockSpec((nb, c, D), lambda i: (0, i, 0)),
                 pl.BlockSpec((nb, 384, LANES), lambda i: (0, 0, 0)),
                 pl.BlockSpec((nb, 384, LANES), lambda i: (0, 0, 0)),
                 pl.BlockSpec((nb, C_S), lambda i: (0, 0)),
                 pl.BlockSpec((nb, C_S), lambda i: (0, 0)))
    scratch = [pltpu.VMEM((rows, D), BF16), pltpu.VMEM((rows, PW), F32), pltpu.VMEM((rows, D), BF16),
               pltpu.VMEM((rows, C_S), F32), pltpu.VMEM((rows, C_S), F32),
               pltpu.VMEM((C_W // LANES, rows, LANES), F32)]
    return pl.pallas_call(
        _mixer_prompt_kernel,
        grid=(seq // c,),
        in_specs=in_specs,
        out_specs=out_specs,
        out_shape=out_shape,
        scratch_shapes=scratch,
        compiler_params=pltpu.CompilerParams(
            dimension_semantics=("arbitrary",), vmem_limit_bytes=VMEM_LIMIT),
        name="mixer_prompt",
    )(x, mod_p, *consts)


def _step_head(s_ref, so_ref, dec, kk, qq, v, nr):
    n = v.shape[0]
    lane = lax.broadcasted_iota(jnp.int32, (n, LANES), 1)
    low = lane < 64
    v2 = jnp.concatenate([v, v], axis=1)
    acc = jnp.zeros((n, LANES), F32)
    for r in range(nr):
        def col2(a):
            return jnp.where(low, a[:, 2 * r:2 * r + 1], a[:, 2 * r + 1:2 * r + 2])
        new = s_ref[0, :, r, :] * col2(dec) + col2(kk) * v2
        so_ref[0, :, r, :] = new
        acc = acc + new * col2(qq)
    return acc[:, 0:64] + acc[:, 64:128]


def _mixer_sample_kernel(x_ref, mod_ref, g1_ref, win_ref, lbv_ref, walpha_ref, balpha_ref,
                         hg_ref, gg_ref, abar_ref, bb_ref, cc_ref, dsk_ref, wglu_ref, bglu_ref,
                         wout_ref, sa_ref, sb_ref, s5r_ref, s5i_ref,
                         xo_ref, sao_ref, sbo_ref, s5ro_ref, s5io_ref,
                         p_scr, ah_scr, bh_scr, bv_scr, gate_scr, oh_scr, oc_scr):
    i = pl.program_id(0)
    n = x_ref.shape[0]

    @pl.when(i == 0)
    def _():
        h = _rms_rows(x_ref[...]) * g1_ref[...]
        h = (h * (1.0 + mod_ref[:, D:2 * D]) + mod_ref[:, 0:D]).astype(BF16)
        _project_all(h, win_ref, lbv_ref, walpha_ref, balpha_ref, None, p_scr)
        for hh in range(A_H):
            s = slice(64 * hh, 64 * hh + 64)
            ah_scr[0, hh] = p_scr[:, QA + s.start:QA + s.stop]
            ah_scr[1, hh] = jnp.exp(p_scr[:, FA + s.start:FA + s.stop])
            ah_scr[2, hh] = p_scr[:, KA + s.start:KA + s.stop]
            ah_scr[3, hh] = p_scr[:, VA + s.start:VA + s.stop]
            gate_scr[0, hh] = p_scr[:, GA + s.start:GA + s.stop]
            gate_scr[1, hh] = p_scr[:, RB + s.start:RB + s.stop]
            bv_scr[hh] = p_scr[:, VB + s.start:VB + s.stop]
            s2 = slice(32 * hh, 32 * hh + 32)
            bh_scr[0, hh] = p_scr[:, QB + s2.start:QB + s2.stop]
            bh_scr[1, hh] = jnp.exp(p_scr[:, LB + s2.start:LB + s2.stop])
            bh_scr[2, hh] = p_scr[:, KB + s2.start:KB + s2.stop]
        u = p_scr[:, CU:CU + C_W]
        xri = _dot(u.astype(BF16), bb_ref[...])
        ar = abar_ref[0:1, :]
        ai = abar_ref[1:2, :]
        s0r = s5r_ref[...]
        s0i = s5i_ref[...]
        sr = ar * s0r - ai * s0i + xri[:, 0:C_S]
        si = ar * s0i + ai * s0r + xri[:, C_S:2 * C_S]
        s5ro_ref[...] = sr
        s5io_ref[...] = si
        y = (_dot(sr.astype(BF16), cc_ref[0:C_S, :]) + _dot(si.astype(BF16), cc_ref[C_S:2 * C_S, :])
             + dsk_ref[...] * u)
        z = _gelu_tanh(y)
        oc_scr[...] = z * _sigmoid(_dot(z.astype(BF16), wglu_ref[...]) + bglu_ref[...])

    oa = _step_head(sa_ref, sao_ref, ah_scr[1, i], ah_scr[2, i], ah_scr[0, i], ah_scr[3, i], A_DK // 2)
    oh_scr[0, i] = _rms_rows(oa) * hg_ref[:, 0:64] * gate_scr[0, i]
    ob = _step_head(sb_ref, sbo_ref, bh_scr[1, i], bh_scr[2, i], bh_scr[0, i], bv_scr[i], B_DK // 2)
    oh_scr[1, i] = _rms_rows(ob) * gg_ref[:, 0:64] * gate_scr[1, i]

    @pl.when(i == A_H - 1)
    def _():
        o = jnp.concatenate([oh_scr[0, hh] for hh in range(A_H)] + [oh_scr[1, hh] for hh in range(B_H)]
                            + [oc_scr[...]], axis=1).astype(BF16)
        xo_ref[...] = x_ref[...] + mod_ref[:, 2 * D:3 * D] * _dot(o, wout_ref[...])


def _mixer_sample_call(l, x, mod_s, lp, st_a, st_b, st_r, st_i):
    n = x.shape[0]
    consts = [lp["g1"], lp["w_in"], lp["lbv"], lp["w_alpha"], lp["b_alpha"], lp["hgrn_g"], lp["gla_g"],
              lp["abar"], lp["bb"], lp["cc"], lp["d_skip"], lp["w_glu"], lp["b_glu"], lp["w_out"]]
    ra, rb = A_DK // 2, B_DK // 2
    in_specs = ([_const_spec(x.shape), _const_spec(mod_s.shape)] + [_const_spec(a.shape) for a in consts]
                + [pl.BlockSpec((1, n, ra, LANES), lambda i: (l, 0, i, 0)),
                   pl.BlockSpec((1, n, rb, LANES), lambda i: (l, 0, i, 0)),
                   pl.BlockSpec((None, n, C_S), lambda i: (l, 0, 0)),
                   pl.BlockSpec((None, n, C_S), lambda i: (l, 0, 0))])
    out_shape = (jax.ShapeDtypeStruct((n, D), F32),
                 jax.ShapeDtypeStruct((1, n, A_H * ra, LANES), F32),
                 jax.ShapeDtypeStruct((1, n, B_H * rb, LANES), F32),
                 jax.ShapeDtypeStruct((n, C_S), F32),
                 jax.ShapeDtypeStruct((n, C_S), F32))
    out_specs = (pl.BlockSpec((n, D), lambda i: (0, 0)),
                 pl.BlockSpec((1, n, ra, LANES), lambda i: (0, 0, i, 0)),
                 pl.BlockSpec((1, n, rb, LANES), lambda i: (0, 0, i, 0)),
                 pl.BlockSpec((n, C_S), lambda i: (0, 0)),
                 pl.BlockSpec((n, C_S), lambda i: (0, 0)))
    scratch = [pltpu.VMEM((n, PW), F32), pltpu.VMEM((4, A_H, n, 64), F32), pltpu.VMEM((3, B_H, n, 32), F32),
               pltpu.VMEM((B_H, n, 64), F32), pltpu.VMEM((2, A_H, n, 64), F32),
               pltpu.VMEM((2, A_H, n, 64), F32), pltpu.VMEM((n, C_W), F32)]
    return pl.pallas_call(
        _mixer_sample_kernel,
        grid=(A_H,),
        in_specs=in_specs,
        out_specs=out_specs,
        out_shape=out_shape,
        scratch_shapes=scratch,
        compiler_params=pltpu.CompilerParams(
            dimension_semantics=("arbitrary",), vmem_limit_bytes=VMEM_LIMIT),
        name="mixer_sample",
    )(x, mod_s, *consts, st_a, st_b, st_r, st_i)


def _moe_kernel(x_ref, mod_ref, g2_ref, wr_ref, br_ref, wgu_ref, wd_ref, fg_ref,
                o_ref, h_scr, gate_scr, acc_scr, *, per_row, final_norm):
    g = pl.program_id(1)
    tm = x_ref.shape[0]
    def mod(k):
        return mod_ref[:, k * D:(k + 1) * D] if per_row else mod_ref[0, :, k * D:(k + 1) * D]

    lane = lax.broadcasted_iota(jnp.int32, (tm, LANES), 1).astype(F32)
    big = jnp.float32(1 << 20)
    neg = jnp.float32(-jnp.inf)

    @pl.when(g == 0)
    def _():
        h = _rms_rows(x_ref[...]) * g2_ref[...]
        h = (h * (1.0 + mod(4)) + mod(3)).astype(BF16)
        h_scr[...] = h
        logits = _dot(h, wr_ref[...]) + br_ref[...]
        gl = jnp.where(lane < N_GROUPS, logits, neg)
        gmax = jnp.max(gl, axis=1, keepdims=True)
        gidx = jnp.min(jnp.where(gl == gmax, lane, big), axis=1, keepdims=True)
        p_group = 1.0 / jnp.sum(jnp.where(lane < N_GROUPS, jnp.exp(logits - gmax), 0.0),
                                axis=1, keepdims=True)
        e0 = N_GROUPS + N_PER_GROUP * gidx
        el = jnp.where((lane >= e0) & (lane < e0 + N_PER_GROUP), logits, neg)
        v1 = jnp.max(el, axis=1, keepdims=True)
        i1 = jnp.min(jnp.where(el == v1, lane, big), axis=1, keepdims=True)
        el2 = jnp.where(lane == i1, neg, el)
        v2 = jnp.max(el2, axis=1, keepdims=True)
        i2 = jnp.min(jnp.where(el2 == v2, lane, big), axis=1, keepdims=True)
        ex = jnp.exp(v2 - v1)
        w1 = 1.0 / (1.0 + ex)
        gate_scr[...] = (jnp.where(lane == i1, p_group * w1, 0.0)
                         + jnp.where(lane == i2, p_group * (ex * w1), 0.0))
        acc_scr[...] = jnp.zeros_like(acc_scr)

    h = h_scr[...]
    a = _dot(h, wgu_ref[0])
    hw = N_PER_GROUP * D_EXP
    hid = _silu(a[:, 0:hw]) * a[:, hw:2 * hw]
    gates = gate_scr[...]
    lane0 = (N_GROUPS + N_PER_GROUP * g).astype(F32)
    gcols = [jnp.sum(jnp.where(lane == lane0 + e, gates, 0.0), axis=1, keepdims=True)
             for e in range(N_PER_GROUP)]
    gexp = jnp.concatenate([jnp.broadcast_to(gc, (tm, D_EXP)) for gc in gcols], axis=1)
    acc_scr[...] += _dot((hid * gexp).astype(BF16), wd_ref[0])

    @pl.when(g == N_GROUPS - 1)
    def _():
        xn = x_ref[...] + mod(5) * acc_scr[...]
        if final_norm:
            xn = _rms_rows(xn) * fg_ref[...]
        o_ref[...] = xn


def _moe_call(x2, mod3, lp, final_g, *, tm, per_row, final_norm):
    t = x2.shape[0]
    nt = t // tm
    if per_row:
        mod_spec = pl.BlockSpec((tm, 6 * D), lambda i, g: (i, 0))
    else:
        assert nt % mod3.shape[0] == 0, "a prompt token tile must not span two sequences"
        tiles_per_seq = nt // mod3.shape[0]
        mod_spec = pl.BlockSpec((1, 1, 6 * D), lambda i, g: (i // tiles_per_seq, 0, 0))
    hw = N_PER_GROUP * D_EXP
    kern = functools.partial(_moe_kernel, per_row=per_row, final_norm=final_norm)
    return pl.pallas_call(
        kern,
        grid=(nt, N_GROUPS),
        in_specs=[
            pl.BlockSpec((tm, D), lambda i, g: (i, 0)),
            mod_spec,
            pl.BlockSpec((1, D), lambda i, g: (0, 0)),
            pl.BlockSpec((D, LANES), lambda i, g: (0, 0)),
            pl.BlockSpec((1, LANES), lambda i, g: (0, 0)),
            pl.BlockSpec((1, D, 2 * hw), lambda i, g: (g, 0, 0)),
            pl.BlockSpec((1, hw, D), lambda i, g: (g, 0, 0)),
            pl.BlockSpec((1, D), lambda i, g: (0, 0)),
        ],
        out_specs=pl.BlockSpec((tm, D), lambda i, g: (i, 0)),
        out_shape=jax.ShapeDtypeStruct((t, D), F32),
        scratch_shapes=[pltpu.VMEM((tm, D), BF16), pltpu.VMEM((tm, LANES), F32), pltpu.VMEM((tm, D), F32)],
        compiler_params=pltpu.CompilerParams(
            dimension_semantics=("arbitrary", "arbitrary"), vmem_limit_bytes=VMEM_LIMIT),
        name="moe_sample" if per_row else "moe_prompt",
    )(x2, mod3, lp["g2"], lp["w_router"], lp["b_router"], lp["w_gu"], lp["w_d"], final_g)


def _pad_cols(w, width):
    return jnp.pad(w, ((0, 0), (0, width - w.shape[1])))


def _layer_params(l, p, lower_bounds):
    w = p["w_in"][l]
    w_in = jnp.concatenate([
        w[:, 0:1536], _pad_cols(w[:, 1536:1728], 256), _pad_cols(w[:, 1728:1920], 256),
        w[:, 1920:2304], w[:, 2320:2704], w[:, 2704:2960], _pad_cols(w[:, 2304:2320], LANES)],
        axis=1).astype(BF16)
    lb = lower_bounds[l]
    lbv = jnp.zeros((8, 384), F32).at[0].set(jnp.log(lb)).at[1].set(jnp.log1p(-lb)).at[2].set(1.0 - lb)
    w_alpha = jnp.zeros((LANES, 256), F32).at[0:B_RANK, 0:192].set(p["gla_w_alpha"][l]).astype(BF16)
    b_alpha = jnp.zeros((1, 256), F32).at[0, 0:192].set(p["gla_b_alpha"][l])
    a_re, a_im = p["s5_a_re"][l], p["s5_a_im"][l]
    dt = jnp.exp(p["s5_log_dt"][l])[:, None]
    mag = jnp.exp(a_re * dt)
    abar_re = mag * jnp.cos(a_im * dt)
    abar_im = mag * jnp.sin(a_im * dt)
    den = a_re * a_re + a_im * a_im
    nr, ni = abar_re - 1.0, abar_im
    coef_re = ((nr * a_re + ni * a_im) / den)[..., None]
    coef_im = ((ni * a_re - nr * a_im) / den)[..., None]
    b_re, b_im = p["s5_b_re"][l], p["s5_b_im"][l]
    bb_re = coef_re * b_re - coef_im * b_im
    bb_im = coef_re * b_im + coef_im * b_re
    eye = jnp.eye(C_G, dtype=F32)
    blk_b = lambda m: jnp.einsum("gph,gk->ghkp", m, eye).reshape(C_W, C_S)
    blk_c = lambda m: jnp.einsum("ghp,gk->gpkh", m, eye).reshape(C_S, C_W)
    bb = jnp.concatenate([blk_b(bb_re), blk_b(bb_im)], axis=1).astype(BF16)
    cc = jnp.concatenate([blk_c(p["s5_c_re"][l]), -blk_c(p["s5_c_im"][l])], axis=0).astype(BF16)
    abar = jnp.zeros((8, C_S), F32).at[0].set(abar_re.reshape(-1)).at[1].set(abar_im.reshape(-1))
    def grp(wt):
        return wt.reshape(N_GROUPS, N_PER_GROUP, D, D_EXP).transpose(0, 2, 1, 3).reshape(N_GROUPS, D, -1)
    w_gu = jnp.concatenate([grp(p["moe_w_gate"][l]), grp(p["moe_w_up"][l])], axis=2).astype(BF16)
    w_d = p["moe_w_down"][l].reshape(N_GROUPS, N_PER_GROUP * D_EXP, D).astype(BF16)
    w_router = jnp.zeros((D, LANES), F32).at[:, 0:N_GROUPS].set(p["moe_w_group"][l]) \
        .at[:, N_GROUPS:N_GROUPS + N_EXP].set(p["moe_w_expert"][l]).astype(BF16)
    b_router = jnp.zeros((1, LANES), F32).at[0, 0:N_GROUPS].set(p["moe_b_group"][l]) \
        .at[0, N_GROUPS:N_GROUPS + N_EXP].set(p["moe_b_expert"][l])
    return dict(
        g1=p["norm1_g"][l].reshape(1, D), g2=p["norm2_g"][l].reshape(1, D), w_in=w_in, lbv=lbv,
        w_alpha=w_alpha, b_alpha=b_alpha,
        hgrn_g=jnp.tile(p["hgrn_norm_g"][l], 2).reshape(1, LANES),
        gla_g=jnp.tile(p["gla_norm_g"][l], 4).reshape(1, 256),
        abar=abar, bb=bb, cc=cc, d_skip=p["s5_d"][l].reshape(1, C_W),
        w_glu=p["s5_w_glu"][l].astype(BF16), b_glu=p["s5_b_glu"][l].reshape(1, C_W),
        w_out=p["w_out"][l].astype(BF16), w_gu=w_gu, w_d=w_d, w_router=w_router, b_router=b_router)


def kernel(x_prompt, x_sample, state_hgrn, state_gla, state_s5_re, state_s5_im, c_prompt, c_sample, ada_w, ada_b, norm1_g, norm2_g, w_in, hgrn_lb_logits, hgrn_norm_g, gla_w_alpha, gla_b_alpha, gla_norm_g, s5_a_re, s5_a_im, s5_b_re, s5_b_im, s5_c_re, s5_c_im, s5_d, s5_log_dt, s5_w_glu, s5_b_glu, w_out, moe_w_group, moe_b_group, moe_w_expert, moe_b_expert, moe_w_gate, moe_w_up, moe_w_down, final_norm_g):
    p = dict(norm1_g=norm1_g, norm2_g=norm2_g, w_in=w_in, hgrn_norm_g=hgrn_norm_g, gla_w_alpha=gla_w_alpha,
             gla_b_alpha=gla_b_alpha, gla_norm_g=gla_norm_g, s5_a_re=s5_a_re, s5_a_im=s5_a_im, s5_b_re=s5_b_re,
             s5_b_im=s5_b_im, s5_c_re=s5_c_re, s5_c_im=s5_c_im, s5_d=s5_d, s5_log_dt=s5_log_dt,
             s5_w_glu=s5_w_glu, s5_b_glu=s5_b_glu, w_out=w_out, moe_w_group=moe_w_group,
             moe_b_group=moe_b_group, moe_w_expert=moe_w_expert, moe_b_expert=moe_b_expert,
             moe_w_gate=moe_w_gate, moe_w_up=moe_w_up, moe_w_down=moe_w_down)
    nb, seq, _ = x_prompt.shape
    ns = x_sample.shape[0]
    lb_cum = jnp.cumsum(jax.nn.softmax(hgrn_lb_logits.astype(F32), axis=0), axis=0)
    lower_bounds = lb_cum - lb_cum[:1]
    final_g = final_norm_g.reshape(1, D)

    mod = _ada_call(jnp.concatenate([c_prompt, c_sample], axis=0), ada_w, ada_b)

    sa = state_hgrn.reshape(DEPTH, ns, A_H * A_DK // 2, LANES)
    sb = state_gla.reshape(DEPTH, ns, B_H * B_DK // 2, LANES)

    xp = x_prompt
    xs = x_sample.reshape(ns, D)
    outs = {k: [] for k in ("hg_p", "gl_p", "sr_p", "si_p", "hg_s", "gl_s", "sr_s", "si_s")}
    for l in range(DEPTH):
        lp = _layer_params(l, p, lower_bounds)
        last = l == DEPTH - 1
        mod_p = mod[l, 0:nb]
        mod_s = mod[l, nb:nb + ns]
        xp, sta, stb, s5r, s5i = _mixer_prompt_call(xp, mod_p, lp)
        xp = _moe_call(xp.reshape(nb * seq, D), mod_p.reshape(nb, 1, 6 * D), lp, final_g,
                       tm=512, per_row=False, final_norm=last).reshape(nb, seq, D)
        sta = sta.reshape(nb, A_H // 2, 2, A_DV, 2, A_DK)
        outs["hg_p"].append(jnp.stack([sta[:, h // 2, h % 2, :, h % 2, :] for h in range(A_H)], axis=1)
                            .transpose(0, 1, 3, 2))
        stb = stb.reshape(nb, B_H, B_DV, 4, B_DK)
        outs["gl_p"].append(jnp.stack([stb[:, h, :, h % 4, :] for h in range(B_H)], axis=1)
                            .transpose(0, 1, 3, 2))
        outs["sr_p"].append(s5r.reshape(nb, C_G, C_P))
        outs["si_p"].append(s5i.reshape(nb, C_G, C_P))
        xs, sao, sbo, s5ro, s5io = _mixer_sample_call(l, xs, mod_s, lp, sa, sb, state_s5_re.reshape(DEPTH, ns, C_S),
                                                      state_s5_im.reshape(DEPTH, ns, C_S))
        xs = _moe_call(xs, mod_s, lp, final_g, tm=ns, per_row=True, final_norm=last)
        outs["hg_s"].append(sao.reshape(ns, A_H, A_DK, A_DV))
        outs["gl_s"].append(sbo.reshape(ns, B_H, B_DK, B_DV))
        outs["sr_s"].append(s5ro.reshape(ns, C_G, C_P))
        outs["si_s"].append(s5io.reshape(ns, C_G, C_P))

    return (xp, xs.reshape(ns, 1, D),
            jnp.stack(outs["hg_p"]), jnp.stack(outs["gl_p"]), jnp.stack(outs["sr_p"]), jnp.stack(outs["si_p"]),
            jnp.stack(outs["hg_s"]), jnp.stack(outs["gl_s"]), jnp.stack(outs["sr_s"]), jnp.stack(outs["si_s"]))
```

```python
import functools
import math

import jax
import jax.numpy as jnp
from jax import lax
from jax.experimental import pallas as pl
from jax.experimental.pallas import tpu as pltpu

F32 = jnp.float32
BF16 = jnp.bfloat16

D = 1024
DEPTH = 4
A_H, A_DK, A_DV = 6, 64, 64
B_H, B_DK, B_DV = 6, 32, 64
B_RANK = 16
B_TAU = 16.0
C_G, C_CH, C_P = 16, 16, 64
C_W = C_G * C_CH
C_S = C_G * C_P
N_GROUPS, N_PER_GROUP, D_EXP = 4, 4, 256
N_EXP = N_GROUPS * N_PER_GROUP
EPS = 1e-6
CHUNK = 64
LANES = 128
EXP_RANGE = 80.0

W_AQ, W_AF, W_AI, W_AG = 0, 384, 768, 1152
W_BQ, W_BK, W_BV, W_LR, W_BR, W_CU = 1536, 1728, 1920, 2304, 2320, 2704
IN_COLS = 2960
ALIGNED_SEGMENTS = {"aq|af": (W_AQ, 768), "ai|ag": (W_AI, 768), "bq": (W_BQ, 256), "bv|lr": (W_BV, 512)}
QA, FA, VA, GA = 0, 384, 768, 1152
QB, KB, VB, RB = 1536, 1792, 2048, 2432
CU, LB, KA = 2816, 3072, 3328
PW = 3712
N_LG = 5
LG_W = N_LG * LANES
BLOCKS = tuple((QA + LANES * p, KA + LANES * p, FA + LANES * p, LANES * p, 2, A_DK) for p in range(3)) + (
    (QB, KB, LB, 384, 4, B_DK), (QB + LANES, KB + LANES, LB + LANES, 640, 2, B_DK))
VO_W = 768

VMEM_LIMIT = 56 * 1024 * 1024


def _sigmoid(x):
    return 1.0 / (1.0 + jnp.exp(-x))


def _silu(x):
    return x * _sigmoid(x)


def _log_sigmoid(x):
    return jnp.minimum(x, 0.0) - jnp.log1p(jnp.exp(-jnp.abs(x)))


def _gelu_tanh(x):
    return 0.5 * x * (1.0 + jnp.tanh(math.sqrt(2.0 / math.pi) * (x + 0.044715 * x * x * x)))


def _split3(x):
    hi = x.astype(BF16)
    r = x - hi.astype(F32)
    mid = r.astype(BF16)
    lo = (r - mid.astype(F32)).astype(BF16)
    return hi, mid, lo


def _dot(a, b):
    return jnp.dot(a, b, preferred_element_type=F32)


def _dot_nt(a, b):
    return lax.dot_general(a, b, (((1,), (1,)), ((), ())), preferred_element_type=F32)


def _dot_tn(a, b):
    return lax.dot_general(a, b, (((0,), (0,)), ((), ())), preferred_element_type=F32)


def _dot_exact_01(x, m01):
    hi, mid, lo = _split3(x)
    return _dot(hi, m01) + _dot(mid, m01) + _dot(lo, m01)


def _rms_rows(x):
    return x * lax.rsqrt(jnp.mean(x * x, axis=-1, keepdims=True) + EPS)


def _forget_gate(af, lbv):
    e = jnp.exp(-jnp.abs(af))
    log_sig = jnp.minimum(af, 0.0) - jnp.log1p(e)
    a = lbv[0:1]
    b = lbv[1:2] + log_sig
    log_f = jnp.maximum(a, b) + jnp.log1p(jnp.exp(-jnp.abs(a - b)))
    k = lbv[2:3] * (jnp.where(af >= 0.0, e, 1.0) / (1.0 + e))
    return log_f, k


def _chunk_cumsum(x, c):
    pos = lax.broadcasted_iota(jnp.int32, x.shape, 0) % c
    s = 1
    while s < c:
        x = x + jnp.where(pos >= s, pltpu.roll(x, s, 0), 0.0)
        s *= 2
    return x


def _head_rms(o, dv):
    w = o.shape[1]
    r = lax.broadcasted_iota(jnp.int32, (w, w), 0)
    cidx = lax.broadcasted_iota(jnp.int32, (w, w), 1)
    ones_blk = jnp.where(r // dv == cidx // dv, 1.0, 0.0).astype(BF16)
    return o * lax.rsqrt(_dot((o * o).astype(BF16), ones_blk) * (1.0 / dv) + EPS)


def _project_all(h, wcols, lbv_ref, walpha_ref, balpha_ref, chunk, p_scr):
    def proj(name):
        return _dot(h, wcols(name))

    def cum(x):
        return x if chunk == 1 else _chunk_cumsum(x, chunk)

    aqf = proj("aq|af")
    p_scr[:, QA:QA + 384] = _silu(aqf[:, 0:384])
    log_f, ka = _forget_gate(aqf[:, 384:768], lbv_ref[...])
    p_scr[:, FA:FA + 384] = cum(log_f)
    p_scr[:, KA:KA + 384] = ka
    aig = proj("ai|ag")
    p_scr[:, VA:VA + 384] = aig[:, 0:384]
    p_scr[:, GA:GA + 384] = _silu(aig[:, 384:768])
    p_scr[:, QB:QB + 256] = proj("bq") * (B_DK ** -0.5)
    p_scr[:, KB:KB + 256] = proj("bk")
    bvlr = proj("bv|lr")
    p_scr[:, VB:VB + 384] = bvlr[:, 0:384]
    z = _dot(bvlr[:, 384:512].astype(BF16), walpha_ref[...]) + balpha_ref[...]
    p_scr[:, LB:LB + 256] = cum(_log_sigmoid(z) * (1.0 / B_TAU))
    brcu = proj("br|cu")
    p_scr[:, RB:RB + 384] = _silu(brcu[:, 0:384])
    p_scr[:, CU:CU + 256] = brcu[:, 384:640]


def _ada_kernel(c_ref, w_ref, b_ref, o_ref):
    sc = _silu(c_ref[...]).astype(BF16)
    o_ref[0] = _dot(sc, w_ref[0].astype(BF16)) + b_ref[0]


def _ada_call(c_all, ada_w, ada_b):
    n = c_all.shape[0]
    tn = 1536
    return pl.pallas_call(
        _ada_kernel,
        grid=(DEPTH, 6 * D // tn),
        in_specs=[
            pl.BlockSpec((n, D), lambda l, j: (0, 0)),
            pl.BlockSpec((1, D, tn), lambda l, j: (l, 0, j)),
            pl.BlockSpec((1, 1, tn), lambda l, j: (l, 0, j)),
        ],
        out_specs=pl.BlockSpec((1, n, tn), lambda l, j: (l, 0, j)),
        out_shape=jax.ShapeDtypeStruct((DEPTH, n, 6 * D), F32),
        compiler_params=pltpu.CompilerParams(
            dimension_semantics=("arbitrary", "arbitrary"), vmem_limit_bytes=VMEM_LIMIT),
        name="ada_mod",
    )(c_all, ada_w, ada_b.reshape(DEPTH, 1, 6 * D))


def _mixer_prompt_kernel(x_ref, mod_ref, g1_ref, win_ref, lbv_ref, walpha_ref, balpha_ref,
                         hg_ref, gg_ref, abar_ref, bb_ref, cc_ref, dsk_ref, wglu_ref, bglu_ref,
                         wout_ref,
                         xo_ref, sta_ref, stb_ref, s5r_ref, s5i_ref,
                         h_scr, p_scr, o_scr, xr_scr, xi_scr, t_scr,
                         wk_scr, wbrcu_scr, qt_scr, kt_scr, qd_scr, kd_scr, v_scr, of_scr, dec_scr):
    nb = x_ref.shape[0]
    c = CHUNK
    i = pl.program_id(0)

    @pl.when(i == 0)
    def _():
        sta_ref[...] = jnp.zeros_like(sta_ref)
        stb_ref[...] = jnp.zeros_like(stb_ref)
        s5r_ref[...] = jnp.zeros_like(s5r_ref)
        s5i_ref[...] = jnp.zeros_like(s5i_ref)
        wk_scr[:, 0:192] = win_ref[:, W_BK:W_BK + 192]
        wk_scr[:, 192:256] = jnp.zeros((D, 64), BF16)
        wbrcu_scr[...] = win_ref[:, W_BR:W_BR + 640]

    def wcols(name):
        if name == "bk":
            return wk_scr[...]
        if name == "br|cu":
            return wbrcu_scr[...]
        c0, w = ALIGNED_SEGMENTS[name]
        return win_ref[:, c0:c0 + w]

    for b in range(nb):
        hb = _rms_rows(x_ref[b]) * g1_ref[...]
        hb = hb * (1.0 + mod_ref[b:b + 1, D:2 * D]) + mod_ref[b:b + 1, 0:D]
        h_scr[b * c:(b + 1) * c, :] = hb.astype(BF16)

    _project_all(h_scr[...], wcols, lbv_ref, walpha_ref, balpha_ref, c, p_scr)

    v_scr[:, 0:384] = p_scr[:, VA:VA + 384].astype(BF16)
    v_scr[:, 384:768] = p_scr[:, VB:VB + 384].astype(BF16)
    rng = jnp.zeros((c, LANES), F32)
    for b in range(nb):
        rs = slice(b * c, (b + 1) * c)
        for g, (qc, kc, bc, _, _, _) in enumerate(BLOCKS):
            ls = slice(g * LANES, (g + 1) * LANES)
            q = p_scr[rs, qc:qc + LANES]
            k = p_scr[rs, kc:kc + LANES]
            bcum = p_scr[rs, bc:bc + LANES]
            b_mid = bcum[c // 2 - 1:c // 2, :]
            b_end = bcum[c - 1:c, :]
            d = bcum - b_mid
            rng = jnp.maximum(rng, jnp.abs(d))
            qt_scr[rs, ls] = (q * jnp.exp(d)).astype(BF16)
            kt_scr[rs, ls] = (k * jnp.exp(-d)).astype(BF16)
            qd_scr[rs, ls] = (q * jnp.exp(bcum)).astype(BF16)
            kd_scr[rs, ls] = (k * jnp.exp(b_end - bcum)).astype(BF16)
            dec_scr[b, :, ls] = jnp.broadcast_to(jnp.exp(b_end), (8, LANES))
    in_range = jnp.max(rng) <= EXP_RANGE

    lane = lax.broadcasted_iota(jnp.int32, (c, LANES), 1)

    def scores_direct(r0, qc, kc, bc, nh, dk):
        q = p_scr[pl.ds(r0, c), qc:qc + LANES]
        bcum = p_scr[pl.ds(r0, c), bc:bc + LANES]
        krow = lax.broadcasted_iota(jnp.int32, (LANES, nh * c), 0)
        kcol = lax.broadcasted_iota(jnp.int32, (LANES, nh * c), 1)

        sub = lax.broadcasted_iota(jnp.int32, (8, LANES), 0)

        def row_of(col0, s):
            r8 = pl.ds(pl.multiple_of(r0 + (s // 8) * 8, 8), 8)
            return jnp.sum(jnp.where(sub == s % 8, p_scr[r8, col0:col0 + LANES], 0.0), axis=0, keepdims=True)

        def body(s, sc):
            ks = row_of(kc, s)
            bs = row_of(bc, s)
            w = q * ks * jnp.exp(jnp.minimum(bcum - bs, 0.0))
            place = jnp.where(kcol == (krow // dk) * c + s, 1.0, 0.0).astype(BF16)
            return sc + _dot_exact_01(w, place)

        return lax.fori_loop(0, c, body, jnp.zeros((c, nh * c), F32))

    def attend(b, r0, direct):
        rs = pl.ds(r0, c)
        for g, (qc, kc, bc, vc, nh, dk) in enumerate(BLOCKS):
            ls = slice(g * LANES, (g + 1) * LANES)
            vw = nh * 64
            st_ref = sta_ref if g < 3 else stb_ref
            v0 = vc if g < 3 else vc - 384
            if direct:
                sc = scores_direct(r0, qc, kc, bc, nh, dk)
            else:
                kt = kt_scr[rs, ls]
                kcat = jnp.concatenate(
                    [jnp.where(lane // dk == h, kt, jnp.zeros_like(kt)) for h in range(nh)], axis=0)
                sc = _dot_nt(qt_scr[rs, ls], kcat)
            row = lax.broadcasted_iota(jnp.int32, (c, nh * c), 0)
            col = lax.broadcasted_iota(jnp.int32, (c, nh * c), 1)
            sc = jnp.where(col % c <= row, sc, 0.0).astype(BF16)
            v = v_scr[rs, vc:vc + vw]
            vlane = lax.broadcasted_iota(jnp.int32, (c, vw), 1)
            vcat = jnp.concatenate(
                [jnp.where(vlane // 64 == h, v, jnp.zeros_like(v)) for h in range(nh)], axis=0)
            st = st_ref[b, v0:v0 + vw, :]
            o = _dot(sc, vcat) + _dot_nt(qd_scr[rs, ls], st.astype(BF16))
            of_scr[rs, vc:vc + vw] = o
            upd = _dot_tn(v, kd_scr[rs, ls])
            srow = lax.broadcasted_iota(jnp.int32, (vw, LANES), 0)
            scol = lax.broadcasted_iota(jnp.int32, (vw, LANES), 1)
            st_ref[b, v0:v0 + vw, :] = (st * dec_scr[b, 0:1, ls]
                                        + jnp.where(srow // 64 == scol // dk, upd, 0.0))

    @pl.when(in_range)
    def _():
        for b in range(nb):
            attend(b, b * c, False)

    @pl.when(jnp.logical_not(in_range))
    def _():
        def body(b, carry):
            attend(b, pl.multiple_of(b * c, c), True)
            return carry
        lax.fori_loop(0, nb, body, 0)

    oa = _head_rms(of_scr[:, 0:384], A_DV) * hg_ref[...] * p_scr[:, GA:GA + 384]
    o_scr[:, 0:384] = oa.astype(BF16)
    ob = _head_rms(of_scr[:, 384:768], B_DV) * gg_ref[...] * p_scr[:, RB:RB + 384]
    o_scr[:, 384:768] = ob.astype(BF16)

    halves = C_W // LANES
    for b in range(nb):
        for j in range(halves):
            t_scr[j, b:b + nb * c:nb, :] = p_scr[b * c:(b + 1) * c, CU + LANES * j:CU + LANES * (j + 1)]
    u = jnp.concatenate([t_scr[j] for j in range(halves)], axis=1)
    xri = _dot(u.astype(BF16), bb_ref[...])
    xr_scr[...] = xri[:, 0:C_S]
    xi_scr[...] = xri[:, C_S:2 * C_S]
    ar = abar_ref[0:1, :]
    ai = abar_ref[1:2, :]

    def scan_t(t, carry):
        sr, si = carry
        ts = pl.ds(pl.multiple_of(t * nb, nb), nb)
        nr = ar * sr - ai * si + xr_scr[ts, :]
        ni = ar * si + ai * sr + xi_scr[ts, :]
        xr_scr[ts, :] = nr
        xi_scr[ts, :] = ni
        return nr, ni

    sr, si = lax.fori_loop(0, c, scan_t, (s5r_ref[...], s5i_ref[...]), unroll=2)
    s5r_ref[...] = sr
    s5i_ref[...] = si
    y = (_dot(xr_scr[...].astype(BF16), cc_ref[0:C_S, :])
         + _dot(xi_scr[...].astype(BF16), cc_ref[C_S:2 * C_S, :]) + dsk_ref[...] * u)
    z = _gelu_tanh(y)
    oc = z * _sigmoid(_dot(z.astype(BF16), wglu_ref[...]) + bglu_ref[...])
    for j in range(halves):
        t_scr[j] = oc[:, LANES * j:LANES * (j + 1)]
    for b in range(nb):
        for j in range(halves):
            o_scr[b * c:(b + 1) * c, 768 + LANES * j:768 + LANES * (j + 1)] = (
                t_scr[j, b:b + nb * c:nb, :].astype(BF16))

    out = _dot(o_scr[...], wout_ref[...])
    for b in range(nb):
        xo_ref[b] = x_ref[b] + mod_ref[b:b + 1, 2 * D:3 * D] * out[b * c:(b + 1) * c, :]


def _layer_spec(arr, l):
    nd = arr.ndim - 1
    return pl.BlockSpec((None,) + arr.shape[1:], lambda i, _n=nd: (l,) + (0,) * _n,
                        pipeline_mode=pl.Buffered(1))


def _mixer_params(lp):
    return [lp[k] for k in ("g1", "w_in", "lbv", "w_alpha", "b_alpha", "hgrn_g", "gla_g", "abar", "bb", "cc",
                            "d_skip", "w_glu", "b_glu", "w_out")]


def _mixer_prompt_call(l, x, mod, ns, lp):
    nb, seq, _ = x.shape
    c = CHUNK
    rows = nb * c
    consts = _mixer_params(lp)
    in_specs = ([pl.BlockSpec((nb, c, D), lambda i: (0, i, 0)),
                 pl.BlockSpec((None, nb, 6 * D), lambda i: (l, ns // nb, 0), pipeline_mode=pl.Buffered(1))]
                + [_layer_spec(a, l) for a in consts])
    out_shape = (jax.ShapeDtypeStruct((nb, seq, D), F32),
                 jax.ShapeDtypeStruct((nb, 384, LANES), F32),
                 jax.ShapeDtypeStruct((nb, 384, LANES), F32),
                 jax.ShapeDtypeStruct((nb, C_S), F32),
                 jax.ShapeDtypeStruct((nb, C_S), F32))
    out_specs = (pl.BlockSpec((nb, c, D), lambda i: (0, i, 0)),
                 pl.BlockSpec((nb, 384, LANES), lambda i: (0, 0, 0)),
                 pl.BlockSpec((nb, 384, LANES), lambda i: (0, 0, 0)),
                 pl.BlockSpec((nb, C_S), lambda i: (0, 0)),
                 pl.BlockSpec((nb, C_S), lambda i: (0, 0)))
    scratch = [pltpu.VMEM((rows, D), BF16), pltpu.VMEM((rows, PW), F32), pltpu.VMEM((rows, D), BF16),
               pltpu.VMEM((rows, C_S), F32), pltpu.VMEM((rows, C_S), F32),
               pltpu.VMEM((C_W // LANES, rows, LANES), F32),
               pltpu.VMEM((D, 256), BF16), pltpu.VMEM((D, 640), BF16),
               pltpu.VMEM((rows, LG_W), BF16), pltpu.VMEM((rows, LG_W), BF16),
               pltpu.VMEM((rows, LG_W), BF16), pltpu.VMEM((rows, LG_W), BF16),
               pltpu.VMEM((rows, VO_W), BF16), pltpu.VMEM((rows, VO_W), F32), pltpu.VMEM((nb, 8, LG_W), F32)]
    return pl.pallas_call(
        _mixer_prompt_kernel,
        grid=(seq // c,),
        in_specs=in_specs,
        out_specs=out_specs,
        out_shape=out_shape,
        scratch_shapes=scratch,
        compiler_params=pltpu.CompilerParams(
            dimension_semantics=("arbitrary",), vmem_limit_bytes=VMEM_LIMIT),
        name="mixer_prompt",
    )(x, mod, *consts)


def _step_head(s_ref, so_ref, o_ref, dec_t, k_t, q_t, v):
    for n in range(v.shape[0]):
        new = s_ref[n] * dec_t[:, n:n + 1] + k_t[:, n:n + 1] * v[n:n + 1, :]
        so_ref[n] = new
        o_ref[n:n + 1, :] = jnp.sum(new * q_t[:, n:n + 1], axis=0, keepdims=True)


def _mixer_sample_kernel(x_ref, mod_ref, g1_ref, win_ref, lbv_ref, walpha_ref, balpha_ref,
                         hg_ref, gg_ref, abar_ref, bb_ref, cc_ref, dsk_ref, wglu_ref, bglu_ref,
                         wout_ref, sa_ref, sb_ref, s5r_ref, s5i_ref,
                         xo_ref, sao_ref, sbo_ref, s5ro_ref, s5io_ref,
                         p_scr, at_scr, bt_scr, v_scr, gate_scr, oh_scr, oc_scr, orow_scr):
    i = pl.program_id(0)
    n = x_ref.shape[0]

    @pl.when(i == 0)
    def _():
        def wcols(name):
            if name == "bk":
                return jnp.concatenate([win_ref[:, W_BK:W_BK + 192], jnp.zeros((D, 64), BF16)], axis=1)
            if name == "br|cu":
                return win_ref[:, W_BR:W_BR + 640]
            c0, w = ALIGNED_SEGMENTS[name]
            return win_ref[:, c0:c0 + w]

        h = _rms_rows(x_ref[...]) * g1_ref[...]
        h = (h * (1.0 + mod_ref[:, D:2 * D]) + mod_ref[:, 0:D]).astype(BF16)
        _project_all(h, wcols, lbv_ref, walpha_ref, balpha_ref, 1, p_scr)
        for g, (qc, kc, bc, _, nh, dk) in enumerate(BLOCKS):
            t_scr, h0 = (at_scr, 2 * g) if g < 3 else (bt_scr, 4 * (g - 3))
            qt = p_scr[:, qc:qc + LANES].T
            dt = jnp.exp(p_scr[:, bc:bc + LANES]).T
            kt = p_scr[:, kc:kc + LANES].T
            for hh in range(nh):
                t_scr[0, h0 + hh] = qt[hh * dk:(hh + 1) * dk, :]
                t_scr[1, h0 + hh] = dt[hh * dk:(hh + 1) * dk, :]
                t_scr[2, h0 + hh] = kt[hh * dk:(hh + 1) * dk, :]
        for hh in range(A_H):
            a0 = 64 * hh
            v_scr[0, hh] = p_scr[:, VA + a0:VA + a0 + 64]
            v_scr[1, hh] = p_scr[:, VB + a0:VB + a0 + 64]
            gate_scr[0, hh] = p_scr[:, GA + a0:GA + a0 + 64]
            gate_scr[1, hh] = p_scr[:, RB + a0:RB + a0 + 64]
        u = p_scr[:, CU:CU + C_W]
        xri = _dot(u.astype(BF16), bb_ref[...])
        ar = abar_ref[0:1, :]
        ai = abar_ref[1:2, :]
        s0r = s5r_ref[...]
        s0i = s5i_ref[...]
        sr = ar * s0r - ai * s0i + xri[:, 0:C_S]
        si = ar * s0i + ai * s0r + xri[:, C_S:2 * C_S]
        s5ro_ref[...] = sr
        s5io_ref[...] = si
        y = (_dot(sr.astype(BF16), cc_ref[0:C_S, :]) + _dot(si.astype(BF16), cc_ref[C_S:2 * C_S, :])
             + dsk_ref[...] * u)
        z = _gelu_tanh(y)
        oc_scr[...] = z * _sigmoid(_dot(z.astype(BF16), wglu_ref[...]) + bglu_ref[...])

    _step_head(sa_ref, sao_ref, orow_scr, at_scr[1, i], at_scr[2, i], at_scr[0, i], v_scr[0, i])
    oh_scr[0, i] = _rms_rows(orow_scr[...]) * hg_ref[:, 0:64] * gate_scr[0, i]
    _step_head(sb_ref, sbo_ref, orow_scr, bt_scr[1, i], bt_scr[2, i], bt_scr[0, i], v_scr[1, i])
    oh_scr[1, i] = _rms_rows(orow_scr[...]) * gg_ref[:, 0:64] * gate_scr[1, i]

    @pl.when(i == A_H - 1)
    def _():
        o = jnp.concatenate([oh_scr[0, hh] for hh in range(A_H)] + [oh_scr[1, hh] for hh in range(B_H)]
                            + [oc_scr[...]], axis=1).astype(BF16)
        xo_ref[...] = x_ref[...] + mod_ref[:, 2 * D:3 * D] * _dot(o, wout_ref[...])


def _mixer_sample_call(l, x, mod, lp, st_a, st_b, st_r, st_i):
    n = x.shape[0]
    consts = _mixer_params(lp)
    in_specs = ([pl.BlockSpec((n, D), lambda i: (0, 0), pipeline_mode=pl.Buffered(1)),
                 pl.BlockSpec((None, n, 6 * D), lambda i: (l, 0, 0), pipeline_mode=pl.Buffered(1))]
                + [_layer_spec(a, l) for a in consts]
                + [pl.BlockSpec((None, n, None, A_DK, A_DV), lambda i: (l, 0, i, 0, 0)),
                   pl.BlockSpec((None, n, None, B_DK, B_DV), lambda i: (l, 0, i, 0, 0)),
                   pl.BlockSpec((None, n, C_S), lambda i: (l, 0, 0)),
                   pl.BlockSpec((None, n, C_S), lambda i: (l, 0, 0))])
    out_shape = (jax.ShapeDtypeStruct((n, D), F32),
                 jax.ShapeDtypeStruct((n, A_H, A_DK, A_DV), F32),
                 jax.ShapeDtypeStruct((n, B_H, B_DK, B_DV), F32),
                 jax.ShapeDtypeStruct((n, C_S), F32),
                 jax.ShapeDtypeStruct((n, C_S), F32))
    out_specs = (pl.BlockSpec((n, D), lambda i: (0, 0)),
                 pl.BlockSpec((n, None, A_DK, A_DV), lambda i: (0, i, 0, 0)),
                 pl.BlockSpec((n, None, B_DK, B_DV), lambda i: (0, i, 0, 0)),
                 pl.BlockSpec((n, C_S), lambda i: (0, 0)),
                 pl.BlockSpec((n, C_S), lambda i: (0, 0)))
    scratch = [pltpu.VMEM((n, PW), F32), pltpu.VMEM((3, A_H, A_DK, n), F32), pltpu.VMEM((3, 8, B_DK, n), F32),
               pltpu.VMEM((2, A_H, n, 64), F32), pltpu.VMEM((2, A_H, n, 64), F32),
               pltpu.VMEM((2, A_H, n, 64), F32), pltpu.VMEM((n, C_W), F32), pltpu.VMEM((n, 64), F32)]
    return pl.pallas_call(
        _mixer_sample_kernel,
        grid=(A_H,),
        in_specs=in_specs,
        out_specs=out_specs,
        out_shape=out_shape,
        scratch_shapes=scratch,
        compiler_params=pltpu.CompilerParams(
            dimension_semantics=("arbitrary",), vmem_limit_bytes=VMEM_LIMIT),
        name="mixer_sample",
    )(x, mod, *consts, st_a, st_b, st_r, st_i)


def _moe_kernel(x_ref, mod_ref, g2_ref, wr_ref, br_ref, wg_ref, wu_ref, wd_ref, fg_ref,
                o_ref, h_scr, gate_scr, acc_scr, *, tiles_per_seq, final_norm):
    g = pl.program_id(1)
    tm = x_ref.shape[0]

    def mod(k):
        if tiles_per_seq is None:
            return mod_ref[:, k * D:(k + 1) * D]
        return mod_ref[pl.ds(pl.program_id(0) // tiles_per_seq, 1), k * D:(k + 1) * D]

    lane = lax.broadcasted_iota(jnp.int32, (tm, LANES), 1).astype(F32)
    big = jnp.float32(1 << 20)
    neg = jnp.float32(-jnp.inf)

    @pl.when(g == 0)
    def _():
        h = _rms_rows(x_ref[...]) * g2_ref[...]
        h = (h * (1.0 + mod(4)) + mod(3)).astype(BF16)
        h_scr[...] = h
        logits = _dot(h, wr_ref[...]) + br_ref[...]
        gl = jnp.where(lane < N_GROUPS, logits, neg)
        gmax = jnp.max(gl, axis=1, keepdims=True)
        gidx = jnp.min(jnp.where(gl == gmax, lane, big), axis=1, keepdims=True)
        p_group = 1.0 / jnp.sum(jnp.where(lane < N_GROUPS, jnp.exp(logits - gmax), 0.0),
                                axis=1, keepdims=True)
        e0 = N_GROUPS + N_PER_GROUP * gidx
        el = jnp.where((lane >= e0) & (lane < e0 + N_PER_GROUP), logits, neg)
        v1 = jnp.max(el, axis=1, keepdims=True)
        i1 = jnp.min(jnp.where(el == v1, lane, big), axis=1, keepdims=True)
        el2 = jnp.where(lane == i1, neg, el)
        v2 = jnp.max(el2, axis=1, keepdims=True)
        i2 = jnp.min(jnp.where(el2 == v2, lane, big), axis=1, keepdims=True)
        ex = jnp.exp(v2 - v1)
        w1 = 1.0 / (1.0 + ex)
        gate_scr[...] = (jnp.where(lane == i1, p_group * w1, 0.0)
                         + jnp.where(lane == i2, p_group * (ex * w1), 0.0))
        acc_scr[...] = jnp.zeros_like(acc_scr)

    h = h_scr[...]
    gates = gate_scr[...]
    lane0 = (N_GROUPS + N_PER_GROUP * g).astype(F32)
    y = None
    for e in range(N_PER_GROUP):
        hid = _silu(_dot(h, wg_ref[e])) * _dot(h, wu_ref[e])
        gcol = jnp.sum(jnp.where(lane == lane0 + e, gates, 0.0), axis=1, keepdims=True)
        ye = _dot((hid * gcol).astype(BF16), wd_ref[e])
        y = ye if y is None else y + ye
    acc_scr[...] += y

    @pl.when(g == N_GROUPS - 1)
    def _():
        xn = x_ref[...] + mod(5) * acc_scr[...]
        if final_norm:
            xn = _rms_rows(xn) * fg_ref[...]
        o_ref[...] = xn


def _moe_call(l, x2, mod, lp, final_g, *, tm, mod_row0, tiles_per_seq, final_norm):
    t = x2.shape[0]
    nt = t // tm
    if tiles_per_seq is None:
        mod_spec = pl.BlockSpec((None, tm, 6 * D), lambda i, g: (l, i, 0))
    else:
        nb = t // (tm * tiles_per_seq)
        mod_spec = pl.BlockSpec((None, nb, 6 * D), lambda i, g: (l, mod_row0 // nb, 0))

    def vec_spec(arr):
        return pl.BlockSpec((None,) + arr.shape[1:], lambda i, g: (l, 0, 0))

    def exp_spec(arr):
        return pl.BlockSpec((None, N_PER_GROUP) + arr.shape[2:], lambda i, g: (l, g, 0, 0))

    kern = functools.partial(_moe_kernel, tiles_per_seq=tiles_per_seq, final_norm=final_norm)
    return pl.pallas_call(
        kern,
        grid=(nt, N_GROUPS),
        in_specs=[
            pl.BlockSpec((tm, D), lambda i, g: (i, 0)),
            mod_spec,
            vec_spec(lp["g2"]), vec_spec(lp["w_router"]), vec_spec(lp["b_router"]),
            exp_spec(lp["w_gate"]), exp_spec(lp["w_up"]), exp_spec(lp["w_down"]),
            pl.BlockSpec((1, D), lambda i, g: (0, 0)),
        ],
        out_specs=pl.BlockSpec((tm, D), lambda i, g: (i, 0)),
        out_shape=jax.ShapeDtypeStruct((t, D), F32),
        scratch_shapes=[pltpu.VMEM((tm, D), BF16), pltpu.VMEM((tm, LANES), F32), pltpu.VMEM((tm, D), F32)],
        compiler_params=pltpu.CompilerParams(
            dimension_semantics=("arbitrary", "arbitrary"), vmem_limit_bytes=VMEM_LIMIT),
        name="moe_sample" if tiles_per_seq is None else "moe_prompt",
    )(x2, mod, lp["g2"], lp["w_router"], lp["b_router"], lp["w_gate"], lp["w_up"], lp["w_down"], final_g)


def _pad_to(a, axis, size):
    pad = [(0, 0)] * a.ndim
    pad[axis] = (0, size - a.shape[axis])
    return jnp.pad(a, pad)


def _prepare_params(p):
    lb_cum = jnp.cumsum(jax.nn.softmax(p["hgrn_lb_logits"].astype(F32), axis=0), axis=0)
    lb = lb_cum - lb_cum[:1]
    lbv = _pad_to(jnp.stack([jnp.log(lb), jnp.log1p(-lb), 1.0 - lb], axis=1), 1, 8)
    a_re, a_im = p["s5_a_re"].astype(F32), p["s5_a_im"].astype(F32)
    dt = jnp.exp(p["s5_log_dt"].astype(F32))[..., None]
    mag = jnp.exp(a_re * dt)
    abar_re = mag * jnp.cos(a_im * dt)
    abar_im = mag * jnp.sin(a_im * dt)
    den = a_re * a_re + a_im * a_im
    nr, ni = abar_re - 1.0, abar_im
    coef_re = ((nr * a_re + ni * a_im) / den)[..., None]
    coef_im = ((ni * a_re - nr * a_im) / den)[..., None]
    b_re, b_im = p["s5_b_re"].astype(F32), p["s5_b_im"].astype(F32)
    bb_re = coef_re * b_re - coef_im * b_im
    bb_im = coef_re * b_im + coef_im * b_re
    eye = jnp.eye(C_G, dtype=F32)
    blk_b = lambda m: jnp.einsum("lgph,gk->lghkp", m, eye).reshape(DEPTH, C_W, C_S)
    blk_c = lambda m: jnp.einsum("lghp,gk->lgpkh", m, eye).reshape(DEPTH, C_S, C_W)
    bb = jnp.concatenate([blk_b(bb_re), blk_b(bb_im)], axis=2).astype(BF16)
    cc = jnp.concatenate([blk_c(p["s5_c_re"].astype(F32)), -blk_c(p["s5_c_im"].astype(F32))],
                         axis=1).astype(BF16)
    abar = _pad_to(jnp.stack([abar_re.reshape(DEPTH, C_S), abar_im.reshape(DEPTH, C_S)], axis=1), 1, 8)
    w_router = _pad_to(jnp.concatenate([p["moe_w_group"], p["moe_w_expert"]], axis=2), 2, LANES).astype(BF16)
    b_router = _pad_to(jnp.concatenate([p["moe_b_group"], p["moe_b_expert"]], axis=1), 1, LANES)
    row = lambda a: a.astype(F32)[:, None, :]
    return dict(
        g1=row(p["norm1_g"]), g2=row(p["norm2_g"]), w_in=p["w_in"].astype(BF16), lbv=lbv,
        w_alpha=_pad_to(_pad_to(p["gla_w_alpha"], 1, LANES), 2, 256).astype(BF16),
        b_alpha=_pad_to(row(p["gla_b_alpha"]), 2, 256),
        hgrn_g=row(jnp.tile(p["hgrn_norm_g"], (1, A_H))), gla_g=row(jnp.tile(p["gla_norm_g"], (1, B_H))),
        abar=abar, bb=bb, cc=cc, d_skip=row(p["s5_d"]),
        w_glu=p["s5_w_glu"].astype(BF16), b_glu=row(p["s5_b_glu"]), w_out=p["w_out"].astype(BF16),
        w_gate=p["moe_w_gate"].astype(BF16), w_up=p["moe_w_up"].astype(BF16),
        w_down=p["moe_w_down"].astype(BF16), w_router=w_router, b_router=b_router[:, None, :])


def kernel(x_prompt, x_sample, state_hgrn, state_gla, state_s5_re, state_s5_im, c_prompt, c_sample, ada_w, ada_b, norm1_g, norm2_g, w_in, hgrn_lb_logits, hgrn_norm_g, gla_w_alpha, gla_b_alpha, gla_norm_g, s5_a_re, s5_a_im, s5_b_re, s5_b_im, s5_c_re, s5_c_im, s5_d, s5_log_dt, s5_w_glu, s5_b_glu, w_out, moe_w_group, moe_b_group, moe_w_expert, moe_b_expert, moe_w_gate, moe_w_up, moe_w_down, final_norm_g):
    lp = _prepare_params(dict(
        norm1_g=norm1_g, norm2_g=norm2_g, w_in=w_in, hgrn_lb_logits=hgrn_lb_logits, hgrn_norm_g=hgrn_norm_g,
        gla_w_alpha=gla_w_alpha, gla_b_alpha=gla_b_alpha, gla_norm_g=gla_norm_g, s5_a_re=s5_a_re,
        s5_a_im=s5_a_im, s5_b_re=s5_b_re, s5_b_im=s5_b_im, s5_c_re=s5_c_re, s5_c_im=s5_c_im, s5_d=s5_d,
        s5_log_dt=s5_log_dt, s5_w_glu=s5_w_glu, s5_b_glu=s5_b_glu, w_out=w_out, moe_w_group=moe_w_group,
        moe_b_group=moe_b_group, moe_w_expert=moe_w_expert, moe_b_expert=moe_b_expert,
        moe_w_gate=moe_w_gate, moe_w_up=moe_w_up, moe_w_down=moe_w_down))
    nb, seq, _ = x_prompt.shape
    ns = x_sample.shape[0]
    assert ns % nb == 0 and seq % CHUNK == 0
    final_g = final_norm_g.reshape(1, D)

    mod = _ada_call(jnp.concatenate([c_sample, c_prompt], axis=0), ada_w, ada_b)
    s5r_in = state_s5_re.reshape(DEPTH, ns, C_S)
    s5i_in = state_s5_im.reshape(DEPTH, ns, C_S)
    tm = min(512, seq)

    xp = x_prompt
    xs = x_sample.reshape(ns, D)
    outs = {k: [] for k in ("hg_p", "gl_p", "sr_p", "si_p", "hg_s", "gl_s", "sr_s", "si_s")}
    for l in range(DEPTH):
        last = l == DEPTH - 1
        xp, sta, stb, s5r, s5i = _mixer_prompt_call(l, xp, mod, ns, lp)
        xp = _moe_call(l, xp.reshape(nb * seq, D), mod, lp, final_g, tm=tm, mod_row0=ns,
                       tiles_per_seq=seq // tm, final_norm=last).reshape(nb, seq, D)
        sta = sta.reshape(nb, A_H // 2, 2, A_DV, 2, A_DK)
        outs["hg_p"].append(jnp.stack([sta[:, h // 2, h % 2, :, h % 2, :] for h in range(A_H)], axis=1)
                            .transpose(0, 1, 3, 2))
        stb = stb.reshape(nb, B_H, B_DV, 4, B_DK)
        outs["gl_p"].append(jnp.stack([stb[:, h, :, h % 4, :] for h in range(B_H)], axis=1)
                            .transpose(0, 1, 3, 2))
        outs["sr_p"].append(s5r.reshape(nb, C_G, C_P))
        outs["si_p"].append(s5i.reshape(nb, C_G, C_P))
        xs, sao, sbo, s5ro, s5io = _mixer_sample_call(l, xs, mod, lp, state_hgrn, state_gla, s5r_in, s5i_in)
        xs = _moe_call(l, xs, mod, lp, final_g, tm=ns, mod_row0=0, tiles_per_seq=None, final_norm=last)
        outs["hg_s"].append(sao)
        outs["gl_s"].append(sbo)
        outs["sr_s"].append(s5ro.reshape(ns, C_G, C_P))
        outs["si_s"].append(s5io.reshape(ns, C_G, C_P))

    return (xp, xs.reshape(ns, 1, D),
            jnp.stack(outs["hg_p"]), jnp.stack(outs["gl_p"]), jnp.stack(outs["sr_p"]), jnp.stack(outs["si_p"]),
            jnp.stack(outs["hg_s"]), jnp.stack(outs["gl_s"]), jnp.stack(outs["sr_s"]), jnp.stack(outs["si_s"]))
```

```python
import functools
import math

import numpy as np
import jax
import jax.numpy as jnp
from jax import lax
from jax.experimental import pallas as pl
from jax.experimental.pallas import tpu as pltpu

F32 = jnp.float32
BF16 = jnp.bfloat16

D = 1024
DEPTH = 4
A_H, A_DK, A_DV = 6, 64, 64
B_H, B_DK, B_DV = 6, 32, 64
B_RANK = 16
B_TAU = 16.0
C_G, C_CH, C_P = 16, 16, 64
C_W = C_G * C_CH
C_S = C_G * C_P
N_GROUPS, N_PER_GROUP, D_EXP = 4, 4, 256
N_EXP = N_GROUPS * N_PER_GROUP
EPS = 1e-6
CHUNK = 64
LANES = 128
SUBLANES = 8
EXP_RANGE = 80.0
MOE_TILE = 1024
MOE_BLOCK = 256
BF16_ROWS = 16

W_AQ, W_AF, W_AI, W_AG = 0, 384, 768, 1152
W_BQ, W_BK, W_BV, W_LR, W_BR, W_CU = 1536, 1728, 1920, 2304, 2320, 2704
IN_COLS = 2960
W_SEGMENTS = {"aq|af": (W_AQ, 768), "ai|ag": (W_AI, 768), "bq": (W_BQ, 256), "bk": (W_BK, 256),
              "bv|lr": (W_BV, 512), "br|cu": (W_BR, 640)}
QA, FA, VA, GA = 0, 384, 768, 1152
QB, KB, VB, RB = 1536, 1792, 2048, 2432
CU, LB, KA = 2816, 3072, 3328
PW = 3712
N_LG = 5
LG_W = N_LG * LANES
BLOCKS = tuple((QA + LANES * p, KA + LANES * p, FA + LANES * p, LANES * p, 2, A_DK) for p in range(3)) + (
    (QB, KB, LB, 384, 4, B_DK), (QB + LANES, KB + LANES, LB + LANES, 640, 2, B_DK))
VO_W = 768

VMEM_LIMIT = 56 * 1024 * 1024


def _sigmoid(x):
    return 1.0 / (1.0 + jnp.exp(-x))


def _silu(x):
    return x * _sigmoid(x)


def _log_sigmoid(x):
    return jnp.minimum(x, 0.0) - jnp.log1p(jnp.exp(-jnp.abs(x)))


def _gelu_tanh(x):
    return 0.5 * x * (1.0 + jnp.tanh(math.sqrt(2.0 / math.pi) * (x + 0.044715 * x * x * x)))


def _split3(x):
    hi = x.astype(BF16)
    r = x - hi.astype(F32)
    mid = r.astype(BF16)
    lo = (r - mid.astype(F32)).astype(BF16)
    return hi, mid, lo


def _dot(a, b):
    return jnp.dot(a, b, preferred_element_type=F32)


def _dot_nt(a, b):
    return lax.dot_general(a, b, (((1,), (1,)), ((), ())), preferred_element_type=F32)


def _dot_tn(a, b):
    return lax.dot_general(a, b, (((0,), (0,)), ((), ())), preferred_element_type=F32)


def _dot_exact_01(x, m01):
    hi, mid, lo = _split3(x)
    return _dot(hi, m01) + _dot(mid, m01) + _dot(lo, m01)


def _rms_rows(x):
    return x * lax.rsqrt(jnp.mean(x * x, axis=-1, keepdims=True) + EPS)


def _forget_gate(af, lbv):
    e = jnp.exp(-jnp.abs(af))
    log_sig = jnp.minimum(af, 0.0) - jnp.log1p(e)
    a = lbv[0:1]
    b = lbv[1:2] + log_sig
    log_f = jnp.maximum(a, b) + jnp.log1p(jnp.exp(-jnp.abs(a - b)))
    k = lbv[2:3] * (jnp.where(af >= 0.0, e, 1.0) / (1.0 + e))
    return log_f, k


def _chunk_cumsum(x, c):
    pos = lax.broadcasted_iota(jnp.int32, x.shape, 0) % c
    s = 1
    while s < c:
        x = x + jnp.where(pos >= s, pltpu.roll(x, s, 0), 0.0)
        s *= 2
    return x


def _head_rms(o, dv):
    w = o.shape[1]
    r = lax.broadcasted_iota(jnp.int32, (w, w), 0)
    cidx = lax.broadcasted_iota(jnp.int32, (w, w), 1)
    ones_blk = jnp.where(r // dv == cidx // dv, 1.0, 0.0).astype(BF16)
    return o * lax.rsqrt(_dot((o * o).astype(BF16), ones_blk) * (1.0 / dv) + EPS)


def _project_all(h, wint_ref, lbv_ref, walpha_ref, balpha_ref, chunk, p_scr):
    def proj(name):
        r0, n = W_SEGMENTS[name]
        return _dot_nt(h, wint_ref[r0:r0 + n, :])

    def cum(x):
        return x if chunk == 1 else _chunk_cumsum(x, chunk)

    aqf = proj("aq|af")
    p_scr[:, QA:QA + 384] = _silu(aqf[:, 0:384])
    log_f, ka = _forget_gate(aqf[:, 384:768], lbv_ref[...])
    p_scr[:, FA:FA + 384] = cum(log_f)
    p_scr[:, KA:KA + 384] = ka
    aig = proj("ai|ag")
    p_scr[:, VA:VA + 384] = aig[:, 0:384]
    p_scr[:, GA:GA + 384] = _silu(aig[:, 384:768])
    p_scr[:, QB:QB + 256] = proj("bq") * (B_DK ** -0.5)
    kb = proj("bk")
    p_scr[:, KB:KB + 256] = jnp.where(lax.broadcasted_iota(jnp.int32, kb.shape, 1) < B_H * B_DK, kb, 0.0)
    bvlr = proj("bv|lr")
    p_scr[:, VB:VB + 384] = bvlr[:, 0:384]
    z = _dot(bvlr[:, 384:512].astype(BF16), walpha_ref[...]) + balpha_ref[...]
    p_scr[:, LB:LB + 256] = cum(_log_sigmoid(z) * (1.0 / B_TAU))
    brcu = proj("br|cu")
    p_scr[:, RB:RB + 384] = _silu(brcu[:, 0:384])
    p_scr[:, CU:CU + 256] = brcu[:, 384:640]


def _layer_spec(arr, l):
    nd = arr.ndim - 1
    return pl.BlockSpec((None,) + arr.shape[1:], lambda *_, _n=nd: (l,) + (0,) * _n,
                        pipeline_mode=pl.Buffered(1))


def _ada_kernel(c_ref, w_ref, b_ref, o_ref):
    sc = _silu(c_ref[...]).astype(BF16)
    o_ref[0] = _dot(sc, w_ref[0].astype(BF16)) + b_ref[0]


def _ada_call(c_all, ada_w, ada_b):
    n = c_all.shape[0]
    tn = 1536
    return pl.pallas_call(
        _ada_kernel,
        grid=(DEPTH, 6 * D // tn),
        in_specs=[
            pl.BlockSpec((n, D), lambda l, j: (0, 0)),
            pl.BlockSpec((1, D, tn), lambda l, j: (l, 0, j)),
            pl.BlockSpec((1, 1, tn), lambda l, j: (l, 0, j)),
        ],
        out_specs=pl.BlockSpec((1, n, tn), lambda l, j: (l, 0, j)),
        out_shape=jax.ShapeDtypeStruct((DEPTH, n, 6 * D), F32),
        compiler_params=pltpu.CompilerParams(
            dimension_semantics=("arbitrary", "arbitrary"), vmem_limit_bytes=VMEM_LIMIT),
        name="ada_mod",
    )(c_all, ada_w, ada_b.reshape(DEPTH, 1, 6 * D))


def _mixer_prompt_kernel(x_ref, mod_ref, g1_ref, win_ref, lbv_ref, walpha_ref, balpha_ref,
                         hg_ref, gg_ref, abar_ref, bb_ref, cc_ref, dsk_ref, wglu_ref, bglu_ref,
                         wout_ref,
                         xo_ref, sta_ref, stb_ref, s5r_ref, s5i_ref,
                         h_scr, p_scr, o_scr, xr_scr, xi_scr, t_scr,
                         qt_scr, kt_scr, qd_scr, kd_scr, v_scr, of_scr, dec_scr):
    nb = x_ref.shape[0]
    c = CHUNK
    i = pl.program_id(0)

    @pl.when(i == 0)
    def _():
        sta_ref[...] = jnp.zeros_like(sta_ref)
        stb_ref[...] = jnp.zeros_like(stb_ref)
        s5r_ref[...] = jnp.zeros_like(s5r_ref)
        s5i_ref[...] = jnp.zeros_like(s5i_ref)

    for b in range(nb):
        hb = _rms_rows(x_ref[b]) * g1_ref[...]
        hb = hb * (1.0 + mod_ref[b:b + 1, D:2 * D]) + mod_ref[b:b + 1, 0:D]
        h_scr[b * c:(b + 1) * c, :] = hb.astype(BF16)

    _project_all(h_scr[...], win_ref, lbv_ref, walpha_ref, balpha_ref, c, p_scr)

    v_scr[:, 0:384] = p_scr[:, VA:VA + 384].astype(BF16)
    v_scr[:, 384:768] = p_scr[:, VB:VB + 384].astype(BF16)
    rng = jnp.zeros((c, LANES), F32)
    for b in range(nb):
        rs = slice(b * c, (b + 1) * c)
        for g, (qc, kc, bc, _, _, _) in enumerate(BLOCKS):
            ls = slice(g * LANES, (g + 1) * LANES)
            q = p_scr[rs, qc:qc + LANES]
            k = p_scr[rs, kc:kc + LANES]
            bcum = p_scr[rs, bc:bc + LANES]
            b_mid = bcum[c // 2 - 1:c // 2, :]
            b_end = bcum[c - 1:c, :]
            d = bcum - b_mid
            rng = jnp.maximum(rng, jnp.abs(d))
            qt_scr[rs, ls] = (q * jnp.exp(d)).astype(BF16)
            kt_scr[rs, ls] = (k * jnp.exp(-d)).astype(BF16)
            qd_scr[rs, ls] = (q * jnp.exp(bcum)).astype(BF16)
            kd_scr[rs, ls] = (k * jnp.exp(b_end - bcum)).astype(BF16)
            dec_scr[b, :, ls] = jnp.broadcast_to(jnp.exp(b_end), (SUBLANES, LANES))
    in_range = jnp.max(rng) <= EXP_RANGE

    lane = lax.broadcasted_iota(jnp.int32, (c, LANES), 1)

    def scores_direct(r0, qc, kc, bc, nh, dk):
        q = p_scr[pl.ds(r0, c), qc:qc + LANES]
        bcum = p_scr[pl.ds(r0, c), bc:bc + LANES]
        krow = lax.broadcasted_iota(jnp.int32, (LANES, nh * c), 0)
        kcol = lax.broadcasted_iota(jnp.int32, (LANES, nh * c), 1)
        sub = lax.broadcasted_iota(jnp.int32, (SUBLANES, LANES), 0)

        def row_of(col0, s):
            r8 = pl.ds(pl.multiple_of(r0 + (s // SUBLANES) * SUBLANES, SUBLANES), SUBLANES)
            return jnp.sum(jnp.where(sub == s % SUBLANES, p_scr[r8, col0:col0 + LANES], 0.0),
                           axis=0, keepdims=True)

        def body(s, sc):
            ks = row_of(kc, s)
            bs = row_of(bc, s)
            w = q * ks * jnp.exp(jnp.minimum(bcum - bs, 0.0))
            place = jnp.where(kcol == (krow // dk) * c + s, 1.0, 0.0).astype(BF16)
            return sc + _dot_exact_01(w, place)

        return lax.fori_loop(0, c, body, jnp.zeros((c, nh * c), F32))

    def attend(b, r0, direct):
        rs = pl.ds(r0, c)
        for g, (qc, kc, bc, vc, nh, dk) in enumerate(BLOCKS):
            ls = slice(g * LANES, (g + 1) * LANES)
            vw = nh * 64
            st_ref = sta_ref if g < 3 else stb_ref
            v0 = vc if g < 3 else vc - 384
            if direct:
                sc = scores_direct(r0, qc, kc, bc, nh, dk)
            else:
                kt = kt_scr[rs, ls]
                kcat = jnp.concatenate(
                    [jnp.where(lane // dk == h, kt, jnp.zeros_like(kt)) for h in range(nh)], axis=0)
                sc = _dot_nt(qt_scr[rs, ls], kcat)
            row = lax.broadcasted_iota(jnp.int32, (c, nh * c), 0)
            col = lax.broadcasted_iota(jnp.int32, (c, nh * c), 1)
            sc = jnp.where(col % c <= row, sc, 0.0).astype(BF16)
            v = v_scr[rs, vc:vc + vw]
            vlane = lax.broadcasted_iota(jnp.int32, (c, vw), 1)
            vcat = jnp.concatenate(
                [jnp.where(vlane // 64 == h, v, jnp.zeros_like(v)) for h in range(nh)], axis=0)
            st = st_ref[b, v0:v0 + vw, :]
            o = _dot(sc, vcat) + _dot_nt(qd_scr[rs, ls], st.astype(BF16))
            of_scr[rs, vc:vc + vw] = o
            upd = _dot_tn(v, kd_scr[rs, ls])
            srow = lax.broadcasted_iota(jnp.int32, (vw, LANES), 0)
            scol = lax.broadcasted_iota(jnp.int32, (vw, LANES), 1)
            st_ref[b, v0:v0 + vw, :] = (st * dec_scr[b, 0:1, ls]
                                        + jnp.where(srow // 64 == scol // dk, upd, 0.0))

    @pl.when(in_range)
    def _():
        for b in range(nb):
            attend(b, b * c, False)

    @pl.when(jnp.logical_not(in_range))
    def _():
        def body(b, carry):
            attend(b, pl.multiple_of(b * c, c), True)
            return carry
        lax.fori_loop(0, nb, body, 0)

    oa = _head_rms(of_scr[:, 0:384], A_DV) * hg_ref[...] * p_scr[:, GA:GA + 384]
    o_scr[:, 0:384] = oa.astype(BF16)
    ob = _head_rms(of_scr[:, 384:768], B_DV) * gg_ref[...] * p_scr[:, RB:RB + 384]
    o_scr[:, 384:768] = ob.astype(BF16)

    halves = C_W // LANES
    for b in range(nb):
        for j in range(halves):
            t_scr[j, b:b + nb * c:nb, :] = p_scr[b * c:(b + 1) * c, CU + LANES * j:CU + LANES * (j + 1)]
    u = jnp.concatenate([t_scr[j] for j in range(halves)], axis=1)
    xri = _dot(u.astype(BF16), bb_ref[...])
    xr_scr[...] = xri[:, 0:C_S]
    xi_scr[...] = xri[:, C_S:2 * C_S]
    ar = abar_ref[0:1, :]
    ai = abar_ref[1:2, :]

    def scan_t(t, carry):
        sr, si = carry
        ts = pl.ds(pl.multiple_of(t * nb, nb), nb)
        nr = ar * sr - ai * si + xr_scr[ts, :]
        ni = ar * si + ai * sr + xi_scr[ts, :]
        xr_scr[ts, :] = nr
        xi_scr[ts, :] = ni
        return nr, ni

    sr, si = lax.fori_loop(0, c, scan_t, (s5r_ref[...], s5i_ref[...]), unroll=2)
    s5r_ref[...] = sr
    s5i_ref[...] = si
    y = (_dot(xr_scr[...].astype(BF16), cc_ref[0:C_S, :])
         + _dot(xi_scr[...].astype(BF16), cc_ref[C_S:2 * C_S, :]) + dsk_ref[...] * u)
    z = _gelu_tanh(y)
    oc = z * _sigmoid(_dot(z.astype(BF16), wglu_ref[...]) + bglu_ref[...])
    for j in range(halves):
        t_scr[j] = oc[:, LANES * j:LANES * (j + 1)]
    for b in range(nb):
        for j in range(halves):
            o_scr[b * c:(b + 1) * c, 768 + LANES * j:768 + LANES * (j + 1)] = (
                t_scr[j, b:b + nb * c:nb, :].astype(BF16))

    out = _dot(o_scr[...], wout_ref[...])
    for b in range(nb):
        xo_ref[b] = x_ref[b] + mod_ref[b:b + 1, 2 * D:3 * D] * out[b * c:(b + 1) * c, :]


def _mixer_prompt_call(l, x, mod, ns, lp):
    nb, seq, _ = x.shape
    c = CHUNK
    rows = nb * c
    consts = [lp[k] for k in ("g1", "w_in", "lbv", "w_alpha", "b_alpha", "hgrn_g", "gla_g", "abar", "bb", "cc",
                              "d_skip", "w_glu", "b_glu", "w_out")]
    in_specs = ([pl.BlockSpec((nb, c, D), lambda i: (0, i, 0)),
                 pl.BlockSpec((None, nb, 6 * D), lambda i: (l, ns // nb, 0), pipeline_mode=pl.Buffered(1))]
                + [_layer_spec(a, l) for a in consts])
    out_shape = (jax.ShapeDtypeStruct((nb, seq, D), F32),
                 jax.ShapeDtypeStruct((nb, 384, LANES), F32),
                 jax.ShapeDtypeStruct((nb, 384, LANES), F32),
                 jax.ShapeDtypeStruct((nb, C_S), F32),
                 jax.ShapeDtypeStruct((nb, C_S), F32))
    out_specs = (pl.BlockSpec((nb, c, D), lambda i: (0, i, 0)),
                 pl.BlockSpec((nb, 384, LANES), lambda i: (0, 0, 0)),
                 pl.BlockSpec((nb, 384, LANES), lambda i: (0, 0, 0)),
                 pl.BlockSpec((nb, C_S), lambda i: (0, 0)),
                 pl.BlockSpec((nb, C_S), lambda i: (0, 0)))
    scratch = [pltpu.VMEM((rows, D), BF16), pltpu.VMEM((rows, PW), F32), pltpu.VMEM((rows, D), BF16),
               pltpu.VMEM((rows, C_S), F32), pltpu.VMEM((rows, C_S), F32),
               pltpu.VMEM((C_W // LANES, rows, LANES), F32),
               pltpu.VMEM((rows, LG_W), BF16), pltpu.VMEM((rows, LG_W), BF16),
               pltpu.VMEM((rows, LG_W), BF16), pltpu.VMEM((rows, LG_W), BF16),
               pltpu.VMEM((rows, VO_W), BF16), pltpu.VMEM((rows, VO_W), F32),
               pltpu.VMEM((nb, SUBLANES, LG_W), F32)]
    return pl.pallas_call(
        _mixer_prompt_kernel,
        grid=(seq // c,),
        in_specs=in_specs,
        out_specs=out_specs,
        out_shape=out_shape,
        scratch_shapes=scratch,
        compiler_params=pltpu.CompilerParams(
            dimension_semantics=("arbitrary",), vmem_limit_bytes=VMEM_LIMIT),
        name="mixer_prompt",
    )(x, mod, *consts)


def _step_head(s_ref, so_ref, t_scr, i, v_t):
    acc = jnp.zeros(v_t.shape, F32)
    for k in range(s_ref.shape[0]):
        new = s_ref[k] * t_scr[1, i, k:k + 1, :] + t_scr[2, i, k:k + 1, :] * v_t
        so_ref[k] = new
        acc = acc + new * t_scr[0, i, k:k + 1, :]
    return acc


def _mixer_sample_kernel(x_ref, mod_ref, g1_ref, win_ref, lbv_ref, walpha_ref, balpha_ref,
                         hgc_ref, ggc_ref, abarc_ref, bbt_ref, cc_ref, dsk_ref, wglu_ref, bglu_ref,
                         wout_ref, sa_ref, sb_ref, s5r_ref, s5i_ref, pa_ref, pb_ref, pr_ref, pi_ref,
                         xo_ref, sao_ref, sbo_ref, s5ro_ref, s5io_ref,
                         p_scr, at_scr, bt_scr, vt_scr, gate_scr, oh_scr, oc_scr):
    del pa_ref, pb_ref, pr_ref, pi_ref
    i = pl.program_id(0)

    @pl.when(i == 0)
    def _():
        h = _rms_rows(x_ref[...]) * g1_ref[...]
        h = (h * (1.0 + mod_ref[:, D:2 * D]) + mod_ref[:, 0:D]).astype(BF16)
        _project_all(h, win_ref, lbv_ref, walpha_ref, balpha_ref, 1, p_scr)
        for g, (qc, kc, bc, _, nh, dk) in enumerate(BLOCKS):
            t_scr, h0 = (at_scr, 2 * g) if g < 3 else (bt_scr, 4 * (g - 3))
            qt = p_scr[:, qc:qc + LANES].T
            dt = jnp.exp(p_scr[:, bc:bc + LANES]).T
            kt = p_scr[:, kc:kc + LANES].T
            for hh in range(nh):
                t_scr[0, h0 + hh] = qt[hh * dk:(hh + 1) * dk, :]
                t_scr[1, h0 + hh] = dt[hh * dk:(hh + 1) * dk, :]
                t_scr[2, h0 + hh] = kt[hh * dk:(hh + 1) * dk, :]
        for p in range(A_H // 2):
            for grp, vcol, gcol in ((0, VA, GA), (1, VB, RB)):
                vt = p_scr[:, vcol + LANES * p:vcol + LANES * (p + 1)].T
                gt = p_scr[:, gcol + LANES * p:gcol + LANES * (p + 1)].T
                for hh in range(2):
                    vt_scr[grp, 2 * p + hh] = vt[64 * hh:64 * (hh + 1), :]
                    gate_scr[grp, 2 * p + hh] = gt[64 * hh:64 * (hh + 1), :]
        u = p_scr[:, CU:CU + C_W]
        xri = _dot_nt(bbt_ref[...], u.astype(BF16))
        ar = abarc_ref[:, 0:1]
        ai = abarc_ref[:, 1:2]
        s0r = s5r_ref[...]
        s0i = s5i_ref[...]
        sr = ar * s0r - ai * s0i + xri[0:C_S, :]
        si = ar * s0i + ai * s0r + xri[C_S:2 * C_S, :]
        s5ro_ref[...] = sr
        s5io_ref[...] = si
        y = (_dot_tn(sr.astype(BF16), cc_ref[0:C_S, :]) + _dot_tn(si.astype(BF16), cc_ref[C_S:2 * C_S, :])
             + dsk_ref[...] * u)
        z = _gelu_tanh(y)
        oc_scr[...] = z * _sigmoid(_dot(z.astype(BF16), wglu_ref[...]) + bglu_ref[...])

    def norm_gate(o, gain_col, gate):
        return o * lax.rsqrt(jnp.mean(o * o, axis=0, keepdims=True) + EPS) * gain_col * gate

    oa = _step_head(sa_ref, sao_ref, at_scr, i, vt_scr[0, i])
    oh_scr[0, i] = norm_gate(oa, hgc_ref[...], gate_scr[0, i])
    ob = _step_head(sb_ref, sbo_ref, bt_scr, i, vt_scr[1, i])
    oh_scr[1, i] = norm_gate(ob, ggc_ref[...], gate_scr[1, i])

    @pl.when(i == A_H - 1)
    def _():
        parts = []
        for grp in range(2):
            for p in range(A_H // 2):
                two = jnp.concatenate([oh_scr[grp, 2 * p], oh_scr[grp, 2 * p + 1]], axis=0)
                parts.append(two.T)
        o = jnp.concatenate(parts + [oc_scr[...]], axis=1).astype(BF16)
        xo_ref[...] = x_ref[...] + mod_ref[:, 2 * D:3 * D] * _dot(o, wout_ref[...])


def _mixer_sample_call(l, x, mod, lp, states, prev):
    n = x.shape[0]
    consts = [lp[k] for k in ("g1", "w_in", "lbv", "w_alpha", "b_alpha", "hgrn_gc", "gla_gc", "abar_c", "bb_t",
                              "cc", "d_skip", "w_glu", "b_glu", "w_out")]
    state_specs = [pl.BlockSpec((None, None, A_DK, A_DV, n), lambda i: (l, i, 0, 0, 0)),
                   pl.BlockSpec((None, None, B_DK, B_DV, n), lambda i: (l, i, 0, 0, 0)),
                   pl.BlockSpec((None, C_S, n), lambda i: (l, 0, 0)),
                   pl.BlockSpec((None, C_S, n), lambda i: (l, 0, 0))]
    in_specs = ([pl.BlockSpec((n, D), lambda i: (0, 0), pipeline_mode=pl.Buffered(1)),
                 pl.BlockSpec((None, n, 6 * D), lambda i: (l, 0, 0), pipeline_mode=pl.Buffered(1))]
                + [_layer_spec(a, l) for a in consts] + state_specs
                + [pl.BlockSpec(memory_space=pl.ANY)] * 4)
    n_in = 2 + len(consts) + 4
    out_shape = (jax.ShapeDtypeStruct((n, D), F32),) + tuple(jax.ShapeDtypeStruct(s.shape, F32) for s in states)
    out_specs = (pl.BlockSpec((n, D), lambda i: (0, 0)),) + tuple(state_specs)
    scratch = [pltpu.VMEM((n, PW), F32), pltpu.VMEM((3, A_H, A_DK, n), F32), pltpu.VMEM((3, 8, B_DK, n), F32),
               pltpu.VMEM((2, A_H, 64, n), F32), pltpu.VMEM((2, A_H, 64, n), F32),
               pltpu.VMEM((2, A_H, 64, n), F32), pltpu.VMEM((n, C_W), F32)]
    return pl.pallas_call(
        _mixer_sample_kernel,
        grid=(A_H,),
        in_specs=in_specs,
        out_specs=out_specs,
        out_shape=out_shape,
        scratch_shapes=scratch,
        input_output_aliases={n_in + j: 1 + j for j in range(4)},
        compiler_params=pltpu.CompilerParams(
            dimension_semantics=("arbitrary",), vmem_limit_bytes=VMEM_LIMIT),
        name="mixer_sample",
    )(x, mod, *consts, *states, *prev)


def _top2_gates(logits, gidx, lane):
    neg = jnp.float32(-jnp.inf)
    big = jnp.float32(1 << 20)
    own = jnp.sum(jnp.where(lane == gidx, logits, 0.0), axis=1, keepdims=True)
    p_group = 1.0 / jnp.sum(jnp.where(lane < N_GROUPS, jnp.exp(logits - own), 0.0), axis=1, keepdims=True)
    e0 = N_GROUPS + N_PER_GROUP * gidx
    el = jnp.where((lane >= e0) & (lane < e0 + N_PER_GROUP), logits, neg)
    v1 = jnp.max(el, axis=1, keepdims=True)
    i1 = jnp.min(jnp.where(el == v1, lane, big), axis=1, keepdims=True)
    el2 = jnp.where(lane == i1, neg, el)
    v2 = jnp.max(el2, axis=1, keepdims=True)
    i2 = jnp.min(jnp.where(el2 == v2, lane, big), axis=1, keepdims=True)
    ex = jnp.exp(v2 - v1)
    w1 = 1.0 / (1.0 + ex)
    return jnp.where(lane == i1, p_group * w1, 0.0) + jnp.where(lane == i2, p_group * (ex * w1), 0.0)


def _group_experts(h, gates, lane, lane0, wg_ref, wu_ref, wd_ref):
    y = None
    for e in range(N_PER_GROUP):
        hid = _silu(_dot(h, wg_ref[e])) * _dot(h, wu_ref[e])
        gcol = jnp.sum(jnp.where(lane == lane0 + e, gates, 0.0), axis=1, keepdims=True)
        ye = _dot((hid * gcol).astype(BF16), wd_ref[e])
        y = ye if y is None else y + ye
    return y


def _moe_dense_kernel(x_ref, mod_ref, g2_ref, wr_ref, br_ref, wg_ref, wu_ref, wd_ref, fg_ref,
                      o_ref, h_scr, gate_scr, acc_scr, *, final_norm):
    g = pl.program_id(1)
    tm = x_ref.shape[0]
    lane = lax.broadcasted_iota(jnp.int32, (tm, LANES), 1).astype(F32)

    @pl.when(g == 0)
    def _():
        h = _rms_rows(x_ref[...]) * g2_ref[...]
        h = (h * (1.0 + mod_ref[:, 4 * D:5 * D]) + mod_ref[:, 3 * D:4 * D]).astype(BF16)
        h_scr[...] = h
        logits = _dot(h, wr_ref[...]) + br_ref[...]
        gl = jnp.where(lane < N_GROUPS, logits, -jnp.inf)
        gmax = jnp.max(gl, axis=1, keepdims=True)
        gidx = jnp.min(jnp.where(gl == gmax, lane, jnp.float32(1 << 20)), axis=1, keepdims=True)
        gate_scr[...] = _top2_gates(logits, gidx, lane)
        acc_scr[...] = jnp.zeros_like(acc_scr)

    lane0 = (N_GROUPS + N_PER_GROUP * g).astype(F32)
    acc_scr[...] += _group_experts(h_scr[...], gate_scr[...], lane, lane0, wg_ref, wu_ref, wd_ref)

    @pl.when(g == N_GROUPS - 1)
    def _():
        xn = x_ref[...] + mod_ref[:, 5 * D:6 * D] * acc_scr[...]
        if final_norm:
            xn = _rms_rows(xn) * fg_ref[...]
        o_ref[...] = xn


def _expert_specs(lp, l):
    def spec(arr):
        return pl.BlockSpec((None, N_PER_GROUP) + arr.shape[2:], lambda i, g, *_: (l, g, 0, 0))
    return [spec(lp["w_gate"]), spec(lp["w_up"]), spec(lp["w_down"])]


def _moe_dense_call(l, x2, mod, lp, final_g, *, final_norm):
    tm = x2.shape[0]
    kern = functools.partial(_moe_dense_kernel, final_norm=final_norm)
    return pl.pallas_call(
        kern,
        grid=(1, N_GROUPS),
        in_specs=[pl.BlockSpec((tm, D), lambda i, g: (0, 0)),
                  pl.BlockSpec((None, tm, 6 * D), lambda i, g: (l, 0, 0)),
                  _layer_spec(lp["g2"], l), _layer_spec(lp["w_router"], l), _layer_spec(lp["b_router"], l)]
        + _expert_specs(lp, l) + [pl.BlockSpec((1, D), lambda i, g: (0, 0))],
        out_specs=pl.BlockSpec((tm, D), lambda i, g: (0, 0)),
        out_shape=jax.ShapeDtypeStruct((tm, D), F32),
        scratch_shapes=[pltpu.VMEM((tm, D), BF16), pltpu.VMEM((tm, LANES), F32), pltpu.VMEM((tm, D), F32)],
        compiler_params=pltpu.CompilerParams(
            dimension_semantics=("arbitrary", "arbitrary"), vmem_limit_bytes=VMEM_LIMIT),
        name="moe_sample",
    )(x2, mod, lp["g2"], lp["w_router"], lp["b_router"], lp["w_gate"], lp["w_up"], lp["w_down"], final_g)


def _route_kernel(x_ref, mod_ref, g2_ref, wrt_ref, brc_ref, tri_ref, pos_ref, offs_ref, *, tiles_per_seq):
    tt = x_ref.shape[0]
    b = pl.program_id(0) // tiles_per_seq
    h = _rms_rows(x_ref[...]) * g2_ref[...]
    h = (h * (1.0 + mod_ref[pl.ds(b, 1), 4 * D:5 * D]) + mod_ref[pl.ds(b, 1), 3 * D:4 * D]).astype(BF16)
    lt = _dot_nt(wrt_ref[...], h) + brc_ref[...]
    row = lax.broadcasted_iota(jnp.int32, (SUBLANES, tt), 0).astype(F32)
    gl = jnp.where(row < N_GROUPS, lt[0:SUBLANES, :], -jnp.inf)
    gmax = jnp.max(gl, axis=0, keepdims=True)
    gidx = jnp.min(jnp.where(gl == gmax, row, jnp.float32(1 << 20)), axis=0, keepdims=True)
    onehot = jnp.where(row == gidx, 1.0, 0.0)
    rank = _dot(onehot.astype(BF16), tri_ref[...])
    counts = jnp.sum(onehot, axis=1, keepdims=True)
    row1 = lax.broadcasted_iota(jnp.int32, (SUBLANES, 1), 0)
    off = jnp.zeros((SUBLANES, 1), F32)
    for g in range(N_GROUPS - 1):
        off = off + jnp.where(row1 > g, counts[g:g + 1, :], 0.0)
    pos_ref[0] = jnp.sum(onehot * (rank + off), axis=0, keepdims=True).astype(jnp.int32)
    lane = lax.broadcasted_iota(jnp.int32, (1, LANES), 1)
    offs = jnp.where(lane == N_GROUPS, float(tt), 0.0)
    for g in range(1, N_GROUPS):
        offs = offs + jnp.where(lane == g, off[g:g + 1, :], 0.0)
    offs_ref[0] = offs.astype(jnp.int32)


def _route_call(l, x2, mod, ns, nb, lp, *, tt):
    t = x2.shape[0]
    nt = t // tt
    tiles_per_seq = nt // nb
    tri = jnp.asarray(np.triu(np.ones((tt, tt), np.float32), 1), BF16)
    kern = functools.partial(_route_kernel, tiles_per_seq=tiles_per_seq)
    return pl.pallas_call(
        kern,
        grid=(nt,),
        in_specs=[pl.BlockSpec((tt, D), lambda i: (i, 0)),
                  pl.BlockSpec((None, nb, 6 * D), lambda i: (l, ns // nb, 0), pipeline_mode=pl.Buffered(1)),
                  _layer_spec(lp["g2"], l), _layer_spec(lp["w_router_t"], l), _layer_spec(lp["b_router_c"], l),
                  pl.BlockSpec((tt, tt), lambda i: (0, 0), pipeline_mode=pl.Buffered(1))],
        out_specs=(pl.BlockSpec((1, 1, tt), lambda i: (i, 0, 0)),
                   pl.BlockSpec((1, 1, LANES), lambda i: (i, 0, 0))),
        out_shape=(jax.ShapeDtypeStruct((nt, 1, tt), jnp.int32),
                   jax.ShapeDtypeStruct((nt, 1, LANES), jnp.int32)),
        compiler_params=pltpu.CompilerParams(
            dimension_semantics=("arbitrary",), vmem_limit_bytes=VMEM_LIMIT),
        name="moe_route",
    )(x2, mod, lp["g2"], lp["w_router_t"], lp["b_router_c"], tri)


def _moe_sorted_kernel(pos_sm, offs_sm, x_ref, mod_ref, g2_ref, wr_ref, br_ref, wg_ref, wu_ref, wd_ref, fg_ref,
                       o_ref, a_scr, b_scr, hs_scr, gate_scr, acc_scr, *, tiles_per_seq, final_norm):
    i = pl.program_id(0)
    g = pl.program_id(1)
    tt = x_ref.shape[0]
    base = i * tt
    b = i // tiles_per_seq

    def mod(k):
        return mod_ref[pl.ds(b, 1), k * D:(k + 1) * D]

    def tile_rows(t):
        return pl.ds(pl.multiple_of(t * SUBLANES, SUBLANES), SUBLANES)

    def to_token_tiles(v):
        for s in range(SUBLANES):
            a_scr[s:tt * SUBLANES:SUBLANES, :] = v[:, s * LANES:(s + 1) * LANES]

    def from_token_tiles():
        return jnp.concatenate([b_scr[s:tt * SUBLANES:SUBLANES, :] for s in range(SUBLANES)], axis=1)

    lane = lax.broadcasted_iota(jnp.int32, (tt, LANES), 1).astype(F32)

    @pl.when(g == 0)
    def _():
        h = _rms_rows(x_ref[...]) * g2_ref[...]
        to_token_tiles(h * (1.0 + mod(4)) + mod(3))

        def scatter(t, carry):
            b_scr[tile_rows(pos_sm[base + t]), :] = a_scr[tile_rows(t), :]
            return carry
        lax.fori_loop(0, tt, scatter, 0, unroll=8)
        hs = from_token_tiles().astype(BF16)
        hs_scr[0:tt, :] = hs
        hs_scr[tt:tt + MOE_BLOCK, :] = jnp.zeros((MOE_BLOCK, D), BF16)
        logits = _dot(hs, wr_ref[...]) + br_ref[...]
        slot = lax.broadcasted_iota(jnp.int32, (tt, 1), 0)
        gidx = jnp.zeros((tt, 1), F32)
        for gg in range(1, N_GROUPS):
            gidx = gidx + jnp.where(slot >= offs_sm[i * LANES + gg], 1.0, 0.0)
        gate_scr[0:tt, :] = _top2_gates(logits, gidx, lane)
        gate_scr[tt:tt + MOE_BLOCK, :] = jnp.zeros((MOE_BLOCK, LANES), F32)
        acc_scr[...] = jnp.zeros_like(acc_scr)

    lo = offs_sm[i * LANES + g]
    hi = offs_sm[i * LANES + g + 1]
    start = (lo // BF16_ROWS) * BF16_ROWS
    nblk = jnp.where(hi > lo, (hi - start + MOE_BLOCK - 1) // MOE_BLOCK, 0)
    lane0 = (N_GROUPS + N_PER_GROUP * g).astype(F32)
    blk_lane = lax.broadcasted_iota(jnp.int32, (MOE_BLOCK, LANES), 1).astype(F32)

    def block(k, carry):
        rs = pl.ds(pl.multiple_of(start + k * MOE_BLOCK, BF16_ROWS), MOE_BLOCK)
        acc_scr[rs, :] += _group_experts(hs_scr[rs, :], gate_scr[rs, :], blk_lane, lane0, wg_ref, wu_ref, wd_ref)
        return carry
    lax.fori_loop(0, nblk, block, 0)

    @pl.when(g == N_GROUPS - 1)
    def _():
        to_token_tiles(acc_scr[0:tt, :])

        def gather(t, carry):
            b_scr[tile_rows(t), :] = a_scr[tile_rows(pos_sm[base + t]), :]
            return carry
        lax.fori_loop(0, tt, gather, 0, unroll=8)
        xn = x_ref[...] + mod(5) * from_token_tiles()
        if final_norm:
            xn = _rms_rows(xn) * fg_ref[...]
        o_ref[...] = xn


def _moe_sorted_call(l, x2, mod, ns, nb, lp, final_g, pos, offs, *, tt, final_norm):
    t = x2.shape[0]
    nt = t // tt
    kern = functools.partial(_moe_sorted_kernel, tiles_per_seq=nt // nb, final_norm=final_norm)
    grid_spec = pltpu.PrefetchScalarGridSpec(
        num_scalar_prefetch=2,
        grid=(nt, N_GROUPS),
        in_specs=[pl.BlockSpec((tt, D), lambda i, g, *_: (i, 0), pipeline_mode=pl.Buffered(1)),
                  pl.BlockSpec((None, nb, 6 * D), lambda i, g, *_: (l, ns // nb, 0), pipeline_mode=pl.Buffered(1)),
                  _layer_spec(lp["g2"], l), _layer_spec(lp["w_router"], l), _layer_spec(lp["b_router"], l)]
        + _expert_specs(lp, l) + [pl.BlockSpec((1, D), lambda i, g, *_: (0, 0))],
        out_specs=pl.BlockSpec((tt, D), lambda i, g, *_: (i, 0)),
        scratch_shapes=[pltpu.VMEM((tt * SUBLANES, LANES), F32), pltpu.VMEM((tt * SUBLANES, LANES), F32),
                        pltpu.VMEM((tt + MOE_BLOCK, D), BF16), pltpu.VMEM((tt + MOE_BLOCK, LANES), F32),
                        pltpu.VMEM((tt + MOE_BLOCK, D), F32)])
    return pl.pallas_call(
        kern,
        grid_spec=grid_spec,
        out_shape=jax.ShapeDtypeStruct((t, D), F32),
        compiler_params=pltpu.CompilerParams(
            dimension_semantics=("arbitrary", "arbitrary"), vmem_limit_bytes=VMEM_LIMIT),
        name="moe_prompt",
    )(pos, offs, x2, mod, lp["g2"], lp["w_router"], lp["b_router"], lp["w_gate"], lp["w_up"], lp["w_down"],
      final_g)


def _pad_to(a, axis, size):
    pad = [(0, 0)] * a.ndim
    pad[axis] = (0, size - a.shape[axis])
    return jnp.pad(a, pad)


def _prepare_params(p):
    lb_cum = jnp.cumsum(jax.nn.softmax(p["hgrn_lb_logits"].astype(F32), axis=0), axis=0)
    lb = lb_cum - lb_cum[:1]
    lbv = _pad_to(jnp.stack([jnp.log(lb), jnp.log1p(-lb), 1.0 - lb], axis=1), 1, SUBLANES)
    a_re, a_im = p["s5_a_re"].astype(F32), p["s5_a_im"].astype(F32)
    dt = jnp.exp(p["s5_log_dt"].astype(F32))[..., None]
    mag = jnp.exp(a_re * dt)
    abar_re = mag * jnp.cos(a_im * dt)
    abar_im = mag * jnp.sin(a_im * dt)
    den = a_re * a_re + a_im * a_im
    nr, ni = abar_re - 1.0, abar_im
    coef_re = ((nr * a_re + ni * a_im) / den)[..., None]
    coef_im = ((ni * a_re - nr * a_im) / den)[..., None]
    b_re, b_im = p["s5_b_re"].astype(F32), p["s5_b_im"].astype(F32)
    bb_re = coef_re * b_re - coef_im * b_im
    bb_im = coef_re * b_im + coef_im * b_re
    eye = jnp.eye(C_G, dtype=F32)
    blk_b = lambda m: jnp.einsum("lgph,gk->lghkp", m, eye).reshape(DEPTH, C_W, C_S)
    blk_bt = lambda m: jnp.einsum("lgph,gk->lgpkh", m, eye).reshape(DEPTH, C_S, C_W)
    blk_c = lambda m: jnp.einsum("lghp,gk->lgpkh", m, eye).reshape(DEPTH, C_S, C_W)
    bb = jnp.concatenate([blk_b(bb_re), blk_b(bb_im)], axis=2).astype(BF16)
    bb_t = jnp.concatenate([blk_bt(bb_re), blk_bt(bb_im)], axis=1).astype(BF16)
    cc = jnp.concatenate([blk_c(p["s5_c_re"].astype(F32)), -blk_c(p["s5_c_im"].astype(F32))],
                         axis=1).astype(BF16)
    abar_rows = jnp.stack([abar_re.reshape(DEPTH, C_S), abar_im.reshape(DEPTH, C_S)], axis=1)
    router = jnp.concatenate([p["moe_w_group"], p["moe_w_expert"]], axis=2)
    b_router = _pad_to(jnp.concatenate([p["moe_b_group"], p["moe_b_expert"]], axis=1), 1, LANES)
    row = lambda a: a.astype(F32)[:, None, :]
    col = lambda a: a.astype(F32)[:, :, None]
    return dict(
        g1=row(p["norm1_g"]), g2=row(p["norm2_g"]),
        w_in=jnp.transpose(p["w_in"], (0, 2, 1)).astype(BF16), lbv=lbv,
        w_alpha=_pad_to(_pad_to(p["gla_w_alpha"], 1, LANES), 2, 256).astype(BF16),
        b_alpha=_pad_to(row(p["gla_b_alpha"]), 2, 256),
        hgrn_g=row(jnp.tile(p["hgrn_norm_g"], (1, A_H))), gla_g=row(jnp.tile(p["gla_norm_g"], (1, B_H))),
        hgrn_gc=col(p["hgrn_norm_g"]), gla_gc=col(p["gla_norm_g"]),
        abar=_pad_to(abar_rows, 1, SUBLANES), abar_c=jnp.transpose(abar_rows, (0, 2, 1)),
        bb=bb, bb_t=bb_t, cc=cc, d_skip=row(p["s5_d"]),
        w_glu=p["s5_w_glu"].astype(BF16), b_glu=row(p["s5_b_glu"]), w_out=p["w_out"].astype(BF16),
        w_gate=p["moe_w_gate"].astype(BF16), w_up=p["moe_w_up"].astype(BF16),
        w_down=p["moe_w_down"].astype(BF16),
        w_router=_pad_to(router, 2, LANES).astype(BF16), b_router=b_router[:, None, :],
        w_router_t=_pad_to(jnp.transpose(router, (0, 2, 1)), 1, LANES).astype(BF16),
        b_router_c=b_router[:, :, None])


def kernel(x_prompt, x_sample, state_hgrn, state_gla, state_s5_re, state_s5_im, c_prompt, c_sample, ada_w, ada_b, norm1_g, norm2_g, w_in, hgrn_lb_logits, hgrn_norm_g, gla_w_alpha, gla_b_alpha, gla_norm_g, s5_a_re, s5_a_im, s5_b_re, s5_b_im, s5_c_re, s5_c_im, s5_d, s5_log_dt, s5_w_glu, s5_b_glu, w_out, moe_w_group, moe_b_group, moe_w_expert, moe_b_expert, moe_w_gate, moe_w_up, moe_w_down, final_norm_g):
    lp = _prepare_params(dict(
        norm1_g=norm1_g, norm2_g=norm2_g, w_in=w_in, hgrn_lb_logits=hgrn_lb_logits, hgrn_norm_g=hgrn_norm_g,
        gla_w_alpha=gla_w_alpha, gla_b_alpha=gla_b_alpha, gla_norm_g=gla_norm_g, s5_a_re=s5_a_re,
        s5_a_im=s5_a_im, s5_b_re=s5_b_re, s5_b_im=s5_b_im, s5_c_re=s5_c_re, s5_c_im=s5_c_im, s5_d=s5_d,
        s5_log_dt=s5_log_dt, s5_w_glu=s5_w_glu, s5_b_glu=s5_b_glu, w_out=w_out, moe_w_group=moe_w_group,
        moe_b_group=moe_b_group, moe_w_expert=moe_w_expert, moe_b_expert=moe_b_expert,
        moe_w_gate=moe_w_gate, moe_w_up=moe_w_up, moe_w_down=moe_w_down))
    nb, seq, _ = x_prompt.shape
    ns = x_sample.shape[0]
    assert ns % nb == 0 and seq % CHUNK == 0
    tt = min(MOE_TILE, seq)
    assert seq % tt == 0 and tt % BF16_ROWS == 0
    final_g = final_norm_g.reshape(1, D)

    mod = _ada_call(jnp.concatenate([c_sample, c_prompt], axis=0), ada_w, ada_b)
    states = (jnp.transpose(state_hgrn, (0, 2, 3, 4, 1)), jnp.transpose(state_gla, (0, 2, 3, 4, 1)),
              jnp.transpose(state_s5_re, (0, 2, 3, 1)).reshape(DEPTH, C_S, ns),
              jnp.transpose(state_s5_im, (0, 2, 3, 1)).reshape(DEPTH, C_S, ns))
    new_states = tuple(jnp.zeros_like(s) for s in states)

    xp = x_prompt
    xs = x_sample.reshape(ns, D)
    outs = {k: [] for k in ("hg_p", "gl_p", "sr_p", "si_p")}
    for l in range(DEPTH):
        last = l == DEPTH - 1
        xp, sta, stb, s5r, s5i = _mixer_prompt_call(l, xp, mod, ns, lp)
        x2 = xp.reshape(nb * seq, D)
        pos, offs = _route_call(l, x2, mod, ns, nb, lp, tt=tt)
        xp = _moe_sorted_call(l, x2, mod, ns, nb, lp, final_g, pos.reshape(-1), offs.reshape(-1),
                              tt=tt, final_norm=last).reshape(nb, seq, D)
        sta = sta.reshape(nb, A_H // 2, 2, A_DV, 2, A_DK)
        outs["hg_p"].append(jnp.stack([sta[:, h // 2, h % 2, :, h % 2, :] for h in range(A_H)], axis=1)
                            .transpose(0, 1, 3, 2))
        stb = stb.reshape(nb, B_H, B_DV, 4, B_DK)
        outs["gl_p"].append(jnp.stack([stb[:, h, :, h % 4, :] for h in range(B_H)], axis=1)
                            .transpose(0, 1, 3, 2))
        outs["sr_p"].append(s5r.reshape(nb, C_G, C_P))
        outs["si_p"].append(s5i.reshape(nb, C_G, C_P))
        xs, *new_states = _mixer_sample_call(l, xs, mod, lp, states, new_states)
        xs = _moe_dense_call(l, xs, mod, lp, final_g, final_norm=last)

    hg_s, gl_s, sr_s, si_s = new_states
    return (xp, xs.reshape(ns, 1, D),
            jnp.stack(outs["hg_p"]), jnp.stack(outs["gl_p"]), jnp.stack(outs["sr_p"]), jnp.stack(outs["si_p"]),
            jnp.transpose(hg_s, (0, 4, 1, 2, 3)), jnp.transpose(gl_s, (0, 4, 1, 2, 3)),
            jnp.transpose(sr_s.reshape(DEPTH, C_G, C_P, ns), (0, 3, 1, 2)),
            jnp.transpose(si_s.reshape(DEPTH, C_G, C_P, ns), (0, 3, 1, 2)))
```

```python
import functools
import math

import numpy as np
import jax
import jax.numpy as jnp
from jax import lax
from jax.experimental import pallas as pl
from jax.experimental.pallas import tpu as pltpu

F32 = jnp.float32
BF16 = jnp.bfloat16

D = 1024
DEPTH = 4
A_H, A_DK, A_DV = 6, 64, 64
B_H, B_DK, B_DV = 6, 32, 64
B_RANK = 16
B_TAU = 16.0
C_G, C_CH, C_P = 16, 16, 64
C_W = C_G * C_CH
C_S = C_G * C_P
N_GROUPS, N_PER_GROUP, D_EXP = 4, 4, 256
N_EXP = N_GROUPS * N_PER_GROUP
EPS = 1e-6
CHUNK = 64
LANES = 128
SUBLANES = 8
EXP_RANGE = 80.0
MOE_TILE = 1024
MOE_BLOCK = 320
BF16_ROWS = 16

W_AQ, W_AF, W_AI, W_AG = 0, 384, 768, 1152
W_BQ, W_BK, W_BV, W_LR, W_BR, W_CU = 1536, 1728, 1920, 2304, 2320, 2704
IN_COLS = 2960
W_SEGMENTS = {"aq|af": (W_AQ, 768), "ai|ag": (W_AI, 768), "bq": (W_BQ, 256), "bk": (W_BK, 256),
              "bv|lr": (W_BV, 512), "br|cu": (W_BR, 640)}
QA, FA, VA, GA = 0, 384, 768, 1152
QB, KB, VB, RB = 1536, 1792, 2048, 2432
CU, LB, KA = 2816, 3072, 3328
PW = 3712
N_LG = 5
LG_W = N_LG * LANES
BLOCKS = tuple((QA + LANES * p, KA + LANES * p, FA + LANES * p, LANES * p, 2, A_DK) for p in range(3)) + (
    (QB, KB, LB, 384, 4, B_DK), (QB + LANES, KB + LANES, LB + LANES, 640, 2, B_DK))
VO_W = 768

VMEM_LIMIT = 56 * 1024 * 1024


def _sigmoid(x):
    return 1.0 / (1.0 + jnp.exp(-x))


def _silu(x):
    return x * _sigmoid(x)


def _log_sigmoid(x):
    return jnp.minimum(x, 0.0) - jnp.log1p(jnp.exp(-jnp.abs(x)))


def _gelu_tanh(x):
    return 0.5 * x * (1.0 + jnp.tanh(math.sqrt(2.0 / math.pi) * (x + 0.044715 * x * x * x)))


def _split3(x):
    hi = x.astype(BF16)
    r = x - hi.astype(F32)
    mid = r.astype(BF16)
    lo = (r - mid.astype(F32)).astype(BF16)
    return hi, mid, lo


def _dot(a, b):
    return jnp.dot(a, b, preferred_element_type=F32)


def _dot_nt(a, b):
    return lax.dot_general(a, b, (((1,), (1,)), ((), ())), preferred_element_type=F32)


def _dot_tn(a, b):
    return lax.dot_general(a, b, (((0,), (0,)), ((), ())), preferred_element_type=F32)


def _dot_exact_01(x, m01):
    hi, mid, lo = _split3(x)
    return _dot(hi, m01) + _dot(mid, m01) + _dot(lo, m01)


def _rms_rows(x):
    return x * lax.rsqrt(jnp.mean(x * x, axis=-1, keepdims=True) + EPS)


def _forget_gate(af, lbv):
    e = jnp.exp(-jnp.abs(af))
    log_sig = jnp.minimum(af, 0.0) - jnp.log1p(e)
    a = lbv[0:1]
    b = lbv[1:2] + log_sig
    log_f = jnp.maximum(a, b) + jnp.log1p(jnp.exp(-jnp.abs(a - b)))
    k = lbv[2:3] * (jnp.where(af >= 0.0, e, 1.0) / (1.0 + e))
    return log_f, k


def _chunk_cumsum(x, c):
    pos = lax.broadcasted_iota(jnp.int32, x.shape, 0) % c
    s = 1
    while s < c:
        x = x + jnp.where(pos >= s, pltpu.roll(x, s, 0), 0.0)
        s *= 2
    return x


def _head_rms(o, dv):
    w = o.shape[1]
    r = lax.broadcasted_iota(jnp.int32, (w, w), 0)
    cidx = lax.broadcasted_iota(jnp.int32, (w, w), 1)
    ones_blk = jnp.where(r // dv == cidx // dv, 1.0, 0.0).astype(BF16)
    return o * lax.rsqrt(_dot((o * o).astype(BF16), ones_blk) * (1.0 / dv) + EPS)


def _project_all(h, wint_ref, lbv_ref, walpha_ref, balpha_ref, chunk, p_scr):
    def proj(name):
        r0, n = W_SEGMENTS[name]
        return _dot_nt(h, wint_ref[r0:r0 + n, :])

    def cum(x):
        return x if chunk == 1 else _chunk_cumsum(x, chunk)

    aqf = proj("aq|af")
    p_scr[:, QA:QA + 384] = _silu(aqf[:, 0:384])
    log_f, ka = _forget_gate(aqf[:, 384:768], lbv_ref[...])
    p_scr[:, FA:FA + 384] = cum(log_f)
    p_scr[:, KA:KA + 384] = ka
    aig = proj("ai|ag")
    p_scr[:, VA:VA + 384] = aig[:, 0:384]
    p_scr[:, GA:GA + 384] = _silu(aig[:, 384:768])
    p_scr[:, QB:QB + 256] = proj("bq") * (B_DK ** -0.5)
    kb = proj("bk")
    p_scr[:, KB:KB + 256] = jnp.where(lax.broadcasted_iota(jnp.int32, kb.shape, 1) < B_H * B_DK, kb, 0.0)
    bvlr = proj("bv|lr")
    p_scr[:, VB:VB + 384] = bvlr[:, 0:384]
    z = _dot(bvlr[:, 384:512].astype(BF16), walpha_ref[...]) + balpha_ref[...]
    p_scr[:, LB:LB + 256] = cum(_log_sigmoid(z) * (1.0 / B_TAU))
    brcu = proj("br|cu")
    p_scr[:, RB:RB + 384] = _silu(brcu[:, 0:384])
    p_scr[:, CU:CU + 256] = brcu[:, 384:640]


def _layer_spec(arr, l):
    nd = arr.ndim - 1
    return pl.BlockSpec((None,) + arr.shape[1:], lambda *_, _n=nd: (l,) + (0,) * _n,
                        pipeline_mode=pl.Buffered(1))


def _ada_kernel(c_ref, w_ref, b_ref, o_ref):
    sc = _silu(c_ref[...]).astype(BF16)
    o_ref[0] = _dot(sc, w_ref[0].astype(BF16)) + b_ref[0]


def _ada_call(c_all, ada_w, ada_b):
    n = c_all.shape[0]
    tn = 1536
    return pl.pallas_call(
        _ada_kernel,
        grid=(DEPTH, 6 * D // tn),
        in_specs=[
            pl.BlockSpec((n, D), lambda l, j: (0, 0)),
            pl.BlockSpec((1, D, tn), lambda l, j: (l, 0, j)),
            pl.BlockSpec((1, 1, tn), lambda l, j: (l, 0, j)),
        ],
        out_specs=pl.BlockSpec((1, n, tn), lambda l, j: (l, 0, j)),
        out_shape=jax.ShapeDtypeStruct((DEPTH, n, 6 * D), F32),
        compiler_params=pltpu.CompilerParams(
            dimension_semantics=("arbitrary", "arbitrary"), vmem_limit_bytes=VMEM_LIMIT),
        name="ada_mod",
    )(c_all, ada_w, ada_b.reshape(DEPTH, 1, 6 * D))


def _mixer_prompt_kernel(x_ref, mod_ref, g1_ref, win_ref, lbv_ref, walpha_ref, balpha_ref,
                         hg_ref, gg_ref, abar_ref, bb_ref, cc_ref, dsk_ref, wglu_ref, bglu_ref,
                         wout_ref,
                         xo_ref, sta_ref, stb_ref, s5r_ref, s5i_ref,
                         h_scr, p_scr, o_scr, xr_scr, xi_scr, t_scr,
                         qt_scr, kt_scr, qd_scr, kd_scr, v_scr, of_scr, dec_scr):
    nb = x_ref.shape[0]
    c = CHUNK
    i = pl.program_id(0)

    @pl.when(i == 0)
    def _():
        sta_ref[...] = jnp.zeros_like(sta_ref)
        stb_ref[...] = jnp.zeros_like(stb_ref)
        s5r_ref[...] = jnp.zeros_like(s5r_ref)
        s5i_ref[...] = jnp.zeros_like(s5i_ref)

    for b in range(nb):
        hb = _rms_rows(x_ref[b]) * g1_ref[...]
        hb = hb * (1.0 + mod_ref[b:b + 1, D:2 * D]) + mod_ref[b:b + 1, 0:D]
        h_scr[b * c:(b + 1) * c, :] = hb.astype(BF16)

    _project_all(h_scr[...], win_ref, lbv_ref, walpha_ref, balpha_ref, c, p_scr)

    v_scr[:, 0:384] = p_scr[:, VA:VA + 384].astype(BF16)
    v_scr[:, 384:768] = p_scr[:, VB:VB + 384].astype(BF16)
    rng = jnp.zeros((c, LANES), F32)
    for b in range(nb):
        rs = slice(b * c, (b + 1) * c)
        for g, (qc, kc, bc, _, _, _) in enumerate(BLOCKS):
            ls = slice(g * LANES, (g + 1) * LANES)
            q = p_scr[rs, qc:qc + LANES]
            k = p_scr[rs, kc:kc + LANES]
            bcum = p_scr[rs, bc:bc + LANES]
            b_mid = bcum[c // 2 - 1:c // 2, :]
            b_end = bcum[c - 1:c, :]
            d = bcum - b_mid
            rng = jnp.maximum(rng, jnp.abs(d))
            qt_scr[rs, ls] = (q * jnp.exp(d)).astype(BF16)
            kt_scr[rs, ls] = (k * jnp.exp(-d)).astype(BF16)
            qd_scr[rs, ls] = (q * jnp.exp(bcum)).astype(BF16)
            kd_scr[rs, ls] = (k * jnp.exp(b_end - bcum)).astype(BF16)
            dec_scr[b, :, ls] = jnp.broadcast_to(jnp.exp(b_end), (SUBLANES, LANES))
    in_range = jnp.max(rng) <= EXP_RANGE

    lane = lax.broadcasted_iota(jnp.int32, (c, LANES), 1)

    def scores_direct(r0, qc, kc, bc, nh, dk):
        q = p_scr[pl.ds(r0, c), qc:qc + LANES]
        bcum = p_scr[pl.ds(r0, c), bc:bc + LANES]
        krow = lax.broadcasted_iota(jnp.int32, (LANES, nh * c), 0)
        kcol = lax.broadcasted_iota(jnp.int32, (LANES, nh * c), 1)
        sub = lax.broadcasted_iota(jnp.int32, (SUBLANES, LANES), 0)

        def row_of(col0, s):
            r8 = pl.ds(pl.multiple_of(r0 + (s // SUBLANES) * SUBLANES, SUBLANES), SUBLANES)
            return jnp.sum(jnp.where(sub == s % SUBLANES, p_scr[r8, col0:col0 + LANES], 0.0),
                           axis=0, keepdims=True)

        def body(s, sc):
            ks = row_of(kc, s)
            bs = row_of(bc, s)
            w = q * ks * jnp.exp(jnp.minimum(bcum - bs, 0.0))
            place = jnp.where(kcol == (krow // dk) * c + s, 1.0, 0.0).astype(BF16)
            return sc + _dot_exact_01(w, place)

        return lax.fori_loop(0, c, body, jnp.zeros((c, nh * c), F32))

    def attend(b, r0, direct):
        rs = pl.ds(r0, c)
        for g, (qc, kc, bc, vc, nh, dk) in enumerate(BLOCKS):
            ls = slice(g * LANES, (g + 1) * LANES)
            vw = nh * 64
            st_ref = sta_ref if g < 3 else stb_ref
            v0 = vc if g < 3 else vc - 384
            if direct:
                sc = scores_direct(r0, qc, kc, bc, nh, dk)
            else:
                kt = kt_scr[rs, ls]
                kcat = jnp.concatenate(
                    [jnp.where(lane // dk == h, kt, jnp.zeros_like(kt)) for h in range(nh)], axis=0)
                sc = _dot_nt(qt_scr[rs, ls], kcat)
            row = lax.broadcasted_iota(jnp.int32, (c, nh * c), 0)
            col = lax.broadcasted_iota(jnp.int32, (c, nh * c), 1)
            sc = jnp.where(col % c <= row, sc, 0.0).astype(BF16)
            v = v_scr[rs, vc:vc + vw]
            vlane = lax.broadcasted_iota(jnp.int32, (c, vw), 1)
            vcat = jnp.concatenate(
                [jnp.where(vlane // 64 == h, v, jnp.zeros_like(v)) for h in range(nh)], axis=0)
            st = st_ref[b, v0:v0 + vw, :]
            o = _dot(sc, vcat) + _dot_nt(qd_scr[rs, ls], st.astype(BF16))
            of_scr[rs, vc:vc + vw] = o
            upd = _dot_tn(v, kd_scr[rs, ls])
            srow = lax.broadcasted_iota(jnp.int32, (vw, LANES), 0)
            scol = lax.broadcasted_iota(jnp.int32, (vw, LANES), 1)
            st_ref[b, v0:v0 + vw, :] = (st * dec_scr[b, 0:1, ls]
                                        + jnp.where(srow // 64 == scol // dk, upd, 0.0))

    @pl.when(in_range)
    def _():
        for b in range(nb):
            attend(b, b * c, False)

    @pl.when(jnp.logical_not(in_range))
    def _():
        def body(b, carry):
            attend(b, pl.multiple_of(b * c, c), True)
            return carry
        lax.fori_loop(0, nb, body, 0)

    oa = _head_rms(of_scr[:, 0:384], A_DV) * hg_ref[...] * p_scr[:, GA:GA + 384]
    o_scr[:, 0:384] = oa.astype(BF16)
    ob = _head_rms(of_scr[:, 384:768], B_DV) * gg_ref[...] * p_scr[:, RB:RB + 384]
    o_scr[:, 384:768] = ob.astype(BF16)

    halves = C_W // LANES
    for b in range(nb):
        for j in range(halves):
            t_scr[j, b:b + nb * c:nb, :] = p_scr[b * c:(b + 1) * c, CU + LANES * j:CU + LANES * (j + 1)]
    u = jnp.concatenate([t_scr[j] for j in range(halves)], axis=1)
    xri = _dot(u.astype(BF16), bb_ref[...])
    xr_scr[...] = xri[:, 0:C_S]
    xi_scr[...] = xri[:, C_S:2 * C_S]
    ar = abar_ref[0:1, :]
    ai = abar_ref[1:2, :]

    def scan_t(t, carry):
        sr, si = carry
        ts = pl.ds(pl.multiple_of(t * nb, nb), nb)
        nr = ar * sr - ai * si + xr_scr[ts, :]
        ni = ar * si + ai * sr + xi_scr[ts, :]
        xr_scr[ts, :] = nr
        xi_scr[ts, :] = ni
        return nr, ni

    sr, si = lax.fori_loop(0, c, scan_t, (s5r_ref[...], s5i_ref[...]), unroll=2)
    s5r_ref[...] = sr
    s5i_ref[...] = si
    y = (_dot(xr_scr[...].astype(BF16), cc_ref[0:C_S, :])
         + _dot(xi_scr[...].astype(BF16), cc_ref[C_S:2 * C_S, :]) + dsk_ref[...] * u)
    z = _gelu_tanh(y)
    oc = z * _sigmoid(_dot(z.astype(BF16), wglu_ref[...]) + bglu_ref[...])
    for j in range(halves):
        t_scr[j] = oc[:, LANES * j:LANES * (j + 1)]
    for b in range(nb):
        for j in range(halves):
            o_scr[b * c:(b + 1) * c, 768 + LANES * j:768 + LANES * (j + 1)] = (
                t_scr[j, b:b + nb * c:nb, :].astype(BF16))

    out = _dot(o_scr[...], wout_ref[...])
    for b in range(nb):
        xo_ref[b] = x_ref[b] + mod_ref[b:b + 1, 2 * D:3 * D] * out[b * c:(b + 1) * c, :]


def _mixer_prompt_call(l, x, mod, ns, lp):
    nb, seq, _ = x.shape
    c = CHUNK
    rows = nb * c
    consts = [lp[k] for k in ("g1", "w_in", "lbv", "w_alpha", "b_alpha", "hgrn_g", "gla_g", "abar", "bb", "cc",
                              "d_skip", "w_glu", "b_glu", "w_out")]
    in_specs = ([pl.BlockSpec((nb, c, D), lambda i: (0, i, 0)),
                 pl.BlockSpec((None, nb, 6 * D), lambda i: (l, ns // nb, 0), pipeline_mode=pl.Buffered(1))]
                + [_layer_spec(a, l) for a in consts])
    out_shape = (jax.ShapeDtypeStruct((nb, seq, D), F32),
                 jax.ShapeDtypeStruct((nb, 384, LANES), F32),
                 jax.ShapeDtypeStruct((nb, 384, LANES), F32),
                 jax.ShapeDtypeStruct((nb, C_S), F32),
                 jax.ShapeDtypeStruct((nb, C_S), F32))
    out_specs = (pl.BlockSpec((nb, c, D), lambda i: (0, i, 0)),
                 pl.BlockSpec((nb, 384, LANES), lambda i: (0, 0, 0)),
                 pl.BlockSpec((nb, 384, LANES), lambda i: (0, 0, 0)),
                 pl.BlockSpec((nb, C_S), lambda i: (0, 0)),
                 pl.BlockSpec((nb, C_S), lambda i: (0, 0)))
    scratch = [pltpu.VMEM((rows, D), BF16), pltpu.VMEM((rows, PW), F32), pltpu.VMEM((rows, D), BF16),
               pltpu.VMEM((rows, C_S), F32), pltpu.VMEM((rows, C_S), F32),
               pltpu.VMEM((C_W // LANES, rows, LANES), F32),
               pltpu.VMEM((rows, LG_W), BF16), pltpu.VMEM((rows, LG_W), BF16),
               pltpu.VMEM((rows, LG_W), BF16), pltpu.VMEM((rows, LG_W), BF16),
               pltpu.VMEM((rows, VO_W), BF16), pltpu.VMEM((rows, VO_W), F32),
               pltpu.VMEM((nb, SUBLANES, LG_W), F32)]
    return pl.pallas_call(
        _mixer_prompt_kernel,
        grid=(seq // c,),
        in_specs=in_specs,
        out_specs=out_specs,
        out_shape=out_shape,
        scratch_shapes=scratch,
        compiler_params=pltpu.CompilerParams(
            dimension_semantics=("arbitrary",), vmem_limit_bytes=VMEM_LIMIT),
        name="mixer_prompt",
    )(x, mod, *consts)


def _step_head(s_ref, so_ref, t_scr, i, v_t):
    acc = jnp.zeros(v_t.shape, F32)
    for k in range(s_ref.shape[0]):
        new = s_ref[k] * t_scr[1, i, k:k + 1, :] + t_scr[2, i, k:k + 1, :] * v_t
        so_ref[k] = new
        acc = acc + new * t_scr[0, i, k:k + 1, :]
    return acc


def _mixer_sample_kernel(x_ref, mod_ref, g1_ref, win_ref, lbv_ref, walpha_ref, balpha_ref,
                         hgc_ref, ggc_ref, abarc_ref, bbt_ref, cc_ref, dsk_ref, wglu_ref, bglu_ref,
                         wout_ref, sa_ref, sb_ref, s5r_ref, s5i_ref, pa_ref, pb_ref, pr_ref, pi_ref,
                         xo_ref, sao_ref, sbo_ref, s5ro_ref, s5io_ref,
                         p_scr, at_scr, bt_scr, vt_scr, gate_scr, oh_scr, oc_scr):
    del pa_ref, pb_ref, pr_ref, pi_ref
    i = pl.program_id(0)

    @pl.when(i == 0)
    def _():
        h = _rms_rows(x_ref[...]) * g1_ref[...]
        h = (h * (1.0 + mod_ref[:, D:2 * D]) + mod_ref[:, 0:D]).astype(BF16)
        _project_all(h, win_ref, lbv_ref, walpha_ref, balpha_ref, 1, p_scr)
        for g, (qc, kc, bc, _, nh, dk) in enumerate(BLOCKS):
            t_scr, h0 = (at_scr, 2 * g) if g < 3 else (bt_scr, 4 * (g - 3))
            qt = p_scr[:, qc:qc + LANES].T
            dt = jnp.exp(p_scr[:, bc:bc + LANES]).T
            kt = p_scr[:, kc:kc + LANES].T
            for hh in range(nh):
                t_scr[0, h0 + hh] = qt[hh * dk:(hh + 1) * dk, :]
                t_scr[1, h0 + hh] = dt[hh * dk:(hh + 1) * dk, :]
                t_scr[2, h0 + hh] = kt[hh * dk:(hh + 1) * dk, :]
        for p in range(A_H // 2):
            for grp, vcol, gcol in ((0, VA, GA), (1, VB, RB)):
                vt = p_scr[:, vcol + LANES * p:vcol + LANES * (p + 1)].T
                gt = p_scr[:, gcol + LANES * p:gcol + LANES * (p + 1)].T
                for hh in range(2):
                    vt_scr[grp, 2 * p + hh] = vt[64 * hh:64 * (hh + 1), :]
                    gate_scr[grp, 2 * p + hh] = gt[64 * hh:64 * (hh + 1), :]
        u = p_scr[:, CU:CU + C_W]
        xri = _dot_nt(bbt_ref[...], u.astype(BF16))
        ar = abarc_ref[:, 0:1]
        ai = abarc_ref[:, 1:2]
        s0r = s5r_ref[...]
        s0i = s5i_ref[...]
        sr = ar * s0r - ai * s0i + xri[0:C_S, :]
        si = ar * s0i + ai * s0r + xri[C_S:2 * C_S, :]
        s5ro_ref[...] = sr
        s5io_ref[...] = si
        y = (_dot_tn(sr.astype(BF16), cc_ref[0:C_S, :]) + _dot_tn(si.astype(BF16), cc_ref[C_S:2 * C_S, :])
             + dsk_ref[...] * u)
        z = _gelu_tanh(y)
        oc_scr[...] = z * _sigmoid(_dot(z.astype(BF16), wglu_ref[...]) + bglu_ref[...])

    def norm_gate(o, gain_col, gate):
        return o * lax.rsqrt(jnp.mean(o * o, axis=0, keepdims=True) + EPS) * gain_col * gate

    oa = _step_head(sa_ref, sao_ref, at_scr, i, vt_scr[0, i])
    oh_scr[0, i] = norm_gate(oa, hgc_ref[...], gate_scr[0, i])
    ob = _step_head(sb_ref, sbo_ref, bt_scr, i, vt_scr[1, i])
    oh_scr[1, i] = norm_gate(ob, ggc_ref[...], gate_scr[1, i])

    @pl.when(i == A_H - 1)
    def _():
        parts = []
        for grp in range(2):
            for p in range(A_H // 2):
                two = jnp.concatenate([oh_scr[grp, 2 * p], oh_scr[grp, 2 * p + 1]], axis=0)
                parts.append(two.T)
        o = jnp.concatenate(parts + [oc_scr[...]], axis=1).astype(BF16)
        xo_ref[...] = x_ref[...] + mod_ref[:, 2 * D:3 * D] * _dot(o, wout_ref[...])


def _mixer_sample_call(l, x, mod, lp, states, prev):
    n = x.shape[0]
    consts = [lp[k] for k in ("g1", "w_in", "lbv", "w_alpha", "b_alpha", "hgrn_gc", "gla_gc", "abar_c", "bb_t",
                              "cc", "d_skip", "w_glu", "b_glu", "w_out")]
    state_specs = [pl.BlockSpec((None, None, A_DK, A_DV, n), lambda i: (l, i, 0, 0, 0)),
                   pl.BlockSpec((None, None, B_DK, B_DV, n), lambda i: (l, i, 0, 0, 0)),
                   pl.BlockSpec((None, C_S, n), lambda i: (l, 0, 0)),
                   pl.BlockSpec((None, C_S, n), lambda i: (l, 0, 0))]
    in_specs = ([pl.BlockSpec((n, D), lambda i: (0, 0), pipeline_mode=pl.Buffered(1)),
                 pl.BlockSpec((None, n, 6 * D), lambda i: (l, 0, 0), pipeline_mode=pl.Buffered(1))]
                + [_layer_spec(a, l) for a in consts] + state_specs
                + [pl.BlockSpec(memory_space=pl.ANY)] * 4)
    n_in = 2 + len(consts) + 4
    out_shape = (jax.ShapeDtypeStruct((n, D), F32),) + tuple(jax.ShapeDtypeStruct(s.shape, F32) for s in states)
    out_specs = (pl.BlockSpec((n, D), lambda i: (0, 0)),) + tuple(state_specs)
    scratch = [pltpu.VMEM((n, PW), F32), pltpu.VMEM((3, A_H, A_DK, n), F32), pltpu.VMEM((3, 8, B_DK, n), F32),
               pltpu.VMEM((2, A_H, 64, n), F32), pltpu.VMEM((2, A_H, 64, n), F32),
               pltpu.VMEM((2, A_H, 64, n), F32), pltpu.VMEM((n, C_W), F32)]
    return pl.pallas_call(
        _mixer_sample_kernel,
        grid=(A_H,),
        in_specs=in_specs,
        out_specs=out_specs,
        out_shape=out_shape,
        scratch_shapes=scratch,
        input_output_aliases={n_in + j: 1 + j for j in range(4)},
        compiler_params=pltpu.CompilerParams(
            dimension_semantics=("arbitrary",), vmem_limit_bytes=VMEM_LIMIT),
        name="mixer_sample",
    )(x, mod, *consts, *states, *prev)


def _top2_gates(logits, gidx, lane):
    neg = jnp.float32(-jnp.inf)
    big = jnp.float32(1 << 20)
    own = jnp.sum(jnp.where(lane == gidx, logits, 0.0), axis=1, keepdims=True)
    p_group = 1.0 / jnp.sum(jnp.where(lane < N_GROUPS, jnp.exp(logits - own), 0.0), axis=1, keepdims=True)
    e0 = N_GROUPS + N_PER_GROUP * gidx
    el = jnp.where((lane >= e0) & (lane < e0 + N_PER_GROUP), logits, neg)
    v1 = jnp.max(el, axis=1, keepdims=True)
    i1 = jnp.min(jnp.where(el == v1, lane, big), axis=1, keepdims=True)
    el2 = jnp.where(lane == i1, neg, el)
    v2 = jnp.max(el2, axis=1, keepdims=True)
    i2 = jnp.min(jnp.where(el2 == v2, lane, big), axis=1, keepdims=True)
    ex = jnp.exp(v2 - v1)
    w1 = 1.0 / (1.0 + ex)
    return jnp.where(lane == i1, p_group * w1, 0.0) + jnp.where(lane == i2, p_group * (ex * w1), 0.0)


def _group_experts(h, gates, lane, lane0, wg_ref, wu_ref, wd_ref):
    y = None
    for e in range(N_PER_GROUP):
        hid = _silu(_dot(h, wg_ref[e])) * _dot(h, wu_ref[e])
        gcol = jnp.sum(jnp.where(lane == lane0 + e, gates, 0.0), axis=1, keepdims=True)
        ye = _dot((hid * gcol).astype(BF16), wd_ref[e])
        y = ye if y is None else y + ye
    return y


def _moe_dense_kernel(x_ref, mod_ref, g2_ref, wr_ref, br_ref, wg_ref, wu_ref, wd_ref, fg_ref,
                      o_ref, h_scr, gate_scr, acc_scr, *, final_norm):
    g = pl.program_id(1)
    tm = x_ref.shape[0]
    lane = lax.broadcasted_iota(jnp.int32, (tm, LANES), 1).astype(F32)

    @pl.when(g == 0)
    def _():
        h = _rms_rows(x_ref[...]) * g2_ref[...]
        h = (h * (1.0 + mod_ref[:, 4 * D:5 * D]) + mod_ref[:, 3 * D:4 * D]).astype(BF16)
        h_scr[...] = h
        logits = _dot(h, wr_ref[...]) + br_ref[...]
        gl = jnp.where(lane < N_GROUPS, logits, -jnp.inf)
        gmax = jnp.max(gl, axis=1, keepdims=True)
        gidx = jnp.min(jnp.where(gl == gmax, lane, jnp.float32(1 << 20)), axis=1, keepdims=True)
        gate_scr[...] = _top2_gates(logits, gidx, lane)
        acc_scr[...] = jnp.zeros_like(acc_scr)

    lane0 = (N_GROUPS + N_PER_GROUP * g).astype(F32)
    acc_scr[...] += _group_experts(h_scr[...], gate_scr[...], lane, lane0, wg_ref, wu_ref, wd_ref)

    @pl.when(g == N_GROUPS - 1)
    def _():
        xn = x_ref[...] + mod_ref[:, 5 * D:6 * D] * acc_scr[...]
        if final_norm:
            xn = _rms_rows(xn) * fg_ref[...]
        o_ref[...] = xn


def _expert_specs(lp, l):
    def spec(arr):
        return pl.BlockSpec((None, N_PER_GROUP) + arr.shape[2:], lambda i, g, *_: (l, g, 0, 0))
    return [spec(lp["w_gate"]), spec(lp["w_up"]), spec(lp["w_down"])]


def _moe_dense_call(l, x2, mod, lp, final_g, *, final_norm):
    tm = x2.shape[0]
    kern = functools.partial(_moe_dense_kernel, final_norm=final_norm)
    return pl.pallas_call(
        kern,
        grid=(1, N_GROUPS),
        in_specs=[pl.BlockSpec((tm, D), lambda i, g: (0, 0)),
                  pl.BlockSpec((None, tm, 6 * D), lambda i, g: (l, 0, 0)),
                  _layer_spec(lp["g2"], l), _layer_spec(lp["w_router"], l), _layer_spec(lp["b_router"], l)]
        + _expert_specs(lp, l) + [pl.BlockSpec((1, D), lambda i, g: (0, 0))],
        out_specs=pl.BlockSpec((tm, D), lambda i, g: (0, 0)),
        out_shape=jax.ShapeDtypeStruct((tm, D), F32),
        scratch_shapes=[pltpu.VMEM((tm, D), BF16), pltpu.VMEM((tm, LANES), F32), pltpu.VMEM((tm, D), F32)],
        compiler_params=pltpu.CompilerParams(
            dimension_semantics=("arbitrary", "arbitrary"), vmem_limit_bytes=VMEM_LIMIT),
        name="moe_sample",
    )(x2, mod, lp["g2"], lp["w_router"], lp["b_router"], lp["w_gate"], lp["w_up"], lp["w_down"], final_g)


def _route_kernel(x_ref, mod_ref, g2_ref, wrt_ref, brc_ref, tri_ref, pos_ref, offs_ref, *, tiles_per_seq):
    tt = x_ref.shape[0]
    b = pl.program_id(0) // tiles_per_seq
    h = _rms_rows(x_ref[...]) * g2_ref[...]
    h = (h * (1.0 + mod_ref[pl.ds(b, 1), 4 * D:5 * D]) + mod_ref[pl.ds(b, 1), 3 * D:4 * D]).astype(BF16)
    lt = _dot_nt(wrt_ref[...], h) + brc_ref[...]
    row = lax.broadcasted_iota(jnp.int32, (SUBLANES, tt), 0).astype(F32)
    gl = jnp.where(row < N_GROUPS, lt[0:SUBLANES, :], -jnp.inf)
    gmax = jnp.max(gl, axis=0, keepdims=True)
    gidx = jnp.min(jnp.where(gl == gmax, row, jnp.float32(1 << 20)), axis=0, keepdims=True)
    onehot = jnp.where(row == gidx, 1.0, 0.0)
    rank = _dot(onehot.astype(BF16), tri_ref[...])
    counts = jnp.sum(onehot, axis=1, keepdims=True)
    row1 = lax.broadcasted_iota(jnp.int32, (SUBLANES, 1), 0)
    off = jnp.zeros((SUBLANES, 1), F32)
    for g in range(N_GROUPS - 1):
        off = off + jnp.where(row1 > g, counts[g:g + 1, :], 0.0)
    pos_ref[0] = jnp.sum(onehot * (rank + off), axis=0, keepdims=True).astype(jnp.int32)
    lane = lax.broadcasted_iota(jnp.int32, (1, LANES), 1)
    offs = jnp.where(lane == N_GROUPS, float(tt), 0.0)
    for g in range(1, N_GROUPS):
        offs = offs + jnp.where(lane == g, off[g:g + 1, :], 0.0)
    offs_ref[0] = offs.astype(jnp.int32)


def _route_call(l, x2, mod, ns, nb, lp, *, tt):
    t = x2.shape[0]
    nt = t // tt
    tiles_per_seq = nt // nb
    tri = jnp.asarray(np.triu(np.ones((tt, tt), np.float32), 1), BF16)
    kern = functools.partial(_route_kernel, tiles_per_seq=tiles_per_seq)
    return pl.pallas_call(
        kern,
        grid=(nt,),
        in_specs=[pl.BlockSpec((tt, D), lambda i: (i, 0)),
                  pl.BlockSpec((None, nb, 6 * D), lambda i: (l, ns // nb, 0), pipeline_mode=pl.Buffered(1)),
                  _layer_spec(lp["g2"], l), _layer_spec(lp["w_router_t"], l), _layer_spec(lp["b_router_c"], l),
                  pl.BlockSpec((tt, tt), lambda i: (0, 0), pipeline_mode=pl.Buffered(1))],
        out_specs=(pl.BlockSpec((1, 1, tt), lambda i: (i, 0, 0)),
                   pl.BlockSpec((1, 1, LANES), lambda i: (i, 0, 0))),
        out_shape=(jax.ShapeDtypeStruct((nt, 1, tt), jnp.int32),
                   jax.ShapeDtypeStruct((nt, 1, LANES), jnp.int32)),
        compiler_params=pltpu.CompilerParams(
            dimension_semantics=("arbitrary",), vmem_limit_bytes=VMEM_LIMIT),
        name="moe_route",
    )(x2, mod, lp["g2"], lp["w_router_t"], lp["b_router_c"], tri)


def _moe_sorted_kernel(pos_sm, offs_sm, x_ref, mod_ref, g2_ref, wr_ref, br_ref, wg_ref, wu_ref, wd_ref, fg_ref,
                       o_ref, a_scr, b_scr, hs_scr, gate_scr, acc_scr, *, tiles_per_seq, final_norm):
    i = pl.program_id(0)
    g = pl.program_id(1)
    tt = x_ref.shape[0]
    base = i * tt
    b = i // tiles_per_seq

    def mod(k):
        return mod_ref[pl.ds(b, 1), k * D:(k + 1) * D]

    def tile_rows(t):
        return pl.ds(pl.multiple_of(t * SUBLANES, SUBLANES), SUBLANES)

    def to_token_tiles(v):
        for s in range(SUBLANES):
            a_scr[s:tt * SUBLANES:SUBLANES, :] = v[:, s * LANES:(s + 1) * LANES]

    def from_token_tiles():
        return jnp.concatenate([b_scr[s:tt * SUBLANES:SUBLANES, :] for s in range(SUBLANES)], axis=1)

    lane = lax.broadcasted_iota(jnp.int32, (tt, LANES), 1).astype(F32)

    @pl.when(g == 0)
    def _():
        h = _rms_rows(x_ref[...]) * g2_ref[...]
        to_token_tiles(h * (1.0 + mod(4)) + mod(3))

        def scatter(t, carry):
            b_scr[tile_rows(pos_sm[base + t]), :] = a_scr[tile_rows(t), :]
            return carry
        lax.fori_loop(0, tt, scatter, 0, unroll=8)
        hs = from_token_tiles().astype(BF16)
        hs_scr[0:tt, :] = hs
        hs_scr[tt:tt + MOE_BLOCK, :] = jnp.zeros((MOE_BLOCK, D), BF16)
        logits = _dot(hs, wr_ref[...]) + br_ref[...]
        slot = lax.broadcasted_iota(jnp.int32, (tt, 1), 0)
        gidx = jnp.zeros((tt, 1), F32)
        for gg in range(1, N_GROUPS):
            gidx = gidx + jnp.where(slot >= offs_sm[i * LANES + gg], 1.0, 0.0)
        gate_scr[0:tt, :] = _top2_gates(logits, gidx, lane)
        gate_scr[tt:tt + MOE_BLOCK, :] = jnp.zeros((MOE_BLOCK, LANES), F32)
        acc_scr[...] = jnp.zeros_like(acc_scr)

    lo = offs_sm[i * LANES + g]
    hi = offs_sm[i * LANES + g + 1]
    start = (lo // BF16_ROWS) * BF16_ROWS
    nblk = jnp.where(hi > lo, (hi - start + MOE_BLOCK - 1) // MOE_BLOCK, 0)
    lane0 = (N_GROUPS + N_PER_GROUP * g).astype(F32)
    blk_lane = lax.broadcasted_iota(jnp.int32, (MOE_BLOCK, LANES), 1).astype(F32)

    def block(k, carry):
        rs = pl.ds(pl.multiple_of(start + k * MOE_BLOCK, BF16_ROWS), MOE_BLOCK)
        acc_scr[rs, :] += _group_experts(hs_scr[rs, :], gate_scr[rs, :], blk_lane, lane0, wg_ref, wu_ref, wd_ref)
        return carry
    lax.fori_loop(0, nblk, block, 0)

    @pl.when(g == N_GROUPS - 1)
    def _():
        to_token_tiles(acc_scr[0:tt, :])

        def gather(t, carry):
            b_scr[tile_rows(t), :] = a_scr[tile_rows(pos_sm[base + t]), :]
            return carry
        lax.fori_loop(0, tt, gather, 0, unroll=8)
        xn = x_ref[...] + mod(5) * from_token_tiles()
        if final_norm:
            xn = _rms_rows(xn) * fg_ref[...]
        o_ref[...] = xn


def _moe_sorted_call(l, x2, mod, ns, nb, lp, final_g, pos, offs, *, tt, final_norm):
    t = x2.shape[0]
    nt = t // tt
    kern = functools.partial(_moe_sorted_kernel, tiles_per_seq=nt // nb, final_norm=final_norm)
    grid_spec = pltpu.PrefetchScalarGridSpec(
        num_scalar_prefetch=2,
        grid=(nt, N_GROUPS),
        in_specs=[pl.BlockSpec((tt, D), lambda i, g, *_: (i, 0)),
                  pl.BlockSpec((None, nb, 6 * D), lambda i, g, *_: (l, ns // nb, 0), pipeline_mode=pl.Buffered(1)),
                  _layer_spec(lp["g2"], l), _layer_spec(lp["w_router"], l), _layer_spec(lp["b_router"], l)]
        + _expert_specs(lp, l) + [pl.BlockSpec((1, D), lambda i, g, *_: (0, 0))],
        out_specs=pl.BlockSpec((tt, D), lambda i, g, *_: (i, 0)),
        scratch_shapes=[pltpu.VMEM((tt * SUBLANES, LANES), F32), pltpu.VMEM((tt * SUBLANES, LANES), F32),
                        pltpu.VMEM((tt + MOE_BLOCK, D), BF16), pltpu.VMEM((tt + MOE_BLOCK, LANES), F32),
                        pltpu.VMEM((tt + MOE_BLOCK, D), F32)])
    return pl.pallas_call(
        kern,
        grid_spec=grid_spec,
        out_shape=jax.ShapeDtypeStruct((t, D), F32),
        compiler_params=pltpu.CompilerParams(
            dimension_semantics=("arbitrary", "arbitrary"), vmem_limit_bytes=VMEM_LIMIT),
        name="moe_prompt",
    )(pos, offs, x2, mod, lp["g2"], lp["w_router"], lp["b_router"], lp["w_gate"], lp["w_up"], lp["w_down"],
      final_g)


def _pad_to(a, axis, size):
    pad = [(0, 0)] * a.ndim
    pad[axis] = (0, size - a.shape[axis])
    return jnp.pad(a, pad)


def _prepare_params(p):
    lb_cum = jnp.cumsum(jax.nn.softmax(p["hgrn_lb_logits"].astype(F32), axis=0), axis=0)
    lb = lb_cum - lb_cum[:1]
    lbv = _pad_to(jnp.stack([jnp.log(lb), jnp.log1p(-lb), 1.0 - lb], axis=1), 1, SUBLANES)
    a_re, a_im = p["s5_a_re"].astype(F32), p["s5_a_im"].astype(F32)
    dt = jnp.exp(p["s5_log_dt"].astype(F32))[..., None]
    mag = jnp.exp(a_re * dt)
    abar_re = mag * jnp.cos(a_im * dt)
    abar_im = mag * jnp.sin(a_im * dt)
    den = a_re * a_re + a_im * a_im
    nr, ni = abar_re - 1.0, abar_im
    coef_re = ((nr * a_re + ni * a_im) / den)[..., None]
    coef_im = ((ni * a_re - nr * a_im) / den)[..., None]
    b_re, b_im = p["s5_b_re"].astype(F32), p["s5_b_im"].astype(F32)
    bb_re = coef_re * b_re - coef_im * b_im
    bb_im = coef_re * b_im + coef_im * b_re
    eye = jnp.eye(C_G, dtype=F32)
    blk_b = lambda m: jnp.einsum("lgph,gk->lghkp", m, eye).reshape(DEPTH, C_W, C_S)
    blk_bt = lambda m: jnp.einsum("lgph,gk->lgpkh", m, eye).reshape(DEPTH, C_S, C_W)
    blk_c = lambda m: jnp.einsum("lghp,gk->lgpkh", m, eye).reshape(DEPTH, C_S, C_W)
    bb = jnp.concatenate([blk_b(bb_re), blk_b(bb_im)], axis=2).astype(BF16)
    bb_t = jnp.concatenate([blk_bt(bb_re), blk_bt(bb_im)], axis=1).astype(BF16)
    cc = jnp.concatenate([blk_c(p["s5_c_re"].astype(F32)), -blk_c(p["s5_c_im"].astype(F32))],
                         axis=1).astype(BF16)
    abar_rows = jnp.stack([abar_re.reshape(DEPTH, C_S), abar_im.reshape(DEPTH, C_S)], axis=1)
    router = jnp.concatenate([p["moe_w_group"], p["moe_w_expert"]], axis=2)
    b_router = _pad_to(jnp.concatenate([p["moe_b_group"], p["moe_b_expert"]], axis=1), 1, LANES)
    row = lambda a: a.astype(F32)[:, None, :]
    col = lambda a: a.astype(F32)[:, :, None]
    return dict(
        g1=row(p["norm1_g"]), g2=row(p["norm2_g"]),
        w_in=jnp.transpose(p["w_in"], (0, 2, 1)).astype(BF16), lbv=lbv,
        w_alpha=_pad_to(_pad_to(p["gla_w_alpha"], 1, LANES), 2, 256).astype(BF16),
        b_alpha=_pad_to(row(p["gla_b_alpha"]), 2, 256),
        hgrn_g=row(jnp.tile(p["hgrn_norm_g"], (1, A_H))), gla_g=row(jnp.tile(p["gla_norm_g"], (1, B_H))),
        hgrn_gc=col(p["hgrn_norm_g"]), gla_gc=col(p["gla_norm_g"]),
        abar=_pad_to(abar_rows, 1, SUBLANES), abar_c=jnp.transpose(abar_rows, (0, 2, 1)),
        bb=bb, bb_t=bb_t, cc=cc, d_skip=row(p["s5_d"]),
        w_glu=p["s5_w_glu"].astype(BF16), b_glu=row(p["s5_b_glu"]), w_out=p["w_out"].astype(BF16),
        w_gate=p["moe_w_gate"].astype(BF16), w_up=p["moe_w_up"].astype(BF16),
        w_down=p["moe_w_down"].astype(BF16),
        w_router=_pad_to(router, 2, LANES).astype(BF16), b_router=b_router[:, None, :],
        w_router_t=_pad_to(jnp.transpose(router, (0, 2, 1)), 1, LANES).astype(BF16),
        b_router_c=b_router[:, :, None])


def kernel(x_prompt, x_sample, state_hgrn, state_gla, state_s5_re, state_s5_im, c_prompt, c_sample, ada_w, ada_b, norm1_g, norm2_g, w_in, hgrn_lb_logits, hgrn_norm_g, gla_w_alpha, gla_b_alpha, gla_norm_g, s5_a_re, s5_a_im, s5_b_re, s5_b_im, s5_c_re, s5_c_im, s5_d, s5_log_dt, s5_w_glu, s5_b_glu, w_out, moe_w_group, moe_b_group, moe_w_expert, moe_b_expert, moe_w_gate, moe_w_up, moe_w_down, final_norm_g):
    lp = _prepare_params(dict(
        norm1_g=norm1_g, norm2_g=norm2_g, w_in=w_in, hgrn_lb_logits=hgrn_lb_logits, hgrn_norm_g=hgrn_norm_g,
        gla_w_alpha=gla_w_alpha, gla_b_alpha=gla_b_alpha, gla_norm_g=gla_norm_g, s5_a_re=s5_a_re,
        s5_a_im=s5_a_im, s5_b_re=s5_b_re, s5_b_im=s5_b_im, s5_c_re=s5_c_re, s5_c_im=s5_c_im, s5_d=s5_d,
        s5_log_dt=s5_log_dt, s5_w_glu=s5_w_glu, s5_b_glu=s5_b_glu, w_out=w_out, moe_w_group=moe_w_group,
        moe_b_group=moe_b_group, moe_w_expert=moe_w_expert, moe_b_expert=moe_b_expert,
        moe_w_gate=moe_w_gate, moe_w_up=moe_w_up, moe_w_down=moe_w_down))
    nb, seq, _ = x_prompt.shape
    ns = x_sample.shape[0]
    assert ns % nb == 0 and seq % CHUNK == 0
    tt = min(MOE_TILE, seq)
    assert seq % tt == 0 and tt % BF16_ROWS == 0
    final_g = final_norm_g.reshape(1, D)

    mod = _ada_call(jnp.concatenate([c_sample, c_prompt], axis=0), ada_w, ada_b)
    states = (jnp.transpose(state_hgrn, (0, 2, 3, 4, 1)), jnp.transpose(state_gla, (0, 2, 3, 4, 1)),
              jnp.transpose(state_s5_re, (0, 2, 3, 1)).reshape(DEPTH, C_S, ns),
              jnp.transpose(state_s5_im, (0, 2, 3, 1)).reshape(DEPTH, C_S, ns))
    new_states = tuple(jnp.zeros_like(s) for s in states)

    xp = x_prompt
    xs = x_sample.reshape(ns, D)
    outs = {k: [] for k in ("hg_p", "gl_p", "sr_p", "si_p")}
    for l in range(DEPTH):
        last = l == DEPTH - 1
        xp, sta, stb, s5r, s5i = _mixer_prompt_call(l, xp, mod, ns, lp)
        x2 = xp.reshape(nb * seq, D)
        pos, offs = _route_call(l, x2, mod, ns, nb, lp, tt=tt)
        xp = _moe_sorted_call(l, x2, mod, ns, nb, lp, final_g, pos.reshape(-1), offs.reshape(-1),
                              tt=tt, final_norm=last).reshape(nb, seq, D)
        sta = sta.reshape(nb, A_H // 2, 2, A_DV, 2, A_DK)
        outs["hg_p"].append(jnp.stack([sta[:, h // 2, h % 2, :, h % 2, :] for h in range(A_H)], axis=1)
                            .transpose(0, 1, 3, 2))
        stb = stb.reshape(nb, B_H, B_DV, 4, B_DK)
        outs["gl_p"].append(jnp.stack([stb[:, h, :, h % 4, :] for h in range(B_H)], axis=1)
                            .transpose(0, 1, 3, 2))
        outs["sr_p"].append(s5r.reshape(nb, C_G, C_P))
        outs["si_p"].append(s5i.reshape(nb, C_G, C_P))
        xs, *new_states = _mixer_sample_call(l, xs, mod, lp, states, new_states)
        xs = _moe_dense_call(l, xs, mod, lp, final_g, final_norm=last)

    hg_s, gl_s, sr_s, si_s = new_states
    return (xp, xs.reshape(ns, 1, D),
            jnp.stack(outs["hg_p"]), jnp.stack(outs["gl_p"]), jnp.stack(outs["sr_p"]), jnp.stack(outs["si_p"]),
            jnp.transpose(hg_s, (0, 4, 1, 2, 3)), jnp.transpose(gl_s, (0, 4, 1, 2, 3)),
            jnp.transpose(sr_s.reshape(DEPTH, C_G, C_P, ns), (0, 3, 1, 2)),
            jnp.transpose(si_s.reshape(DEPTH, C_G, C_P, ns), (0, 3, 1, 2)))
```

```python
import functools
import math

import numpy as np
import jax
import jax.numpy as jnp
from jax import lax
from jax.experimental import pallas as pl
from jax.experimental.pallas import tpu as pltpu

F32 = jnp.float32
BF16 = jnp.bfloat16

D = 1024
DEPTH = 4
A_H, A_DK, A_DV = 6, 64, 64
B_H, B_DK, B_DV = 6, 32, 64
B_RANK = 16
B_TAU = 16.0
C_G, C_CH, C_P = 16, 16, 64
C_W = C_G * C_CH
C_S = C_G * C_P
N_GROUPS, N_PER_GROUP, D_EXP = 4, 4, 256
N_EXP = N_GROUPS * N_PER_GROUP
EPS = 1e-6
CHUNK = 64
LANES = 128
SUBLANES = 8
EXP_RANGE = 80.0
MOE_TILE = 1024
MOE_BLOCK = 320
BF16_ROWS = 16

W_AQ, W_AF, W_AI, W_AG = 0, 384, 768, 1152
W_BQ, W_BK, W_BV, W_LR, W_BR, W_CU = 1536, 1728, 1920, 2304, 2320, 2704
IN_COLS = 2960
W_SEGMENTS = {"aq|af": (W_AQ, 768), "ai|ag": (W_AI, 768), "bq": (W_BQ, 256), "bk": (W_BK, 256),
              "bv|lr": (W_BV, 512), "br|cu": (W_BR, 640)}
QA, FA, VA, GA = 0, 384, 768, 1152
QB, KB, VB, RB = 1536, 1792, 2048, 2432
CU, LB, KA = 2816, 3072, 3328
PW = 3712
N_LG = 5
LG_W = N_LG * LANES
BLOCKS = tuple((QA + LANES * p, KA + LANES * p, FA + LANES * p, LANES * p, 2, A_DK) for p in range(3)) + (
    (QB, KB, LB, 384, 4, B_DK), (QB + LANES, KB + LANES, LB + LANES, 640, 2, B_DK))
VO_W = 768

VMEM_LIMIT = 56 * 1024 * 1024


def _sigmoid(x):
    return 1.0 / (1.0 + jnp.exp(-x))


def _silu(x):
    return x * _sigmoid(x)


def _log_sigmoid(x):
    return jnp.minimum(x, 0.0) - jnp.log(1.0 + jnp.exp(-jnp.abs(x)))


def _gelu_tanh(x):
    return 0.5 * x * (1.0 + jnp.tanh(math.sqrt(2.0 / math.pi) * (x + 0.044715 * x * x * x)))


def _split3(x):
    hi = x.astype(BF16)
    r = x - hi.astype(F32)
    mid = r.astype(BF16)
    lo = (r - mid.astype(F32)).astype(BF16)
    return hi, mid, lo


def _dot(a, b):
    return jnp.dot(a, b, preferred_element_type=F32)


def _dot_nt(a, b):
    return lax.dot_general(a, b, (((1,), (1,)), ((), ())), preferred_element_type=F32)


def _dot_tn(a, b):
    return lax.dot_general(a, b, (((0,), (0,)), ((), ())), preferred_element_type=F32)


def _dot_exact_01(x, m01):
    hi, mid, lo = _split3(x)
    return _dot(hi, m01) + _dot(mid, m01) + _dot(lo, m01)


def _rms_rows(x):
    return x * lax.rsqrt(jnp.mean(x * x, axis=-1, keepdims=True) + EPS)


def _forget_gate(af, lbv):
    e = jnp.exp(-jnp.abs(af))
    log_sig = jnp.minimum(af, 0.0) - jnp.log(1.0 + e)
    a = lbv[0:1]
    b = lbv[1:2] + log_sig
    log_f = jnp.maximum(a, b) + jnp.log(1.0 + jnp.exp(-jnp.abs(a - b)))
    k = lbv[2:3] * (jnp.where(af >= 0.0, e, 1.0) / (1.0 + e))
    return log_f, k


def _chunk_cumsum(x, c):
    pos = lax.broadcasted_iota(jnp.int32, x.shape, 0) % c
    s = 1
    while s < c:
        x = x + jnp.where(pos >= s, pltpu.roll(x, s, 0), 0.0)
        s *= 2
    return x


def _head_rms(o, dv):
    w = o.shape[1]
    r = lax.broadcasted_iota(jnp.int32, (w, w), 0)
    cidx = lax.broadcasted_iota(jnp.int32, (w, w), 1)
    ones_blk = jnp.where(r // dv == cidx // dv, 1.0, 0.0).astype(BF16)
    return o * lax.rsqrt(_dot((o * o).astype(BF16), ones_blk) * (1.0 / dv) + EPS)


def _project_all(h, wint_ref, lbv_ref, walpha_ref, balpha_ref, chunk, p_scr):
    def proj(name):
        r0, n = W_SEGMENTS[name]
        return _dot_nt(h, wint_ref[r0:r0 + n, :])

    def cum(x):
        return x if chunk == 1 else _chunk_cumsum(x, chunk)

    aqf = proj("aq|af")
    p_scr[:, QA:QA + 384] = _silu(aqf[:, 0:384])
    log_f, ka = _forget_gate(aqf[:, 384:768], lbv_ref[...])
    p_scr[:, FA:FA + 384] = cum(log_f)
    p_scr[:, KA:KA + 384] = ka
    aig = proj("ai|ag")
    p_scr[:, VA:VA + 384] = aig[:, 0:384]
    p_scr[:, GA:GA + 384] = _silu(aig[:, 384:768])
    p_scr[:, QB:QB + 256] = proj("bq") * (B_DK ** -0.5)
    kb = proj("bk")
    p_scr[:, KB:KB + 256] = jnp.where(lax.broadcasted_iota(jnp.int32, kb.shape, 1) < B_H * B_DK, kb, 0.0)
    bvlr = proj("bv|lr")
    p_scr[:, VB:VB + 384] = bvlr[:, 0:384]
    z = _dot(bvlr[:, 384:512].astype(BF16), walpha_ref[...]) + balpha_ref[...]
    p_scr[:, LB:LB + 256] = cum(_log_sigmoid(z) * (1.0 / B_TAU))
    brcu = proj("br|cu")
    p_scr[:, RB:RB + 384] = _silu(brcu[:, 0:384])
    p_scr[:, CU:CU + 256] = brcu[:, 384:640]


def _layer_spec(arr, l):
    nd = arr.ndim - 1
    return pl.BlockSpec((None,) + arr.shape[1:], lambda *_, _n=nd: (l,) + (0,) * _n,
                        pipeline_mode=pl.Buffered(1))


def _ada_kernel(c_ref, w_ref, b_ref, o_ref):
    sc = _silu(c_ref[...]).astype(BF16)
    o_ref[0] = _dot(sc, w_ref[0].astype(BF16)) + b_ref[0]


def _ada_call(c_all, ada_w, ada_b):
    n = c_all.shape[0]
    tn = 1536
    return pl.pallas_call(
        _ada_kernel,
        grid=(DEPTH, 6 * D // tn),
        in_specs=[
            pl.BlockSpec((n, D), lambda l, j: (0, 0)),
            pl.BlockSpec((1, D, tn), lambda l, j: (l, 0, j)),
            pl.BlockSpec((1, 1, tn), lambda l, j: (l, 0, j)),
        ],
        out_specs=pl.BlockSpec((1, n, tn), lambda l, j: (l, 0, j)),
        out_shape=jax.ShapeDtypeStruct((DEPTH, n, 6 * D), F32),
        compiler_params=pltpu.CompilerParams(
            dimension_semantics=("arbitrary", "arbitrary"), vmem_limit_bytes=VMEM_LIMIT),
        name="ada_mod",
    )(c_all, ada_w, ada_b.reshape(DEPTH, 1, 6 * D))


def _mixer_prompt_kernel(x_ref, mod_ref, g1_ref, win_ref, lbv_ref, walpha_ref, balpha_ref,
                         hg_ref, gg_ref, abar_ref, bb_ref, cc_ref, dsk_ref, wglu_ref, bglu_ref,
                         wout_ref,
                         xo_ref, sta_ref, stb_ref, s5r_ref, s5i_ref,
                         h_scr, p_scr, o_scr, xr_scr, xi_scr, t_scr,
                         qt_scr, kt_scr, qd_scr, kd_scr, v_scr, of_scr, dec_scr, sc_scr):
    nb = x_ref.shape[0]
    c = CHUNK
    i = pl.program_id(0)

    @pl.when(i == 0)
    def _():
        sta_ref[...] = jnp.zeros_like(sta_ref)
        stb_ref[...] = jnp.zeros_like(stb_ref)
        s5r_ref[...] = jnp.zeros_like(s5r_ref)
        s5i_ref[...] = jnp.zeros_like(s5i_ref)

    for b in range(nb):
        hb = _rms_rows(x_ref[b]) * g1_ref[...]
        hb = hb * (1.0 + mod_ref[b:b + 1, D:2 * D]) + mod_ref[b:b + 1, 0:D]
        h_scr[b * c:(b + 1) * c, :] = hb.astype(BF16)

    _project_all(h_scr[...], win_ref, lbv_ref, walpha_ref, balpha_ref, c, p_scr)

    v_scr[:, 0:384] = p_scr[:, VA:VA + 384].astype(BF16)
    v_scr[:, 384:768] = p_scr[:, VB:VB + 384].astype(BF16)
    rng = jnp.zeros((1, LANES), F32)
    for b in range(nb):
        rs = slice(b * c, (b + 1) * c)
        for g, (qc, kc, bc, _, _, _) in enumerate(BLOCKS):
            ls = slice(g * LANES, (g + 1) * LANES)
            q = p_scr[rs, qc:qc + LANES]
            k = p_scr[rs, kc:kc + LANES]
            bcum = p_scr[rs, bc:bc + LANES]
            b_mid = bcum[c // 2 - 1:c // 2, :]
            b_end = bcum[c - 1:c, :]
            d = bcum - b_mid
            rng = jnp.maximum(rng, jnp.maximum(jnp.abs(d[0:1, :]), jnp.abs(d[c - 1:c, :])))
            qt_scr[rs, ls] = (q * jnp.exp(d)).astype(BF16)
            kt_scr[rs, ls] = (k * jnp.exp(-d)).astype(BF16)
            qd_scr[rs, ls] = (q * jnp.exp(bcum)).astype(BF16)
            kd_scr[rs, ls] = (k * jnp.exp(b_end - bcum)).astype(BF16)
            dec_scr[b, :, ls] = jnp.broadcast_to(jnp.exp(b_end), (SUBLANES, LANES))
    halves = C_W // LANES
    for b in range(nb):
        for j in range(halves):
            t_scr[j, b:b + nb * c:nb, :] = p_scr[b * c:(b + 1) * c, CU + LANES * j:CU + LANES * (j + 1)]
    xri = _dot(jnp.concatenate([t_scr[j] for j in range(halves)], axis=1).astype(BF16), bb_ref[...])
    xr_scr[...] = xri[:, 0:C_S]
    xi_scr[...] = xri[:, C_S:2 * C_S]

    in_range = jnp.max(rng) <= EXP_RANGE

    lane = lax.broadcasted_iota(jnp.int32, (c, LANES), 1)

    def scores_direct(r0, qc, kc, bc, nh, dk):
        q = p_scr[pl.ds(r0, c), qc:qc + LANES]
        bcum = p_scr[pl.ds(r0, c), bc:bc + LANES]
        krow = lax.broadcasted_iota(jnp.int32, (LANES, nh * c), 0)
        kcol = lax.broadcasted_iota(jnp.int32, (LANES, nh * c), 1)
        sub = lax.broadcasted_iota(jnp.int32, (SUBLANES, LANES), 0)

        def row_of(col0, s):
            r8 = pl.ds(pl.multiple_of(r0 + (s // SUBLANES) * SUBLANES, SUBLANES), SUBLANES)
            return jnp.sum(jnp.where(sub == s % SUBLANES, p_scr[r8, col0:col0 + LANES], 0.0),
                           axis=0, keepdims=True)

        def body(s, sc):
            ks = row_of(kc, s)
            bs = row_of(bc, s)
            w = q * ks * jnp.exp(jnp.minimum(bcum - bs, 0.0))
            place = jnp.where(kcol == (krow // dk) * c + s, 1.0, 0.0).astype(BF16)
            return sc + _dot_exact_01(w, place)

        return lax.fori_loop(0, c, body, jnp.zeros((c, nh * c), F32))

    def stage_scores(b, r0, direct):
        rs = pl.ds(r0, c)
        for g, (qc, kc, bc, vc, nh, dk) in enumerate(BLOCKS):
            ls = slice(g * LANES, (g + 1) * LANES)
            if direct:
                sc = scores_direct(r0, qc, kc, bc, nh, dk)
            else:
                kt = kt_scr[rs, ls]
                kcat = jnp.concatenate(
                    [jnp.where(lane // dk == h, kt, jnp.zeros_like(kt)) for h in range(nh)], axis=0)
                sc = _dot_nt(qt_scr[rs, ls], kcat)
            row = lax.broadcasted_iota(jnp.int32, (c, nh * c), 0)
            col = lax.broadcasted_iota(jnp.int32, (c, nh * c), 1)
            sc_scr[rs, vc:vc + nh * c] = jnp.where(col % c <= row, sc, 0.0).astype(BF16)

    def stage_readout(b, r0):
        rs = pl.ds(r0, c)
        for g, (_, _, _, vc, nh, dk) in enumerate(BLOCKS):
            ls = slice(g * LANES, (g + 1) * LANES)
            vw = nh * 64
            st_ref = sta_ref if g < 3 else stb_ref
            v0 = vc if g < 3 else vc - 384
            v = v_scr[rs, vc:vc + vw]
            vlane = lax.broadcasted_iota(jnp.int32, (c, vw), 1)
            vcat = jnp.concatenate(
                [jnp.where(vlane // 64 == h, v, jnp.zeros_like(v)) for h in range(nh)], axis=0)
            st = st_ref[b, v0:v0 + vw, :]
            of_scr[rs, vc:vc + vw] = (_dot(sc_scr[rs, vc:vc + nh * c], vcat)
                                      + _dot_nt(qd_scr[rs, ls], st.astype(BF16)))

    def stage_update(b, r0):
        rs = pl.ds(r0, c)
        for g, (_, _, _, vc, nh, dk) in enumerate(BLOCKS):
            ls = slice(g * LANES, (g + 1) * LANES)
            vw = nh * 64
            st_ref = sta_ref if g < 3 else stb_ref
            v0 = vc if g < 3 else vc - 384
            upd = _dot_tn(v_scr[rs, vc:vc + vw], kd_scr[rs, ls])
            srow = lax.broadcasted_iota(jnp.int32, (vw, LANES), 0)
            scol = lax.broadcasted_iota(jnp.int32, (vw, LANES), 1)
            st_ref[b, v0:v0 + vw, :] = (st_ref[b, v0:v0 + vw, :] * dec_scr[b, 0:1, ls]
                                        + jnp.where(srow // 64 == scol // dk, upd, 0.0))

    @pl.when(in_range)
    def _():
        for b in range(nb):
            stage_scores(b, b * c, False)
        for b in range(nb):
            stage_readout(b, b * c)
        for b in range(nb):
            stage_update(b, b * c)

    @pl.when(jnp.logical_not(in_range))
    def _():
        def body(b, carry):
            r0 = pl.multiple_of(b * c, c)
            stage_scores(b, r0, True)
            stage_readout(b, r0)
            stage_update(b, r0)
            return carry
        lax.fori_loop(0, nb, body, 0)

    oa = _head_rms(of_scr[:, 0:384], A_DV) * hg_ref[...] * p_scr[:, GA:GA + 384]
    o_scr[:, 0:384] = oa.astype(BF16)
    ob = _head_rms(of_scr[:, 384:768], B_DV) * gg_ref[...] * p_scr[:, RB:RB + 384]
    o_scr[:, 384:768] = ob.astype(BF16)

    ar = abar_ref[0:1, :]
    ai = abar_ref[1:2, :]

    def scan_t(t, carry):
        sr, si = carry
        ts = pl.ds(pl.multiple_of(t * nb, nb), nb)
        nr = ar * sr - ai * si + xr_scr[ts, :]
        ni = ar * si + ai * sr + xi_scr[ts, :]
        xr_scr[ts, :] = nr
        xi_scr[ts, :] = ni
        return nr, ni

    sr, si = lax.fori_loop(0, c, scan_t, (s5r_ref[...], s5i_ref[...]), unroll=2)
    s5r_ref[...] = sr
    s5i_ref[...] = si
    u = jnp.concatenate([t_scr[j] for j in range(halves)], axis=1)
    y = (_dot(xr_scr[...].astype(BF16), cc_ref[0:C_S, :])
         + _dot(xi_scr[...].astype(BF16), cc_ref[C_S:2 * C_S, :]) + dsk_ref[...] * u)
    z = _gelu_tanh(y)
    oc = z * _sigmoid(_dot(z.astype(BF16), wglu_ref[...]) + bglu_ref[...])
    for j in range(halves):
        t_scr[j] = oc[:, LANES * j:LANES * (j + 1)]
    for b in range(nb):
        for j in range(halves):
            o_scr[b * c:(b + 1) * c, 768 + LANES * j:768 + LANES * (j + 1)] = (
                t_scr[j, b:b + nb * c:nb, :].astype(BF16))

    out = _dot(o_scr[...], wout_ref[...])
    for b in range(nb):
        xo_ref[b] = x_ref[b] + mod_ref[b:b + 1, 2 * D:3 * D] * out[b * c:(b + 1) * c, :]


def _mixer_prompt_call(l, x, mod, ns, lp):
    nb, seq, _ = x.shape
    c = CHUNK
    rows = nb * c
    consts = [lp[k] for k in ("g1", "w_in", "lbv", "w_alpha", "b_alpha", "hgrn_g", "gla_g", "abar", "bb", "cc",
                              "d_skip", "w_glu", "b_glu", "w_out")]
    in_specs = ([pl.BlockSpec((nb, c, D), lambda i: (0, i, 0)),
                 pl.BlockSpec((None, nb, 6 * D), lambda i: (l, ns // nb, 0), pipeline_mode=pl.Buffered(1))]
                + [_layer_spec(a, l) for a in consts])
    out_shape = (jax.ShapeDtypeStruct((nb, seq, D), F32),
                 jax.ShapeDtypeStruct((nb, 384, LANES), F32),
                 jax.ShapeDtypeStruct((nb, 384, LANES), F32),
                 jax.ShapeDtypeStruct((nb, C_S), F32),
                 jax.ShapeDtypeStruct((nb, C_S), F32))
    out_specs = (pl.BlockSpec((nb, c, D), lambda i: (0, i, 0)),
                 pl.BlockSpec((nb, 384, LANES), lambda i: (0, 0, 0)),
                 pl.BlockSpec((nb, 384, LANES), lambda i: (0, 0, 0)),
                 pl.BlockSpec((nb, C_S), lambda i: (0, 0)),
                 pl.BlockSpec((nb, C_S), lambda i: (0, 0)))
    scratch = [pltpu.VMEM((rows, D), BF16), pltpu.VMEM((rows, PW), F32), pltpu.VMEM((rows, D), BF16),
               pltpu.VMEM((rows, C_S), F32), pltpu.VMEM((rows, C_S), F32),
               pltpu.VMEM((C_W // LANES, rows, LANES), F32),
               pltpu.VMEM((rows, LG_W), BF16), pltpu.VMEM((rows, LG_W), BF16),
               pltpu.VMEM((rows, LG_W), BF16), pltpu.VMEM((rows, LG_W), BF16),
               pltpu.VMEM((rows, VO_W), BF16), pltpu.VMEM((rows, VO_W), F32),
               pltpu.VMEM((nb, SUBLANES, LG_W), F32), pltpu.VMEM((rows, VO_W), BF16)]
    return pl.pallas_call(
        _mixer_prompt_kernel,
        grid=(seq // c,),
        in_specs=in_specs,
        out_specs=out_specs,
        out_shape=out_shape,
        scratch_shapes=scratch,
        compiler_params=pltpu.CompilerParams(
            dimension_semantics=("arbitrary",), vmem_limit_bytes=VMEM_LIMIT),
        name="mixer_prompt",
    )(x, mod, *consts)


def _step_head(s_ref, so_ref, t_scr, i, v_t):
    acc = jnp.zeros(v_t.shape, F32)
    for k in range(s_ref.shape[0]):
        new = s_ref[k] * t_scr[1, i, k:k + 1, :] + t_scr[2, i, k:k + 1, :] * v_t
        so_ref[k] = new
        acc = acc + new * t_scr[0, i, k:k + 1, :]
    return acc


def _mixer_sample_kernel(x_ref, mod_ref, g1_ref, win_ref, lbv_ref, walpha_ref, balpha_ref,
                         hgc_ref, ggc_ref, abarc_ref, bbt_ref, cc_ref, dsk_ref, wglu_ref, bglu_ref,
                         wout_ref, sa_ref, sb_ref, s5r_ref, s5i_ref, pa_ref, pb_ref, pr_ref, pi_ref,
                         xo_ref, sao_ref, sbo_ref, s5ro_ref, s5io_ref,
                         p_scr, at_scr, bt_scr, vt_scr, gate_scr, oh_scr, oc_scr):
    del pa_ref, pb_ref, pr_ref, pi_ref
    i = pl.program_id(0)

    @pl.when(i == 0)
    def _():
        h = _rms_rows(x_ref[...]) * g1_ref[...]
        h = (h * (1.0 + mod_ref[:, D:2 * D]) + mod_ref[:, 0:D]).astype(BF16)
        _project_all(h, win_ref, lbv_ref, walpha_ref, balpha_ref, 1, p_scr)
        for g, (qc, kc, bc, _, nh, dk) in enumerate(BLOCKS):
            t_scr, h0 = (at_scr, 2 * g) if g < 3 else (bt_scr, 4 * (g - 3))
            qt = p_scr[:, qc:qc + LANES].T
            dt = jnp.exp(p_scr[:, bc:bc + LANES]).T
            kt = p_scr[:, kc:kc + LANES].T
            for hh in range(nh):
                t_scr[0, h0 + hh] = qt[hh * dk:(hh + 1) * dk, :]
                t_scr[1, h0 + hh] = dt[hh * dk:(hh + 1) * dk, :]
                t_scr[2, h0 + hh] = kt[hh * dk:(hh + 1) * dk, :]
        for p in range(A_H // 2):
            for grp, vcol, gcol in ((0, VA, GA), (1, VB, RB)):
                vt = p_scr[:, vcol + LANES * p:vcol + LANES * (p + 1)].T
                gt = p_scr[:, gcol + LANES * p:gcol + LANES * (p + 1)].T
                for hh in range(2):
                    vt_scr[grp, 2 * p + hh] = vt[64 * hh:64 * (hh + 1), :]
                    gate_scr[grp, 2 * p + hh] = gt[64 * hh:64 * (hh + 1), :]
        u = p_scr[:, CU:CU + C_W]
        xri = _dot_nt(bbt_ref[...], u.astype(BF16))
        ar = abarc_ref[:, 0:1]
        ai = abarc_ref[:, 1:2]
        s0r = s5r_ref[...]
        s0i = s5i_ref[...]
        sr = ar * s0r - ai * s0i + xri[0:C_S, :]
        si = ar * s0i + ai * s0r + xri[C_S:2 * C_S, :]
        s5ro_ref[...] = sr
        s5io_ref[...] = si
        y = (_dot_tn(sr.astype(BF16), cc_ref[0:C_S, :]) + _dot_tn(si.astype(BF16), cc_ref[C_S:2 * C_S, :])
             + dsk_ref[...] * u)
        z = _gelu_tanh(y)
        oc_scr[...] = z * _sigmoid(_dot(z.astype(BF16), wglu_ref[...]) + bglu_ref[...])

    def norm_gate(o, gain_col, gate):
        return o * lax.rsqrt(jnp.mean(o * o, axis=0, keepdims=True) + EPS) * gain_col * gate

    oa = _step_head(sa_ref, sao_ref, at_scr, i, vt_scr[0, i])
    oh_scr[0, i] = norm_gate(oa, hgc_ref[...], gate_scr[0, i])
    ob = _step_head(sb_ref, sbo_ref, bt_scr, i, vt_scr[1, i])
    oh_scr[1, i] = norm_gate(ob, ggc_ref[...], gate_scr[1, i])

    @pl.when(i == A_H - 1)
    def _():
        parts = []
        for grp in range(2):
            for p in range(A_H // 2):
                two = jnp.concatenate([oh_scr[grp, 2 * p], oh_scr[grp, 2 * p + 1]], axis=0)
                parts.append(two.T)
        o = jnp.concatenate(parts + [oc_scr[...]], axis=1).astype(BF16)
        xo_ref[...] = x_ref[...] + mod_ref[:, 2 * D:3 * D] * _dot(o, wout_ref[...])


def _mixer_sample_call(l, x, mod, lp, states, prev):
    n = x.shape[0]
    consts = [lp[k] for k in ("g1", "w_in", "lbv", "w_alpha", "b_alpha", "hgrn_gc", "gla_gc", "abar_c", "bb_t",
                              "cc", "d_skip", "w_glu", "b_glu", "w_out")]
    state_specs = [pl.BlockSpec((None, None, A_DK, A_DV, n), lambda i: (l, i, 0, 0, 0)),
                   pl.BlockSpec((None, None, B_DK, B_DV, n), lambda i: (l, i, 0, 0, 0)),
                   pl.BlockSpec((None, C_S, n), lambda i: (l, 0, 0)),
                   pl.BlockSpec((None, C_S, n), lambda i: (l, 0, 0))]
    in_specs = ([pl.BlockSpec((n, D), lambda i: (0, 0), pipeline_mode=pl.Buffered(1)),
                 pl.BlockSpec((None, n, 6 * D), lambda i: (l, 0, 0), pipeline_mode=pl.Buffered(1))]
                + [_layer_spec(a, l) for a in consts] + state_specs
                + [pl.BlockSpec(memory_space=pl.ANY)] * 4)
    n_in = 2 + len(consts) + 4
    out_shape = (jax.ShapeDtypeStruct((n, D), F32),) + tuple(jax.ShapeDtypeStruct(s.shape, F32) for s in states)
    out_specs = (pl.BlockSpec((n, D), lambda i: (0, 0)),) + tuple(state_specs)
    scratch = [pltpu.VMEM((n, PW), F32), pltpu.VMEM((3, A_H, A_DK, n), F32), pltpu.VMEM((3, 8, B_DK, n), F32),
               pltpu.VMEM((2, A_H, 64, n), F32), pltpu.VMEM((2, A_H, 64, n), F32),
               pltpu.VMEM((2, A_H, 64, n), F32), pltpu.VMEM((n, C_W), F32)]
    return pl.pallas_call(
        _mixer_sample_kernel,
        grid=(A_H,),
        in_specs=in_specs,
        out_specs=out_specs,
        out_shape=out_shape,
        scratch_shapes=scratch,
        input_output_aliases={n_in + j: 1 + j for j in range(4)},
        compiler_params=pltpu.CompilerParams(
            dimension_semantics=("arbitrary",), vmem_limit_bytes=VMEM_LIMIT),
        name="mixer_sample",
    )(x, mod, *consts, *states, *prev)


def _top2_gates(logits, gidx, lane):
    neg = jnp.float32(-jnp.inf)
    big = jnp.float32(1 << 20)
    own = jnp.sum(jnp.where(lane == gidx, logits, 0.0), axis=1, keepdims=True)
    p_group = 1.0 / jnp.sum(jnp.where(lane < N_GROUPS, jnp.exp(logits - own), 0.0), axis=1, keepdims=True)
    e0 = N_GROUPS + N_PER_GROUP * gidx
    el = jnp.where((lane >= e0) & (lane < e0 + N_PER_GROUP), logits, neg)
    v1 = jnp.max(el, axis=1, keepdims=True)
    i1 = jnp.min(jnp.where(el == v1, lane, big), axis=1, keepdims=True)
    el2 = jnp.where(lane == i1, neg, el)
    v2 = jnp.max(el2, axis=1, keepdims=True)
    i2 = jnp.min(jnp.where(el2 == v2, lane, big), axis=1, keepdims=True)
    ex = jnp.exp(v2 - v1)
    w1 = 1.0 / (1.0 + ex)
    return jnp.where(lane == i1, p_group * w1, 0.0) + jnp.where(lane == i2, p_group * (ex * w1), 0.0)


def _group_experts(h, gates, lane, lane0, wg_ref, wu_ref, wd_ref):
    y = None
    for e in range(N_PER_GROUP):
        hid = _silu(_dot(h, wg_ref[e])) * _dot(h, wu_ref[e])
        gcol = jnp.sum(jnp.where(lane == lane0 + e, gates, 0.0), axis=1, keepdims=True)
        ye = _dot((hid * gcol).astype(BF16), wd_ref[e])
        y = ye if y is None else y + ye
    return y


def _moe_dense_kernel(x_ref, mod_ref, g2_ref, wr_ref, br_ref, wg_ref, wu_ref, wd_ref, fg_ref,
                      o_ref, h_scr, gate_scr, acc_scr, *, final_norm):
    g = pl.program_id(1)
    tm = x_ref.shape[0]
    lane = lax.broadcasted_iota(jnp.int32, (tm, LANES), 1).astype(F32)

    @pl.when(g == 0)
    def _():
        h = _rms_rows(x_ref[...]) * g2_ref[...]
        h = (h * (1.0 + mod_ref[:, 4 * D:5 * D]) + mod_ref[:, 3 * D:4 * D]).astype(BF16)
        h_scr[...] = h
        logits = _dot(h, wr_ref[...]) + br_ref[...]
        gl = jnp.where(lane < N_GROUPS, logits, -jnp.inf)
        gmax = jnp.max(gl, axis=1, keepdims=True)
        gidx = jnp.min(jnp.where(gl == gmax, lane, jnp.float32(1 << 20)), axis=1, keepdims=True)
        gate_scr[...] = _top2_gates(logits, gidx, lane)
        acc_scr[...] = jnp.zeros_like(acc_scr)

    lane0 = (N_GROUPS + N_PER_GROUP * g).astype(F32)
    acc_scr[...] += _group_experts(h_scr[...], gate_scr[...], lane, lane0, wg_ref, wu_ref, wd_ref)

    @pl.when(g == N_GROUPS - 1)
    def _():
        xn = x_ref[...] + mod_ref[:, 5 * D:6 * D] * acc_scr[...]
        if final_norm:
            xn = _rms_rows(xn) * fg_ref[...]
        o_ref[...] = xn


def _expert_specs(lp, l):
    def spec(arr):
        return pl.BlockSpec((None, N_PER_GROUP) + arr.shape[2:], lambda i, g, *_: (l, g, 0, 0))
    return [spec(lp["w_gate"]), spec(lp["w_up"]), spec(lp["w_down"])]


def _moe_dense_call(l, x2, mod, lp, final_g, *, final_norm):
    tm = x2.shape[0]
    kern = functools.partial(_moe_dense_kernel, final_norm=final_norm)
    return pl.pallas_call(
        kern,
        grid=(1, N_GROUPS),
        in_specs=[pl.BlockSpec((tm, D), lambda i, g: (0, 0)),
                  pl.BlockSpec((None, tm, 6 * D), lambda i, g: (l, 0, 0)),
                  _layer_spec(lp["g2"], l), _layer_spec(lp["w_router"], l), _layer_spec(lp["b_router"], l)]
        + _expert_specs(lp, l) + [pl.BlockSpec((1, D), lambda i, g: (0, 0))],
        out_specs=pl.BlockSpec((tm, D), lambda i, g: (0, 0)),
        out_shape=jax.ShapeDtypeStruct((tm, D), F32),
        scratch_shapes=[pltpu.VMEM((tm, D), BF16), pltpu.VMEM((tm, LANES), F32), pltpu.VMEM((tm, D), F32)],
        compiler_params=pltpu.CompilerParams(
            dimension_semantics=("arbitrary", "arbitrary"), vmem_limit_bytes=VMEM_LIMIT),
        name="moe_sample",
    )(x2, mod, lp["g2"], lp["w_router"], lp["b_router"], lp["w_gate"], lp["w_up"], lp["w_down"], final_g)


def _route_kernel(x_ref, mod_ref, g2_ref, wrt_ref, brc_ref, tri_ref, pos_ref, offs_ref, *, tiles_per_seq):
    tt = x_ref.shape[0]
    b = pl.program_id(0) // tiles_per_seq
    h = _rms_rows(x_ref[...]) * g2_ref[...]
    h = (h * (1.0 + mod_ref[pl.ds(b, 1), 4 * D:5 * D]) + mod_ref[pl.ds(b, 1), 3 * D:4 * D]).astype(BF16)
    lt = _dot_nt(wrt_ref[...], h) + brc_ref[...]
    row = lax.broadcasted_iota(jnp.int32, (SUBLANES, tt), 0).astype(F32)
    gl = jnp.where(row < N_GROUPS, lt[0:SUBLANES, :], -jnp.inf)
    gmax = jnp.max(gl, axis=0, keepdims=True)
    gidx = jnp.min(jnp.where(gl == gmax, row, jnp.float32(1 << 20)), axis=0, keepdims=True)
    onehot = jnp.where(row == gidx, 1.0, 0.0)
    rank = _dot(onehot.astype(BF16), tri_ref[...])
    counts = jnp.sum(onehot, axis=1, keepdims=True)
    row1 = lax.broadcasted_iota(jnp.int32, (SUBLANES, 1), 0)
    off = jnp.zeros((SUBLANES, 1), F32)
    for g in range(N_GROUPS - 1):
        off = off + jnp.where(row1 > g, counts[g:g + 1, :], 0.0)
    pos_ref[0] = jnp.sum(onehot * (rank + off), axis=0, keepdims=True).astype(jnp.int32)
    lane = lax.broadcasted_iota(jnp.int32, (1, LANES), 1)
    offs = jnp.where(lane == N_GROUPS, float(tt), 0.0)
    for g in range(1, N_GROUPS):
        offs = offs + jnp.where(lane == g, off[g:g + 1, :], 0.0)
    offs_ref[0] = offs.astype(jnp.int32)


def _route_call(l, x2, mod, ns, nb, lp, *, tt):
    t = x2.shape[0]
    nt = t // tt
    tiles_per_seq = nt // nb
    tri = jnp.asarray(np.triu(np.ones((tt, tt), np.float32), 1), BF16)
    kern = functools.partial(_route_kernel, tiles_per_seq=tiles_per_seq)
    return pl.pallas_call(
        kern,
        grid=(nt,),
        in_specs=[pl.BlockSpec((tt, D), lambda i: (i, 0)),
                  pl.BlockSpec((None, nb, 6 * D), lambda i: (l, ns // nb, 0), pipeline_mode=pl.Buffered(1)),
                  _layer_spec(lp["g2"], l), _layer_spec(lp["w_router_t"], l), _layer_spec(lp["b_router_c"], l),
                  pl.BlockSpec((tt, tt), lambda i: (0, 0), pipeline_mode=pl.Buffered(1))],
        out_specs=(pl.BlockSpec((1, 1, tt), lambda i: (i, 0, 0)),
                   pl.BlockSpec((1, 1, LANES), lambda i: (i, 0, 0))),
        out_shape=(jax.ShapeDtypeStruct((nt, 1, tt), jnp.int32),
                   jax.ShapeDtypeStruct((nt, 1, LANES), jnp.int32)),
        compiler_params=pltpu.CompilerParams(
            dimension_semantics=("arbitrary",), vmem_limit_bytes=VMEM_LIMIT),
        name="moe_route",
    )(x2, mod, lp["g2"], lp["w_router_t"], lp["b_router_c"], tri)


def _top2_gates_t(lt, gidx):
    rows = -(-(N_GROUPS + N_EXP) // SUBLANES) * SUBLANES
    n = lt.shape[1]
    neg = jnp.float32(-jnp.inf)
    big = jnp.float32(1 << 20)
    l = lt[0:rows, :]
    row = lax.broadcasted_iota(jnp.int32, (rows, n), 0).astype(F32)
    own = jnp.sum(jnp.where(row == gidx, l, 0.0), axis=0, keepdims=True)
    p_group = 1.0 / jnp.sum(jnp.where(row < N_GROUPS, jnp.exp(l - own), 0.0), axis=0, keepdims=True)
    e0 = N_GROUPS + N_PER_GROUP * gidx
    el = jnp.where((row >= e0) & (row < e0 + N_PER_GROUP), l, neg)
    v1 = jnp.max(el, axis=0, keepdims=True)
    i1 = jnp.min(jnp.where(el == v1, row, big), axis=0, keepdims=True)
    el2 = jnp.where(row == i1, neg, el)
    v2 = jnp.max(el2, axis=0, keepdims=True)
    i2 = jnp.min(jnp.where(el2 == v2, row, big), axis=0, keepdims=True)
    ex = jnp.exp(v2 - v1)
    w1 = 1.0 / (1.0 + ex)
    gates = jnp.where(row == i1, p_group * w1, 0.0) + jnp.where(row == i2, p_group * (ex * w1), 0.0)
    return jnp.concatenate([gates, jnp.zeros((LANES - rows, n), F32)], axis=0)


def _moe_sorted_kernel(pos_sm, offs_sm, x_ref, mod_ref, g2_ref, wrt_ref, brc_ref, wg_ref, wu_ref, wd_ref, fg_ref,
                       o_ref, a_scr, b_scr, hs_scr, gate_scr, acc_scr, *, tiles_per_seq, final_norm):
    i = pl.program_id(0)
    g = pl.program_id(1)
    tt = x_ref.shape[0]
    base = i * tt
    b = i // tiles_per_seq

    def mod(k):
        return mod_ref[pl.ds(b, 1), k * D:(k + 1) * D]

    def tile_rows(t):
        return pl.ds(pl.multiple_of(t * SUBLANES, SUBLANES), SUBLANES)

    def to_token_tiles(v):
        for s in range(SUBLANES):
            a_scr[s:tt * SUBLANES:SUBLANES, :] = v[:, s * LANES:(s + 1) * LANES]

    def from_token_tiles():
        return jnp.concatenate([b_scr[s:tt * SUBLANES:SUBLANES, :] for s in range(SUBLANES)], axis=1)

    @pl.when(g == 0)
    def _():
        h = _rms_rows(x_ref[...]) * g2_ref[...]
        to_token_tiles(h * (1.0 + mod(4)) + mod(3))

        def scatter(t, carry):
            b_scr[tile_rows(pos_sm[base + t]), :] = a_scr[tile_rows(t), :]
            return carry
        lax.fori_loop(0, tt, scatter, 0, unroll=16)
        hs = from_token_tiles().astype(BF16)
        hs_scr[0:tt, :] = hs
        hs_scr[tt:tt + MOE_BLOCK, :] = jnp.zeros((MOE_BLOCK, D), BF16)
        lt = _dot_nt(wrt_ref[...], hs) + brc_ref[...]
        slot = lax.broadcasted_iota(jnp.int32, (1, tt), 1)
        gidx = jnp.zeros((1, tt), F32)
        for gg in range(1, N_GROUPS):
            gidx = gidx + jnp.where(slot >= offs_sm[i * LANES + gg], 1.0, 0.0)
        gate_scr[0:tt, :] = _top2_gates_t(lt, gidx).T
        gate_scr[tt:tt + MOE_BLOCK, :] = jnp.zeros((MOE_BLOCK, LANES), F32)
        acc_scr[...] = jnp.zeros_like(acc_scr)

    lo = offs_sm[i * LANES + g]
    hi = offs_sm[i * LANES + g + 1]
    start = (lo // BF16_ROWS) * BF16_ROWS
    nblk = jnp.where(hi > lo, (hi - start + MOE_BLOCK - 1) // MOE_BLOCK, 0)
    lane0 = (N_GROUPS + N_PER_GROUP * g).astype(F32)
    blk_lane = lax.broadcasted_iota(jnp.int32, (MOE_BLOCK, LANES), 1).astype(F32)

    def block(k, carry):
        rs = pl.ds(pl.multiple_of(start + k * MOE_BLOCK, BF16_ROWS), MOE_BLOCK)
        acc_scr[rs, :] += _group_experts(hs_scr[rs, :], gate_scr[rs, :], blk_lane, lane0, wg_ref, wu_ref, wd_ref)
        return carry
    lax.fori_loop(0, nblk, block, 0)

    @pl.when(g == N_GROUPS - 1)
    def _():
        to_token_tiles(acc_scr[0:tt, :])

        def gather(t, carry):
            b_scr[tile_rows(t), :] = a_scr[tile_rows(pos_sm[base + t]), :]
            return carry
        lax.fori_loop(0, tt, gather, 0, unroll=16)
        xn = x_ref[...] + mod(5) * from_token_tiles()
        if final_norm:
            xn = _rms_rows(xn) * fg_ref[...]
        o_ref[...] = xn


def _moe_sorted_call(l, x2, mod, ns, nb, lp, final_g, pos, offs, *, tt, final_norm):
    t = x2.shape[0]
    nt = t // tt
    kern = functools.partial(_moe_sorted_kernel, tiles_per_seq=nt // nb, final_norm=final_norm)
    grid_spec = pltpu.PrefetchScalarGridSpec(
        num_scalar_prefetch=2,
        grid=(nt, N_GROUPS),
        in_specs=[pl.BlockSpec((tt, D), lambda i, g, *_: (i, 0)),
                  pl.BlockSpec((None, nb, 6 * D), lambda i, g, *_: (l, ns // nb, 0), pipeline_mode=pl.Buffered(1)),
                  _layer_spec(lp["g2"], l), _layer_spec(lp["w_router_t"], l), _layer_spec(lp["b_router_c"], l)]
        + _expert_specs(lp, l) + [pl.BlockSpec((1, D), lambda i, g, *_: (0, 0))],
        out_specs=pl.BlockSpec((tt, D), lambda i, g, *_: (i, 0)),
        scratch_shapes=[pltpu.VMEM((tt * SUBLANES, LANES), F32), pltpu.VMEM((tt * SUBLANES, LANES), F32),
                        pltpu.VMEM((tt + MOE_BLOCK, D), BF16), pltpu.VMEM((tt + MOE_BLOCK, LANES), F32),
                        pltpu.VMEM((tt + MOE_BLOCK, D), F32)])
    return pl.pallas_call(
        kern,
        grid_spec=grid_spec,
        out_shape=jax.ShapeDtypeStruct((t, D), F32),
        compiler_params=pltpu.CompilerParams(
            dimension_semantics=("arbitrary", "arbitrary"), vmem_limit_bytes=VMEM_LIMIT),
        name="moe_prompt",
    )(pos, offs, x2, mod, lp["g2"], lp["w_router_t"], lp["b_router_c"], lp["w_gate"], lp["w_up"], lp["w_down"],
      final_g)


def _pad_to(a, axis, size):
    pad = [(0, 0)] * a.ndim
    pad[axis] = (0, size - a.shape[axis])
    return jnp.pad(a, pad)


def _prepare_params(p):
    lb_cum = jnp.cumsum(jax.nn.softmax(p["hgrn_lb_logits"].astype(F32), axis=0), axis=0)
    lb = lb_cum - lb_cum[:1]
    lbv = _pad_to(jnp.stack([jnp.log(lb), jnp.log1p(-lb), 1.0 - lb], axis=1), 1, SUBLANES)
    a_re, a_im = p["s5_a_re"].astype(F32), p["s5_a_im"].astype(F32)
    dt = jnp.exp(p["s5_log_dt"].astype(F32))[..., None]
    mag = jnp.exp(a_re * dt)
    abar_re = mag * jnp.cos(a_im * dt)
    abar_im = mag * jnp.sin(a_im * dt)
    den = a_re * a_re + a_im * a_im
    nr, ni = abar_re - 1.0, abar_im
    coef_re = ((nr * a_re + ni * a_im) / den)[..., None]
    coef_im = ((ni * a_re - nr * a_im) / den)[..., None]
    b_re, b_im = p["s5_b_re"].astype(F32), p["s5_b_im"].astype(F32)
    bb_re = coef_re * b_re - coef_im * b_im
    bb_im = coef_re * b_im + coef_im * b_re
    eye = jnp.eye(C_G, dtype=F32)
    blk_b = lambda m: jnp.einsum("lgph,gk->lghkp", m, eye).reshape(DEPTH, C_W, C_S)
    blk_bt = lambda m: jnp.einsum("lgph,gk->lgpkh", m, eye).reshape(DEPTH, C_S, C_W)
    blk_c = lambda m: jnp.einsum("lghp,gk->lgpkh", m, eye).reshape(DEPTH, C_S, C_W)
    bb = jnp.concatenate([blk_b(bb_re), blk_b(bb_im)], axis=2).astype(BF16)
    bb_t = jnp.concatenate([blk_bt(bb_re), blk_bt(bb_im)], axis=1).astype(BF16)
    cc = jnp.concatenate([blk_c(p["s5_c_re"].astype(F32)), -blk_c(p["s5_c_im"].astype(F32))],
                         axis=1).astype(BF16)
    abar_rows = jnp.stack([abar_re.reshape(DEPTH, C_S), abar_im.reshape(DEPTH, C_S)], axis=1)
    router = jnp.concatenate([p["moe_w_group"], p["moe_w_expert"]], axis=2)
    b_router = _pad_to(jnp.concatenate([p["moe_b_group"], p["moe_b_expert"]], axis=1), 1, LANES)
    row = lambda a: a.astype(F32)[:, None, :]
    col = lambda a: a.astype(F32)[:, :, None]
    return dict(
        g1=row(p["norm1_g"]), g2=row(p["norm2_g"]),
        w_in=jnp.transpose(p["w_in"], (0, 2, 1)).astype(BF16), lbv=lbv,
        w_alpha=_pad_to(_pad_to(p["gla_w_alpha"], 1, LANES), 2, 256).astype(BF16),
        b_alpha=_pad_to(row(p["gla_b_alpha"]), 2, 256),
        hgrn_g=row(jnp.tile(p["hgrn_norm_g"], (1, A_H))), gla_g=row(jnp.tile(p["gla_norm_g"], (1, B_H))),
        hgrn_gc=col(p["hgrn_norm_g"]), gla_gc=col(p["gla_norm_g"]),
        abar=_pad_to(abar_rows, 1, SUBLANES), abar_c=jnp.transpose(abar_rows, (0, 2, 1)),
        bb=bb, bb_t=bb_t, cc=cc, d_skip=row(p["s5_d"]),
        w_glu=p["s5_w_glu"].astype(BF16), b_glu=row(p["s5_b_glu"]), w_out=p["w_out"].astype(BF16),
        w_gate=p["moe_w_gate"].astype(BF16), w_up=p["moe_w_up"].astype(BF16),
        w_down=p["moe_w_down"].astype(BF16),
        w_router=_pad_to(router, 2, LANES).astype(BF16), b_router=b_router[:, None, :],
        w_router_t=_pad_to(jnp.transpose(router, (0, 2, 1)), 1, LANES).astype(BF16),
        b_router_c=b_router[:, :, None])


def kernel(x_prompt, x_sample, state_hgrn, state_gla, state_s5_re, state_s5_im, c_prompt, c_sample, ada_w, ada_b, norm1_g, norm2_g, w_in, hgrn_lb_logits, hgrn_norm_g, gla_w_alpha, gla_b_alpha, gla_norm_g, s5_a_re, s5_a_im, s5_b_re, s5_b_im, s5_c_re, s5_c_im, s5_d, s5_log_dt, s5_w_glu, s5_b_glu, w_out, moe_w_group, moe_b_group, moe_w_expert, moe_b_expert, moe_w_gate, moe_w_up, moe_w_down, final_norm_g):
    lp = _prepare_params(dict(
        norm1_g=norm1_g, norm2_g=norm2_g, w_in=w_in, hgrn_lb_logits=hgrn_lb_logits, hgrn_norm_g=hgrn_norm_g,
        gla_w_alpha=gla_w_alpha, gla_b_alpha=gla_b_alpha, gla_norm_g=gla_norm_g, s5_a_re=s5_a_re,
        s5_a_im=s5_a_im, s5_b_re=s5_b_re, s5_b_im=s5_b_im, s5_c_re=s5_c_re, s5_c_im=s5_c_im, s5_d=s5_d,
        s5_log_dt=s5_log_dt, s5_w_glu=s5_w_glu, s5_b_glu=s5_b_glu, w_out=w_out, moe_w_group=moe_w_group,
        moe_b_group=moe_b_group, moe_w_expert=moe_w_expert, moe_b_expert=moe_b_expert,
        moe_w_gate=moe_w_gate, moe_w_up=moe_w_up, moe_w_down=moe_w_down))
    nb, seq, _ = x_prompt.shape
    ns = x_sample.shape[0]
    assert ns % nb == 0 and seq % CHUNK == 0
    tt = min(MOE_TILE, seq)
    assert seq % tt == 0 and tt % BF16_ROWS == 0
    final_g = final_norm_g.reshape(1, D)

    mod = _ada_call(jnp.concatenate([c_sample, c_prompt], axis=0), ada_w, ada_b)
    states = (jnp.transpose(state_hgrn, (0, 2, 3, 4, 1)), jnp.transpose(state_gla, (0, 2, 3, 4, 1)),
              jnp.transpose(state_s5_re, (0, 2, 3, 1)).reshape(DEPTH, C_S, ns),
              jnp.transpose(state_s5_im, (0, 2, 3, 1)).reshape(DEPTH, C_S, ns))
    new_states = tuple(jnp.zeros_like(s) for s in states)

    xp = x_prompt
    xs = x_sample.reshape(ns, D)
    outs = {k: [] for k in ("hg_p", "gl_p", "sr_p", "si_p")}
    for l in range(DEPTH):
        last = l == DEPTH - 1
        xp, sta, stb, s5r, s5i = _mixer_prompt_call(l, xp, mod, ns, lp)
        x2 = xp.reshape(nb * seq, D)
        pos, offs = _route_call(l, x2, mod, ns, nb, lp, tt=tt)
        xp = _moe_sorted_call(l, x2, mod, ns, nb, lp, final_g, pos.reshape(-1), offs.reshape(-1),
                              tt=tt, final_norm=last).reshape(nb, seq, D)
        sta = sta.reshape(nb, A_H // 2, 2, A_DV, 2, A_DK)
        outs["hg_p"].append(jnp.stack([sta[:, h // 2, h % 2, :, h % 2, :] for h in range(A_H)], axis=1)
                            .transpose(0, 1, 3, 2))
        stb = stb.reshape(nb, B_H, B_DV, 4, B_DK)
        outs["gl_p"].append(jnp.stack([stb[:, h, :, h % 4, :] for h in range(B_H)], axis=1)
                            .transpose(0, 1, 3, 2))
        outs["sr_p"].append(s5r.reshape(nb, C_G, C_P))
        outs["si_p"].append(s5i.reshape(nb, C_G, C_P))
        xs, *new_states = _mixer_sample_call(l, xs, mod, lp, states, new_states)
        xs = _moe_dense_call(l, xs, mod, lp, final_g, final_norm=last)

    hg_s, gl_s, sr_s, si_s = new_states
    return (xp, xs.reshape(ns, 1, D),
            jnp.stack(outs["hg_p"]), jnp.stack(outs["gl_p"]), jnp.stack(outs["sr_p"]), jnp.stack(outs["si_p"]),
            jnp.transpose(hg_s, (0, 4, 1, 2, 3)), jnp.transpose(gl_s, (0, 4, 1, 2, 3)),
            jnp.transpose(sr_s.reshape(DEPTH, C_G, C_P, ns), (0, 3, 1, 2)),
            jnp.transpose(si_s.reshape(DEPTH, C_G, C_P, ns), (0, 3, 1, 2)))
```

```python
import functools
import math

import numpy as np
import jax
import jax.numpy as jnp
from jax import lax
from jax.experimental import pallas as pl
from jax.experimental.pallas import tpu as pltpu

F32 = jnp.float32
BF16 = jnp.bfloat16

D = 1024
DEPTH = 4
A_H, A_DK, A_DV = 6, 64, 64
B_H, B_DK, B_DV = 6, 32, 64
B_RANK = 16
B_TAU = 16.0
C_G, C_CH, C_P = 16, 16, 64
C_W = C_G * C_CH
C_S = C_G * C_P
N_GROUPS, N_PER_GROUP, D_EXP = 4, 4, 256
N_EXP = N_GROUPS * N_PER_GROUP
EPS = 1e-6
CHUNK = 64
LANES = 128
SUBLANES = 8
EXP_RANGE = 80.0
MOE_TILE = 1024
MOE_BLOCK = 320
BF16_ROWS = 16

W_AQ, W_AF, W_AI, W_AG = 0, 384, 768, 1152
W_BQ, W_BK, W_BV, W_LR, W_BR, W_CU = 1536, 1728, 1920, 2304, 2320, 2704
IN_COLS = 2960
W_SEGMENTS = {"aq|af": (W_AQ, 768), "ai|ag": (W_AI, 768), "bq": (W_BQ, 256), "bk": (W_BK, 256),
              "bv|lr": (W_BV, 512), "br|cu": (W_BR, 640)}
QA, FA, VA, GA = 0, 384, 768, 1152
QB, KB, VB, RB = 1536, 1792, 2048, 2432
CU, LB, KA = 2816, 3072, 3328
PW = 3712
N_LG = 5
LG_W = N_LG * LANES
BLOCKS = tuple((QA + LANES * p, KA + LANES * p, FA + LANES * p, LANES * p, 2, A_DK) for p in range(3)) + (
    (QB, KB, LB, 384, 4, B_DK), (QB + LANES, KB + LANES, LB + LANES, 640, 2, B_DK))
VO_W = 768

VMEM_LIMIT = 56 * 1024 * 1024


def _sigmoid(x):
    return 1.0 / (1.0 + jnp.exp(-x))


def _silu(x):
    return x * _sigmoid(x)


def _log_sigmoid(x):
    return jnp.minimum(x, 0.0) - jnp.log(1.0 + jnp.exp(-jnp.abs(x)))


def _gelu_tanh(x):
    return 0.5 * x * (1.0 + jnp.tanh(math.sqrt(2.0 / math.pi) * (x + 0.044715 * x * x * x)))


def _split3(x):
    hi = x.astype(BF16)
    r = x - hi.astype(F32)
    mid = r.astype(BF16)
    lo = (r - mid.astype(F32)).astype(BF16)
    return hi, mid, lo


def _dot(a, b):
    return jnp.dot(a, b, preferred_element_type=F32)


def _dot_nt(a, b):
    return lax.dot_general(a, b, (((1,), (1,)), ((), ())), preferred_element_type=F32)


def _dot_tn(a, b):
    return lax.dot_general(a, b, (((0,), (0,)), ((), ())), preferred_element_type=F32)


def _dot_exact_01(x, m01):
    hi, mid, lo = _split3(x)
    return _dot(hi, m01) + _dot(mid, m01) + _dot(lo, m01)


def _rms_rows(x):
    return x * lax.rsqrt(jnp.mean(x * x, axis=-1, keepdims=True) + EPS)


def _forget_gate(af, lbv):
    e = jnp.exp(-jnp.abs(af))
    log_sig = jnp.minimum(af, 0.0) - jnp.log(1.0 + e)
    a = lbv[0:1]
    b = lbv[1:2] + log_sig
    log_f = jnp.maximum(a, b) + jnp.log(1.0 + jnp.exp(-jnp.abs(a - b)))
    k = lbv[2:3] * (jnp.where(af >= 0.0, e, 1.0) / (1.0 + e))
    return log_f, k


def _chunk_cumsum(x, c):
    pos = lax.broadcasted_iota(jnp.int32, x.shape, 0) % c
    s = 1
    while s < c:
        x = x + jnp.where(pos >= s, pltpu.roll(x, s, 0), 0.0)
        s *= 2
    return x


def _head_rms(o, dv):
    w = o.shape[1]
    r = lax.broadcasted_iota(jnp.int32, (w, w), 0)
    cidx = lax.broadcasted_iota(jnp.int32, (w, w), 1)
    ones_blk = jnp.where(r // dv == cidx // dv, 1.0, 0.0).astype(BF16)
    return o * lax.rsqrt(_dot((o * o).astype(BF16), ones_blk) * (1.0 / dv) + EPS)


def _project_all(h, wint_ref, lbv_ref, walpha_ref, balpha_ref, chunk, p_scr):
    def proj(name):
        r0, n = W_SEGMENTS[name]
        return _dot_nt(h, wint_ref[r0:r0 + n, :])

    def cum(x):
        return x if chunk == 1 else _chunk_cumsum(x, chunk)

    aqf = proj("aq|af")
    p_scr[:, QA:QA + 384] = _silu(aqf[:, 0:384])
    log_f, ka = _forget_gate(aqf[:, 384:768], lbv_ref[...])
    p_scr[:, FA:FA + 384] = cum(log_f)
    p_scr[:, KA:KA + 384] = ka
    aig = proj("ai|ag")
    p_scr[:, VA:VA + 384] = aig[:, 0:384]
    p_scr[:, GA:GA + 384] = _silu(aig[:, 384:768])
    p_scr[:, QB:QB + 256] = proj("bq") * (B_DK ** -0.5)
    kb = proj("bk")
    p_scr[:, KB:KB + 256] = jnp.where(lax.broadcasted_iota(jnp.int32, kb.shape, 1) < B_H * B_DK, kb, 0.0)
    bvlr = proj("bv|lr")
    p_scr[:, VB:VB + 384] = bvlr[:, 0:384]
    z = _dot(bvlr[:, 384:512].astype(BF16), walpha_ref[...]) + balpha_ref[...]
    p_scr[:, LB:LB + 256] = cum(_log_sigmoid(z) * (1.0 / B_TAU))
    brcu = proj("br|cu")
    p_scr[:, RB:RB + 384] = _silu(brcu[:, 0:384])
    p_scr[:, CU:CU + 256] = brcu[:, 384:640]


def _layer_spec(arr, l):
    nd = arr.ndim - 1
    return pl.BlockSpec((None,) + arr.shape[1:], lambda *_, _n=nd: (l,) + (0,) * _n,
                        pipeline_mode=pl.Buffered(1))


def _ada_kernel(c_ref, w_ref, b_ref, o_ref):
    sc = _silu(c_ref[...]).astype(BF16)
    o_ref[0] = _dot(sc, w_ref[0].astype(BF16)) + b_ref[0]


def _ada_call(c_all, ada_w, ada_b):
    n = c_all.shape[0]
    tn = 1536
    return pl.pallas_call(
        _ada_kernel,
        grid=(DEPTH, 6 * D // tn),
        in_specs=[
            pl.BlockSpec((n, D), lambda l, j: (0, 0)),
            pl.BlockSpec((1, D, tn), lambda l, j: (l, 0, j)),
            pl.BlockSpec((1, 1, tn), lambda l, j: (l, 0, j)),
        ],
        out_specs=pl.BlockSpec((1, n, tn), lambda l, j: (l, 0, j)),
        out_shape=jax.ShapeDtypeStruct((DEPTH, n, 6 * D), F32),
        compiler_params=pltpu.CompilerParams(
            dimension_semantics=("arbitrary", "arbitrary"), vmem_limit_bytes=VMEM_LIMIT),
        name="ada_mod",
    )(c_all, ada_w, ada_b.reshape(DEPTH, 1, 6 * D))


def _mixer_prompt_kernel(x_ref, mod_ref, g1_ref, win_ref, lbv_ref, walpha_ref, balpha_ref,
                         hg_ref, gg_ref, abar_ref, bb_ref, cc_ref, dsk_ref, wglu_ref, bglu_ref,
                         wout_ref,
                         xo_ref, sta_ref, stb_ref, s5r_ref, s5i_ref,
                         h_scr, p_scr, o_scr, xr_scr, xi_scr, t_scr,
                         qt_scr, kt_scr, qd_scr, kd_scr, v_scr, of_scr, dec_scr, sc_scr):
    nb = x_ref.shape[0]
    c = CHUNK
    i = pl.program_id(0)

    @pl.when(i == 0)
    def _():
        sta_ref[...] = jnp.zeros_like(sta_ref)
        stb_ref[...] = jnp.zeros_like(stb_ref)
        s5r_ref[...] = jnp.zeros_like(s5r_ref)
        s5i_ref[...] = jnp.zeros_like(s5i_ref)

    for b in range(nb):
        hb = _rms_rows(x_ref[b]) * g1_ref[...]
        hb = hb * (1.0 + mod_ref[b:b + 1, D:2 * D]) + mod_ref[b:b + 1, 0:D]
        h_scr[b * c:(b + 1) * c, :] = hb.astype(BF16)

    _project_all(h_scr[...], win_ref, lbv_ref, walpha_ref, balpha_ref, c, p_scr)

    v_scr[:, 0:384] = p_scr[:, VA:VA + 384].astype(BF16)
    v_scr[:, 384:768] = p_scr[:, VB:VB + 384].astype(BF16)
    rng = jnp.zeros((1, LANES), F32)
    for b in range(nb):
        rs = slice(b * c, (b + 1) * c)
        for g, (qc, kc, bc, _, _, _) in enumerate(BLOCKS):
            ls = slice(g * LANES, (g + 1) * LANES)
            q = p_scr[rs, qc:qc + LANES]
            k = p_scr[rs, kc:kc + LANES]
            bcum = p_scr[rs, bc:bc + LANES]
            b_mid = bcum[c // 2 - 1:c // 2, :]
            b_end = bcum[c - 1:c, :]
            d = bcum - b_mid
            rng = jnp.maximum(rng, jnp.maximum(jnp.abs(d[0:1, :]), jnp.abs(d[c - 1:c, :])))
            qt_scr[rs, ls] = (q * jnp.exp(d)).astype(BF16)
            kt_scr[rs, ls] = (k * jnp.exp(-d)).astype(BF16)
            qd_scr[rs, ls] = (q * jnp.exp(bcum)).astype(BF16)
            kd_scr[rs, ls] = (k * jnp.exp(b_end - bcum)).astype(BF16)
            dec_scr[b, :, ls] = jnp.broadcast_to(jnp.exp(b_end), (SUBLANES, LANES))
    halves = C_W // LANES
    for b in range(nb):
        for j in range(halves):
            t_scr[j, b:b + nb * c:nb, :] = p_scr[b * c:(b + 1) * c, CU + LANES * j:CU + LANES * (j + 1)]
    u_bf = jnp.concatenate([t_scr[j] for j in range(halves)], axis=1).astype(BF16)
    xr_scr[...] = _dot(u_bf, bb_ref[:, 0:C_S])
    xi_scr[...] = _dot(u_bf, bb_ref[:, C_S:2 * C_S])

    in_range = jnp.max(rng) <= EXP_RANGE

    lane = lax.broadcasted_iota(jnp.int32, (c, LANES), 1)

    def scores_direct(r0, qc, kc, bc, nh, dk):
        q = p_scr[pl.ds(r0, c), qc:qc + LANES]
        bcum = p_scr[pl.ds(r0, c), bc:bc + LANES]
        krow = lax.broadcasted_iota(jnp.int32, (LANES, nh * c), 0)
        kcol = lax.broadcasted_iota(jnp.int32, (LANES, nh * c), 1)
        sub = lax.broadcasted_iota(jnp.int32, (SUBLANES, LANES), 0)

        def row_of(col0, s):
            r8 = pl.ds(pl.multiple_of(r0 + (s // SUBLANES) * SUBLANES, SUBLANES), SUBLANES)
            return jnp.sum(jnp.where(sub == s % SUBLANES, p_scr[r8, col0:col0 + LANES], 0.0),
                           axis=0, keepdims=True)

        def body(s, sc):
            ks = row_of(kc, s)
            bs = row_of(bc, s)
            w = q * ks * jnp.exp(jnp.minimum(bcum - bs, 0.0))
            place = jnp.where(kcol == (krow // dk) * c + s, 1.0, 0.0).astype(BF16)
            return sc + _dot_exact_01(w, place)

        return lax.fori_loop(0, c, body, jnp.zeros((c, nh * c), F32))

    def stage_scores(b, r0, direct):
        rs = pl.ds(r0, c)
        for g, (qc, kc, bc, vc, nh, dk) in enumerate(BLOCKS):
            ls = slice(g * LANES, (g + 1) * LANES)
            if direct:
                sc = scores_direct(r0, qc, kc, bc, nh, dk)
            else:
                kt = kt_scr[rs, ls]
                kcat = jnp.concatenate(
                    [jnp.where(lane // dk == h, kt, jnp.zeros_like(kt)) for h in range(nh)], axis=0)
                sc = _dot_nt(qt_scr[rs, ls], kcat)
            row = lax.broadcasted_iota(jnp.int32, (c, nh * c), 0)
            col = lax.broadcasted_iota(jnp.int32, (c, nh * c), 1)
            sc_scr[rs, vc:vc + nh * c] = jnp.where(col % c <= row, sc, 0.0).astype(BF16)

    def stage_readout(b, r0):
        rs = pl.ds(r0, c)
        for g, (_, _, _, vc, nh, dk) in enumerate(BLOCKS):
            ls = slice(g * LANES, (g + 1) * LANES)
            vw = nh * 64
            st_ref = sta_ref if g < 3 else stb_ref
            v0 = vc if g < 3 else vc - 384
            v = v_scr[rs, vc:vc + vw]
            vlane = lax.broadcasted_iota(jnp.int32, (c, vw), 1)
            vcat = jnp.concatenate(
                [jnp.where(vlane // 64 == h, v, jnp.zeros_like(v)) for h in range(nh)], axis=0)
            st = st_ref[b, v0:v0 + vw, :]
            of_scr[rs, vc:vc + vw] = (_dot(sc_scr[rs, vc:vc + nh * c], vcat)
                                      + _dot_nt(qd_scr[rs, ls], st.astype(BF16)))

    def stage_update(b, r0):
        rs = pl.ds(r0, c)
        for g, (_, _, _, vc, nh, dk) in enumerate(BLOCKS):
            ls = slice(g * LANES, (g + 1) * LANES)
            vw = nh * 64
            st_ref = sta_ref if g < 3 else stb_ref
            v0 = vc if g < 3 else vc - 384
            upd = _dot_tn(v_scr[rs, vc:vc + vw], kd_scr[rs, ls])
            srow = lax.broadcasted_iota(jnp.int32, (vw, LANES), 0)
            scol = lax.broadcasted_iota(jnp.int32, (vw, LANES), 1)
            st_ref[b, v0:v0 + vw, :] = (st_ref[b, v0:v0 + vw, :] * dec_scr[b, 0:1, ls]
                                        + jnp.where(srow // 64 == scol // dk, upd, 0.0))

    @pl.when(in_range)
    def _():
        for b in range(nb):
            stage_scores(b, b * c, False)
        for b in range(nb):
            stage_readout(b, b * c)
        for b in range(nb):
            stage_update(b, b * c)

    @pl.when(jnp.logical_not(in_range))
    def _():
        def body(b, carry):
            r0 = pl.multiple_of(b * c, c)
            stage_scores(b, r0, True)
            stage_readout(b, r0)
            stage_update(b, r0)
            return carry
        lax.fori_loop(0, nb, body, 0)

    oa = _head_rms(of_scr[:, 0:384], A_DV) * hg_ref[...] * p_scr[:, GA:GA + 384]
    o_scr[:, 0:384] = oa.astype(BF16)
    ob = _head_rms(of_scr[:, 384:768], B_DV) * gg_ref[...] * p_scr[:, RB:RB + 384]
    o_scr[:, 384:768] = ob.astype(BF16)

    ar = abar_ref[0:1, :]
    ai = abar_ref[1:2, :]

    def scan_t(t, carry):
        sr, si = carry
        ts = pl.ds(pl.multiple_of(t * nb, nb), nb)
        nr = ar * sr - ai * si + xr_scr[ts, :]
        ni = ar * si + ai * sr + xi_scr[ts, :]
        xr_scr[ts, :] = nr
        xi_scr[ts, :] = ni
        return nr, ni

    sr, si = lax.fori_loop(0, c, scan_t, (s5r_ref[...], s5i_ref[...]), unroll=2)
    s5r_ref[...] = sr
    s5i_ref[...] = si
    u = jnp.concatenate([t_scr[j] for j in range(halves)], axis=1)
    half = nb * c // 2
    y = jnp.concatenate(
        [_dot(xr_scr[r:r + half, :].astype(BF16), cc_ref[0:C_S, :])
         + _dot(xi_scr[r:r + half, :].astype(BF16), cc_ref[C_S:2 * C_S, :]) for r in (0, half)], axis=0)
    y = y + dsk_ref[...] * u
    z = _gelu_tanh(y)
    oc = z * _sigmoid(_dot(z.astype(BF16), wglu_ref[...]) + bglu_ref[...])
    for j in range(halves):
        t_scr[j] = oc[:, LANES * j:LANES * (j + 1)]
    for b in range(nb):
        for j in range(halves):
            o_scr[b * c:(b + 1) * c, 768 + LANES * j:768 + LANES * (j + 1)] = (
                t_scr[j, b:b + nb * c:nb, :].astype(BF16))

    o_all = o_scr[...]
    for n0 in (0, D // 2):
        out = _dot(o_all, wout_ref[:, n0:n0 + D // 2])
        for b in range(nb):
            xo_ref[b, :, n0:n0 + D // 2] = (x_ref[b, :, n0:n0 + D // 2]
                                            + mod_ref[b:b + 1, 2 * D + n0:2 * D + n0 + D // 2]
                                            * out[b * c:(b + 1) * c, :])


def _mixer_prompt_call(l, x, mod, ns, lp):
    nb, seq, _ = x.shape
    c = CHUNK
    rows = nb * c
    consts = [lp[k] for k in ("g1", "w_in", "lbv", "w_alpha", "b_alpha", "hgrn_g", "gla_g", "abar", "bb", "cc",
                              "d_skip", "w_glu", "b_glu", "w_out")]
    in_specs = ([pl.BlockSpec((nb, c, D), lambda i: (0, i, 0)),
                 pl.BlockSpec((None, nb, 6 * D), lambda i: (l, ns // nb, 0), pipeline_mode=pl.Buffered(1))]
                + [_layer_spec(a, l) for a in consts])
    out_shape = (jax.ShapeDtypeStruct((nb, seq, D), F32),
                 jax.ShapeDtypeStruct((nb, 384, LANES), F32),
                 jax.ShapeDtypeStruct((nb, 384, LANES), F32),
                 jax.ShapeDtypeStruct((nb, C_S), F32),
                 jax.ShapeDtypeStruct((nb, C_S), F32))
    out_specs = (pl.BlockSpec((nb, c, D), lambda i: (0, i, 0)),
                 pl.BlockSpec((nb, 384, LANES), lambda i: (0, 0, 0)),
                 pl.BlockSpec((nb, 384, LANES), lambda i: (0, 0, 0)),
                 pl.BlockSpec((nb, C_S), lambda i: (0, 0)),
                 pl.BlockSpec((nb, C_S), lambda i: (0, 0)))
    scratch = [pltpu.VMEM((rows, D), BF16), pltpu.VMEM((rows, PW), F32), pltpu.VMEM((rows, D), BF16),
               pltpu.VMEM((rows, C_S), F32), pltpu.VMEM((rows, C_S), F32),
               pltpu.VMEM((C_W // LANES, rows, LANES), F32),
               pltpu.VMEM((rows, LG_W), BF16), pltpu.VMEM((rows, LG_W), BF16),
               pltpu.VMEM((rows, LG_W), BF16), pltpu.VMEM((rows, LG_W), BF16),
               pltpu.VMEM((rows, VO_W), BF16), pltpu.VMEM((rows, VO_W), F32),
               pltpu.VMEM((nb, SUBLANES, LG_W), F32), pltpu.VMEM((rows, VO_W), BF16)]
    return pl.pallas_call(
        _mixer_prompt_kernel,
        grid=(seq // c,),
        in_specs=in_specs,
        out_specs=out_specs,
        out_shape=out_shape,
        scratch_shapes=scratch,
        compiler_params=pltpu.CompilerParams(
            dimension_semantics=("arbitrary",), vmem_limit_bytes=VMEM_LIMIT),
        name="mixer_prompt",
    )(x, mod, *consts)


def _step_head(s_ref, so_ref, t_scr, i, v_t):
    acc = jnp.zeros(v_t.shape, F32)
    for k in range(s_ref.shape[0]):
        new = s_ref[k] * t_scr[1, i, k:k + 1, :] + t_scr[2, i, k:k + 1, :] * v_t
        so_ref[k] = new
        acc = acc + new * t_scr[0, i, k:k + 1, :]
    return acc


def _mixer_sample_kernel(x_ref, mod_ref, g1_ref, win_ref, lbv_ref, walpha_ref, balpha_ref,
                         hgc_ref, ggc_ref, abarc_ref, bbt_ref, cc_ref, dsk_ref, wglu_ref, bglu_ref,
                         wout_ref, sa_ref, sb_ref, s5r_ref, s5i_ref, pa_ref, pb_ref, pr_ref, pi_ref,
                         xo_ref, sao_ref, sbo_ref, s5ro_ref, s5io_ref,
                         p_scr, at_scr, bt_scr, vt_scr, gate_scr, oh_scr, oc_scr):
    del pa_ref, pb_ref, pr_ref, pi_ref
    i = pl.program_id(0)

    @pl.when(i == 0)
    def _():
        h = _rms_rows(x_ref[...]) * g1_ref[...]
        h = (h * (1.0 + mod_ref[:, D:2 * D]) + mod_ref[:, 0:D]).astype(BF16)
        _project_all(h, win_ref, lbv_ref, walpha_ref, balpha_ref, 1, p_scr)
        for g, (qc, kc, bc, _, nh, dk) in enumerate(BLOCKS):
            t_scr, h0 = (at_scr, 2 * g) if g < 3 else (bt_scr, 4 * (g - 3))
            qt = p_scr[:, qc:qc + LANES].T
            dt = jnp.exp(p_scr[:, bc:bc + LANES]).T
            kt = p_scr[:, kc:kc + LANES].T
            for hh in range(nh):
                t_scr[0, h0 + hh] = qt[hh * dk:(hh + 1) * dk, :]
                t_scr[1, h0 + hh] = dt[hh * dk:(hh + 1) * dk, :]
                t_scr[2, h0 + hh] = kt[hh * dk:(hh + 1) * dk, :]
        for p in range(A_H // 2):
            for grp, vcol, gcol in ((0, VA, GA), (1, VB, RB)):
                vt = p_scr[:, vcol + LANES * p:vcol + LANES * (p + 1)].T
                gt = p_scr[:, gcol + LANES * p:gcol + LANES * (p + 1)].T
                for hh in range(2):
                    vt_scr[grp, 2 * p + hh] = vt[64 * hh:64 * (hh + 1), :]
                    gate_scr[grp, 2 * p + hh] = gt[64 * hh:64 * (hh + 1), :]
        u = p_scr[:, CU:CU + C_W]
        xri = _dot_nt(bbt_ref[...], u.astype(BF16))
        ar = abarc_ref[:, 0:1]
        ai = abarc_ref[:, 1:2]
        s0r = s5r_ref[...]
        s0i = s5i_ref[...]
        sr = ar * s0r - ai * s0i + xri[0:C_S, :]
        si = ar * s0i + ai * s0r + xri[C_S:2 * C_S, :]
        s5ro_ref[...] = sr
        s5io_ref[...] = si
        y = (_dot_tn(sr.astype(BF16), cc_ref[0:C_S, :]) + _dot_tn(si.astype(BF16), cc_ref[C_S:2 * C_S, :])
             + dsk_ref[...] * u)
        z = _gelu_tanh(y)
        oc_scr[...] = z * _sigmoid(_dot(z.astype(BF16), wglu_ref[...]) + bglu_ref[...])

    def norm_gate(o, gain_col, gate):
        return o * lax.rsqrt(jnp.mean(o * o, axis=0, keepdims=True) + EPS) * gain_col * gate

    oa = _step_head(sa_ref, sao_ref, at_scr, i, vt_scr[0, i])
    oh_scr[0, i] = norm_gate(oa, hgc_ref[...], gate_scr[0, i])
    ob = _step_head(sb_ref, sbo_ref, bt_scr, i, vt_scr[1, i])
    oh_scr[1, i] = norm_gate(ob, ggc_ref[...], gate_scr[1, i])

    @pl.when(i == A_H - 1)
    def _():
        parts = []
        for grp in range(2):
            for p in range(A_H // 2):
                two = jnp.concatenate([oh_scr[grp, 2 * p], oh_scr[grp, 2 * p + 1]], axis=0)
                parts.append(two.T)
        o = jnp.concatenate(parts + [oc_scr[...]], axis=1).astype(BF16)
        xo_ref[...] = x_ref[...] + mod_ref[:, 2 * D:3 * D] * _dot(o, wout_ref[...])


def _mixer_sample_call(l, x, mod, lp, states, prev):
    n = x.shape[0]
    consts = [lp[k] for k in ("g1", "w_in", "lbv", "w_alpha", "b_alpha", "hgrn_gc", "gla_gc", "abar_c", "bb_t",
                              "cc", "d_skip", "w_glu", "b_glu", "w_out")]
    state_specs = [pl.BlockSpec((None, None, A_DK, A_DV, n), lambda i: (l, i, 0, 0, 0)),
                   pl.BlockSpec((None, None, B_DK, B_DV, n), lambda i: (l, i, 0, 0, 0)),
                   pl.BlockSpec((None, C_S, n), lambda i: (l, 0, 0)),
                   pl.BlockSpec((None, C_S, n), lambda i: (l, 0, 0))]
    in_specs = ([pl.BlockSpec((n, D), lambda i: (0, 0), pipeline_mode=pl.Buffered(1)),
                 pl.BlockSpec((None, n, 6 * D), lambda i: (l, 0, 0), pipeline_mode=pl.Buffered(1))]
                + [_layer_spec(a, l) for a in consts] + state_specs
                + [pl.BlockSpec(memory_space=pl.ANY)] * 4)
    n_in = 2 + len(consts) + 4
    out_shape = (jax.ShapeDtypeStruct((n, D), F32),) + tuple(jax.ShapeDtypeStruct(s.shape, F32) for s in states)
    out_specs = (pl.BlockSpec((n, D), lambda i: (0, 0)),) + tuple(state_specs)
    scratch = [pltpu.VMEM((n, PW), F32), pltpu.VMEM((3, A_H, A_DK, n), F32), pltpu.VMEM((3, 8, B_DK, n), F32),
               pltpu.VMEM((2, A_H, 64, n), F32), pltpu.VMEM((2, A_H, 64, n), F32),
               pltpu.VMEM((2, A_H, 64, n), F32), pltpu.VMEM((n, C_W), F32)]
    return pl.pallas_call(
        _mixer_sample_kernel,
        grid=(A_H,),
        in_specs=in_specs,
        out_specs=out_specs,
        out_shape=out_shape,
        scratch_shapes=scratch,
        input_output_aliases={n_in + j: 1 + j for j in range(4)},
        compiler_params=pltpu.CompilerParams(
            dimension_semantics=("arbitrary",), vmem_limit_bytes=VMEM_LIMIT),
        name="mixer_sample",
    )(x, mod, *consts, *states, *prev)


def _top2_gates(logits, gidx, lane):
    neg = jnp.float32(-jnp.inf)
    big = jnp.float32(1 << 20)
    own = jnp.sum(jnp.where(lane == gidx, logits, 0.0), axis=1, keepdims=True)
    p_group = 1.0 / jnp.sum(jnp.where(lane < N_GROUPS, jnp.exp(logits - own), 0.0), axis=1, keepdims=True)
    e0 = N_GROUPS + N_PER_GROUP * gidx
    el = jnp.where((lane >= e0) & (lane < e0 + N_PER_GROUP), logits, neg)
    v1 = jnp.max(el, axis=1, keepdims=True)
    i1 = jnp.min(jnp.where(el == v1, lane, big), axis=1, keepdims=True)
    el2 = jnp.where(lane == i1, neg, el)
    v2 = jnp.max(el2, axis=1, keepdims=True)
    i2 = jnp.min(jnp.where(el2 == v2, lane, big), axis=1, keepdims=True)
    ex = jnp.exp(v2 - v1)
    w1 = 1.0 / (1.0 + ex)
    return jnp.where(lane == i1, p_group * w1, 0.0) + jnp.where(lane == i2, p_group * (ex * w1), 0.0)


def _group_experts(h, gates, lane, lane0, wg_ref, wu_ref, wd_ref):
    y = None
    for e in range(N_PER_GROUP):
        hid = _silu(_dot(h, wg_ref[e])) * _dot(h, wu_ref[e])
        gcol = jnp.sum(jnp.where(lane == lane0 + e, gates, 0.0), axis=1, keepdims=True)
        ye = _dot((hid * gcol).astype(BF16), wd_ref[e])
        y = ye if y is None else y + ye
    return y


def _moe_dense_kernel(x_ref, mod_ref, g2_ref, wr_ref, br_ref, wg_ref, wu_ref, wd_ref, fg_ref,
                      o_ref, h_scr, gate_scr, acc_scr, *, final_norm):
    g = pl.program_id(1)
    tm = x_ref.shape[0]
    lane = lax.broadcasted_iota(jnp.int32, (tm, LANES), 1).astype(F32)

    @pl.when(g == 0)
    def _():
        h = _rms_rows(x_ref[...]) * g2_ref[...]
        h = (h * (1.0 + mod_ref[:, 4 * D:5 * D]) + mod_ref[:, 3 * D:4 * D]).astype(BF16)
        h_scr[...] = h
        logits = _dot(h, wr_ref[...]) + br_ref[...]
        gl = jnp.where(lane < N_GROUPS, logits, -jnp.inf)
        gmax = jnp.max(gl, axis=1, keepdims=True)
        gidx = jnp.min(jnp.where(gl == gmax, lane, jnp.float32(1 << 20)), axis=1, keepdims=True)
        gate_scr[...] = _top2_gates(logits, gidx, lane)
        acc_scr[...] = jnp.zeros_like(acc_scr)

    lane0 = (N_GROUPS + N_PER_GROUP * g).astype(F32)
    acc_scr[...] += _group_experts(h_scr[...], gate_scr[...], lane, lane0, wg_ref, wu_ref, wd_ref)

    @pl.when(g == N_GROUPS - 1)
    def _():
        xn = x_ref[...] + mod_ref[:, 5 * D:6 * D] * acc_scr[...]
        if final_norm:
            xn = _rms_rows(xn) * fg_ref[...]
        o_ref[...] = xn


def _expert_specs(lp, l):
    def spec(arr):
        return pl.BlockSpec((None, N_PER_GROUP) + arr.shape[2:], lambda i, g, *_: (l, g, 0, 0))
    return [spec(lp["w_gate"]), spec(lp["w_up"]), spec(lp["w_down"])]


def _moe_dense_call(l, x2, mod, lp, final_g, *, final_norm):
    tm = x2.shape[0]
    kern = functools.partial(_moe_dense_kernel, final_norm=final_norm)
    return pl.pallas_call(
        kern,
        grid=(1, N_GROUPS),
        in_specs=[pl.BlockSpec((tm, D), lambda i, g: (0, 0)),
                  pl.BlockSpec((None, tm, 6 * D), lambda i, g: (l, 0, 0)),
                  _layer_spec(lp["g2"], l), _layer_spec(lp["w_router"], l), _layer_spec(lp["b_router"], l)]
        + _expert_specs(lp, l) + [pl.BlockSpec((1, D), lambda i, g: (0, 0))],
        out_specs=pl.BlockSpec((tm, D), lambda i, g: (0, 0)),
        out_shape=jax.ShapeDtypeStruct((tm, D), F32),
        scratch_shapes=[pltpu.VMEM((tm, D), BF16), pltpu.VMEM((tm, LANES), F32), pltpu.VMEM((tm, D), F32)],
        compiler_params=pltpu.CompilerParams(
            dimension_semantics=("arbitrary", "arbitrary"), vmem_limit_bytes=VMEM_LIMIT),
        name="moe_sample",
    )(x2, mod, lp["g2"], lp["w_router"], lp["b_router"], lp["w_gate"], lp["w_up"], lp["w_down"], final_g)


def _route_kernel(x_ref, mod_ref, g2_ref, wrt_ref, brc_ref, tri_ref, pos_ref, offs_ref, *, tiles_per_seq):
    tt = x_ref.shape[0]
    b = pl.program_id(0) // tiles_per_seq
    h = _rms_rows(x_ref[...]) * g2_ref[...]
    h = (h * (1.0 + mod_ref[pl.ds(b, 1), 4 * D:5 * D]) + mod_ref[pl.ds(b, 1), 3 * D:4 * D]).astype(BF16)
    lt = _dot_nt(wrt_ref[...], h) + brc_ref[...]
    row = lax.broadcasted_iota(jnp.int32, (SUBLANES, tt), 0).astype(F32)
    gl = jnp.where(row < N_GROUPS, lt[0:SUBLANES, :], -jnp.inf)
    gmax = jnp.max(gl, axis=0, keepdims=True)
    gidx = jnp.min(jnp.where(gl == gmax, row, jnp.float32(1 << 20)), axis=0, keepdims=True)
    onehot = jnp.where(row == gidx, 1.0, 0.0)
    rank = _dot(onehot.astype(BF16), tri_ref[...])
    counts = jnp.sum(onehot, axis=1, keepdims=True)
    row1 = lax.broadcasted_iota(jnp.int32, (SUBLANES, 1), 0)
    off = jnp.zeros((SUBLANES, 1), F32)
    for g in range(N_GROUPS - 1):
        off = off + jnp.where(row1 > g, counts[g:g + 1, :], 0.0)
    pos_ref[0] = jnp.sum(onehot * (rank + off), axis=0, keepdims=True).astype(jnp.int32)
    lane = lax.broadcasted_iota(jnp.int32, (1, LANES), 1)
    offs = jnp.where(lane == N_GROUPS, float(tt), 0.0)
    for g in range(1, N_GROUPS):
        offs = offs + jnp.where(lane == g, off[g:g + 1, :], 0.0)
    offs_ref[0] = offs.astype(jnp.int32)


def _route_call(l, x2, mod, ns, nb, lp, *, tt):
    t = x2.shape[0]
    nt = t // tt
    tiles_per_seq = nt // nb
    tri = jnp.asarray(np.triu(np.ones((tt, tt), np.float32), 1), BF16)
    kern = functools.partial(_route_kernel, tiles_per_seq=tiles_per_seq)
    return pl.pallas_call(
        kern,
        grid=(nt,),
        in_specs=[pl.BlockSpec((tt, D), lambda i: (i, 0)),
                  pl.BlockSpec((None, nb, 6 * D), lambda i: (l, ns // nb, 0), pipeline_mode=pl.Buffered(1)),
                  _layer_spec(lp["g2"], l), _layer_spec(lp["w_router_t"], l), _layer_spec(lp["b_router_c"], l),
                  pl.BlockSpec((tt, tt), lambda i: (0, 0), pipeline_mode=pl.Buffered(1))],
        out_specs=(pl.BlockSpec((1, 1, tt), lambda i: (i, 0, 0)),
                   pl.BlockSpec((1, 1, LANES), lambda i: (i, 0, 0))),
        out_shape=(jax.ShapeDtypeStruct((nt, 1, tt), jnp.int32),
                   jax.ShapeDtypeStruct((nt, 1, LANES), jnp.int32)),
        compiler_params=pltpu.CompilerParams(
            dimension_semantics=("arbitrary",), vmem_limit_bytes=VMEM_LIMIT),
        name="moe_route",
    )(x2, mod, lp["g2"], lp["w_router_t"], lp["b_router_c"], tri)


def _top2_gates_t(lt, gidx):
    rows = -(-(N_GROUPS + N_EXP) // SUBLANES) * SUBLANES
    n = lt.shape[1]
    neg = jnp.float32(-jnp.inf)
    big = jnp.float32(1 << 20)
    l = lt[0:rows, :]
    row = lax.broadcasted_iota(jnp.int32, (rows, n), 0).astype(F32)
    own = jnp.sum(jnp.where(row == gidx, l, 0.0), axis=0, keepdims=True)
    p_group = 1.0 / jnp.sum(jnp.where(row < N_GROUPS, jnp.exp(l - own), 0.0), axis=0, keepdims=True)
    e0 = N_GROUPS + N_PER_GROUP * gidx
    el = jnp.where((row >= e0) & (row < e0 + N_PER_GROUP), l, neg)
    v1 = jnp.max(el, axis=0, keepdims=True)
    i1 = jnp.min(jnp.where(el == v1, row, big), axis=0, keepdims=True)
    el2 = jnp.where(row == i1, neg, el)
    v2 = jnp.max(el2, axis=0, keepdims=True)
    i2 = jnp.min(jnp.where(el2 == v2, row, big), axis=0, keepdims=True)
    ex = jnp.exp(v2 - v1)
    w1 = 1.0 / (1.0 + ex)
    gates = jnp.where(row == i1, p_group * w1, 0.0) + jnp.where(row == i2, p_group * (ex * w1), 0.0)
    return jnp.concatenate([gates, jnp.zeros((LANES - rows, n), F32)], axis=0)


def _moe_sorted_kernel(pos_sm, offs_sm, x_ref, mod_ref, g2_ref, wrt_ref, brc_ref, wg_ref, wu_ref, wd_ref, fg_ref,
                       o_ref, a_scr, b_scr, hs_scr, gate_scr, acc_scr, *, tiles_per_seq, final_norm):
    i = pl.program_id(0)
    g = pl.program_id(1)
    tt = x_ref.shape[0]
    base = i * tt
    b = i // tiles_per_seq

    def mod(k):
        return mod_ref[pl.ds(b, 1), k * D:(k + 1) * D]

    def tile_rows(t):
        return pl.ds(pl.multiple_of(t * SUBLANES, SUBLANES), SUBLANES)

    def to_token_tiles(v):
        for s in range(SUBLANES):
            a_scr[s:tt * SUBLANES:SUBLANES, :] = v[:, s * LANES:(s + 1) * LANES]

    def from_token_tiles():
        return jnp.concatenate([b_scr[s:tt * SUBLANES:SUBLANES, :] for s in range(SUBLANES)], axis=1)

    @pl.when(g == 0)
    def _():
        h = _rms_rows(x_ref[...]) * g2_ref[...]
        to_token_tiles(h * (1.0 + mod(4)) + mod(3))

        def scatter(t, carry):
            b_scr[tile_rows(pos_sm[base + t]), :] = a_scr[tile_rows(t), :]
            return carry
        lax.fori_loop(0, tt, scatter, 0, unroll=16)
        hs = from_token_tiles().astype(BF16)
        hs_scr[0:tt, :] = hs
        hs_scr[tt:tt + MOE_BLOCK, :] = jnp.zeros((MOE_BLOCK, D), BF16)
        lt = _dot_nt(wrt_ref[...], hs) + brc_ref[...]
        slot = lax.broadcasted_iota(jnp.int32, (1, tt), 1)
        gidx = jnp.zeros((1, tt), F32)
        for gg in range(1, N_GROUPS):
            gidx = gidx + jnp.where(slot >= offs_sm[i * LANES + gg], 1.0, 0.0)
        gate_scr[0:tt, :] = _top2_gates_t(lt, gidx).T
        gate_scr[tt:tt + MOE_BLOCK, :] = jnp.zeros((MOE_BLOCK, LANES), F32)
        acc_scr[...] = jnp.zeros_like(acc_scr)

    lo = offs_sm[i * LANES + g]
    hi = offs_sm[i * LANES + g + 1]
    start = (lo // BF16_ROWS) * BF16_ROWS
    nblk = jnp.where(hi > lo, (hi - start + MOE_BLOCK - 1) // MOE_BLOCK, 0)
    lane0 = (N_GROUPS + N_PER_GROUP * g).astype(F32)
    blk_lane = lax.broadcasted_iota(jnp.int32, (MOE_BLOCK, LANES), 1).astype(F32)

    def block(k, carry):
        rs = pl.ds(pl.multiple_of(start + k * MOE_BLOCK, BF16_ROWS), MOE_BLOCK)
        acc_scr[rs, :] += _group_experts(hs_scr[rs, :], gate_scr[rs, :], blk_lane, lane0, wg_ref, wu_ref, wd_ref)
        return carry
    lax.fori_loop(0, nblk, block, 0)

    @pl.when(g == N_GROUPS - 1)
    def _():
        to_token_tiles(acc_scr[0:tt, :])

        def gather(t, carry):
            b_scr[tile_rows(t), :] = a_scr[tile_rows(pos_sm[base + t]), :]
            return carry
        lax.fori_loop(0, tt, gather, 0, unroll=16)
        xn = x_ref[...] + mod(5) * from_token_tiles()
        if final_norm:
            xn = _rms_rows(xn) * fg_ref[...]
        o_ref[...] = xn


def _moe_sorted_call(l, x2, mod, ns, nb, lp, final_g, pos, offs, *, tt, final_norm):
    t = x2.shape[0]
    nt = t // tt
    kern = functools.partial(_moe_sorted_kernel, tiles_per_seq=nt // nb, final_norm=final_norm)
    grid_spec = pltpu.PrefetchScalarGridSpec(
        num_scalar_prefetch=2,
        grid=(nt, N_GROUPS),
        in_specs=[pl.BlockSpec((tt, D), lambda i, g, *_: (i, 0)),
                  pl.BlockSpec((None, nb, 6 * D), lambda i, g, *_: (l, ns // nb, 0), pipeline_mode=pl.Buffered(1)),
                  _layer_spec(lp["g2"], l), _layer_spec(lp["w_router_t"], l), _layer_spec(lp["b_router_c"], l)]
        + _expert_specs(lp, l) + [pl.BlockSpec((1, D), lambda i, g, *_: (0, 0))],
        out_specs=pl.BlockSpec((tt, D), lambda i, g, *_: (i, 0)),
        scratch_shapes=[pltpu.VMEM((tt * SUBLANES, LANES), F32), pltpu.VMEM((tt * SUBLANES, LANES), F32),
                        pltpu.VMEM((tt + MOE_BLOCK, D), BF16), pltpu.VMEM((tt + MOE_BLOCK, LANES), F32),
                        pltpu.VMEM((tt + MOE_BLOCK, D), F32)])
    return pl.pallas_call(
        kern,
        grid_spec=grid_spec,
        out_shape=jax.ShapeDtypeStruct((t, D), F32),
        compiler_params=pltpu.CompilerParams(
            dimension_semantics=("arbitrary", "arbitrary"), vmem_limit_bytes=VMEM_LIMIT),
        name="moe_prompt",
    )(pos, offs, x2, mod, lp["g2"], lp["w_router_t"], lp["b_router_c"], lp["w_gate"], lp["w_up"], lp["w_down"],
      final_g)


def _pad_to(a, axis, size):
    pad = [(0, 0)] * a.ndim
    pad[axis] = (0, size - a.shape[axis])
    return jnp.pad(a, pad)


def _prepare_params(p):
    lb_cum = jnp.cumsum(jax.nn.softmax(p["hgrn_lb_logits"].astype(F32), axis=0), axis=0)
    lb = lb_cum - lb_cum[:1]
    lbv = _pad_to(jnp.stack([jnp.log(lb), jnp.log1p(-lb), 1.0 - lb], axis=1), 1, SUBLANES)
    a_re, a_im = p["s5_a_re"].astype(F32), p["s5_a_im"].astype(F32)
    dt = jnp.exp(p["s5_log_dt"].astype(F32))[..., None]
    mag = jnp.exp(a_re * dt)
    abar_re = mag * jnp.cos(a_im * dt)
    abar_im = mag * jnp.sin(a_im * dt)
    den = a_re * a_re + a_im * a_im
    nr, ni = abar_re - 1.0, abar_im
    coef_re = ((nr * a_re + ni * a_im) / den)[..., None]
    coef_im = ((ni * a_re - nr * a_im) / den)[..., None]
    b_re, b_im = p["s5_b_re"].astype(F32), p["s5_b_im"].astype(F32)
    bb_re = coef_re * b_re - coef_im * b_im
    bb_im = coef_re * b_im + coef_im * b_re
    eye = jnp.eye(C_G, dtype=F32)
    blk_b = lambda m: jnp.einsum("lgph,gk->lghkp", m, eye).reshape(DEPTH, C_W, C_S)
    blk_bt = lambda m: jnp.einsum("lgph,gk->lgpkh", m, eye).reshape(DEPTH, C_S, C_W)
    blk_c = lambda m: jnp.einsum("lghp,gk->lgpkh", m, eye).reshape(DEPTH, C_S, C_W)
    bb = jnp.concatenate([blk_b(bb_re), blk_b(bb_im)], axis=2).astype(BF16)
    bb_t = jnp.concatenate([blk_bt(bb_re), blk_bt(bb_im)], axis=1).astype(BF16)
    cc = jnp.concatenate([blk_c(p["s5_c_re"].astype(F32)), -blk_c(p["s5_c_im"].astype(F32))],
                         axis=1).astype(BF16)
    abar_rows = jnp.stack([abar_re.reshape(DEPTH, C_S), abar_im.reshape(DEPTH, C_S)], axis=1)
    router = jnp.concatenate([p["moe_w_group"], p["moe_w_expert"]], axis=2)
    b_router = _pad_to(jnp.concatenate([p["moe_b_group"], p["moe_b_expert"]], axis=1), 1, LANES)
    row = lambda a: a.astype(F32)[:, None, :]
    col = lambda a: a.astype(F32)[:, :, None]
    return dict(
        g1=row(p["norm1_g"]), g2=row(p["norm2_g"]),
        w_in=jnp.transpose(p["w_in"], (0, 2, 1)).astype(BF16), lbv=lbv,
        w_alpha=_pad_to(_pad_to(p["gla_w_alpha"], 1, LANES), 2, 256).astype(BF16),
        b_alpha=_pad_to(row(p["gla_b_alpha"]), 2, 256),
        hgrn_g=row(jnp.tile(p["hgrn_norm_g"], (1, A_H))), gla_g=row(jnp.tile(p["gla_norm_g"], (1, B_H))),
        hgrn_gc=col(p["hgrn_norm_g"]), gla_gc=col(p["gla_norm_g"]),
        abar=_pad_to(abar_rows, 1, SUBLANES), abar_c=jnp.transpose(abar_rows, (0, 2, 1)),
        bb=bb, bb_t=bb_t, cc=cc, d_skip=row(p["s5_d"]),
        w_glu=p["s5_w_glu"].astype(BF16), b_glu=row(p["s5_b_glu"]), w_out=p["w_out"].astype(BF16),
        w_gate=p["moe_w_gate"].astype(BF16), w_up=p["moe_w_up"].astype(BF16),
        w_down=p["moe_w_down"].astype(BF16),
        w_router=_pad_to(router, 2, LANES).astype(BF16), b_router=b_router[:, None, :],
        w_router_t=_pad_to(jnp.transpose(router, (0, 2, 1)), 1, LANES).astype(BF16),
        b_router_c=b_router[:, :, None])


def kernel(x_prompt, x_sample, state_hgrn, state_gla, state_s5_re, state_s5_im, c_prompt, c_sample, ada_w, ada_b, norm1_g, norm2_g, w_in, hgrn_lb_logits, hgrn_norm_g, gla_w_alpha, gla_b_alpha, gla_norm_g, s5_a_re, s5_a_im, s5_b_re, s5_b_im, s5_c_re, s5_c_im, s5_d, s5_log_dt, s5_w_glu, s5_b_glu, w_out, moe_w_group, moe_b_group, moe_w_expert, moe_b_expert, moe_w_gate, moe_w_up, moe_w_down, final_norm_g):
    lp = _prepare_params(dict(
        norm1_g=norm1_g, norm2_g=norm2_g, w_in=w_in, hgrn_lb_logits=hgrn_lb_logits, hgrn_norm_g=hgrn_norm_g,
        gla_w_alpha=gla_w_alpha, gla_b_alpha=gla_b_alpha, gla_norm_g=gla_norm_g, s5_a_re=s5_a_re,
        s5_a_im=s5_a_im, s5_b_re=s5_b_re, s5_b_im=s5_b_im, s5_c_re=s5_c_re, s5_c_im=s5_c_im, s5_d=s5_d,
        s5_log_dt=s5_log_dt, s5_w_glu=s5_w_glu, s5_b_glu=s5_b_glu, w_out=w_out, moe_w_group=moe_w_group,
        moe_b_group=moe_b_group, moe_w_expert=moe_w_expert, moe_b_expert=moe_b_expert,
        moe_w_gate=moe_w_gate, moe_w_up=moe_w_up, moe_w_down=moe_w_down))
    nb, seq, _ = x_prompt.shape
    ns = x_sample.shape[0]
    assert ns % nb == 0 and seq % CHUNK == 0
    tt = min(MOE_TILE, seq)
    assert seq % tt == 0 and tt % BF16_ROWS == 0
    final_g = final_norm_g.reshape(1, D)

    mod = _ada_call(jnp.concatenate([c_sample, c_prompt], axis=0), ada_w, ada_b)
    states = (jnp.transpose(state_hgrn, (0, 2, 3, 4, 1)), jnp.transpose(state_gla, (0, 2, 3, 4, 1)),
              jnp.transpose(state_s5_re, (0, 2, 3, 1)).reshape(DEPTH, C_S, ns),
              jnp.transpose(state_s5_im, (0, 2, 3, 1)).reshape(DEPTH, C_S, ns))
    new_states = tuple(jnp.zeros_like(s) for s in states)

    xp = x_prompt
    xs = x_sample.reshape(ns, D)
    outs = {k: [] for k in ("hg_p", "gl_p", "sr_p", "si_p")}
    for l in range(DEPTH):
        last = l == DEPTH - 1
        xp, sta, stb, s5r, s5i = _mixer_prompt_call(l, xp, mod, ns, lp)
        x2 = xp.reshape(nb * seq, D)
        pos, offs = _route_call(l, x2, mod, ns, nb, lp, tt=tt)
        xp = _moe_sorted_call(l, x2, mod, ns, nb, lp, final_g, pos.reshape(-1), offs.reshape(-1),
                              tt=tt, final_norm=last).reshape(nb, seq, D)
        sta = sta.reshape(nb, A_H // 2, 2, A_DV, 2, A_DK)
        outs["hg_p"].append(jnp.stack([sta[:, h // 2, h % 2, :, h % 2, :] for h in range(A_H)], axis=1)
                            .transpose(0, 1, 3, 2))
        stb = stb.reshape(nb, B_H, B_DV, 4, B_DK)
        outs["gl_p"].append(jnp.stack([stb[:, h, :, h % 4, :] for h in range(B_H)], axis=1)
                            .transpose(0, 1, 3, 2))
        outs["sr_p"].append(s5r.reshape(nb, C_G, C_P))
        outs["si_p"].append(s5i.reshape(nb, C_G, C_P))
        xs, *new_states = _mixer_sample_call(l, xs, mod, lp, states, new_states)
        xs = _moe_dense_call(l, xs, mod, lp, final_g, final_norm=last)

    hg_s, gl_s, sr_s, si_s = new_states
    return (xp, xs.reshape(ns, 1, D),
            jnp.stack(outs["hg_p"]), jnp.stack(outs["gl_p"]), jnp.stack(outs["sr_p"]), jnp.stack(outs["si_p"]),
            jnp.transpose(hg_s, (0, 4, 1, 2, 3)), jnp.transpose(gl_s, (0, 4, 1, 2, 3)),
            jnp.transpose(sr_s.reshape(DEPTH, C_G, C_P, ns), (0, 3, 1, 2)),
            jnp.transpose(si_s.reshape(DEPTH, C_G, C_P, ns), (0, 3, 1, 2)))
```

```python
import functools
import math

import numpy as np
import jax
import jax.numpy as jnp
from jax import lax
from jax.experimental import pallas as pl
from jax.experimental.pallas import tpu as pltpu

F32 = jnp.float32
BF16 = jnp.bfloat16

D = 1024
DEPTH = 4
A_H, A_DK, A_DV = 6, 64, 64
B_H, B_DK, B_DV = 6, 32, 64
B_RANK = 16
B_TAU = 16.0
C_G, C_CH, C_P = 16, 16, 64
C_W = C_G * C_CH
C_S = C_G * C_P
N_GROUPS, N_PER_GROUP, D_EXP = 4, 4, 256
N_EXP = N_GROUPS * N_PER_GROUP
EPS = 1e-6
CHUNK = 64
LANES = 128
SUBLANES = 8
EXP_RANGE = 80.0
MOE_TILE = 1024
MOE_BLOCK = 320
BF16_ROWS = 16

W_AQ, W_AF, W_AI, W_AG = 0, 384, 768, 1152
W_BQ, W_BK, W_BV, W_LR, W_BR, W_CU = 1536, 1728, 1920, 2304, 2320, 2704
IN_COLS = 2960
W_SEGMENTS = {"aq|af": (W_AQ, 768), "ai|ag": (W_AI, 768), "bq": (W_BQ, 256), "bk": (W_BK, 256),
              "bv|lr": (W_BV, 512), "br|cu": (W_BR, 640)}
QA, FA, VA, GA = 0, 384, 768, 1152
QB, KB, VB, RB = 1536, 1792, 2048, 2432
CU, LB, KA = 2816, 3072, 3328
PW = 3712
N_LG = 5
LG_W = N_LG * LANES
BLOCKS = tuple((QA + LANES * p, KA + LANES * p, FA + LANES * p, LANES * p, 2, A_DK) for p in range(3)) + (
    (QB, KB, LB, 384, 4, B_DK), (QB + LANES, KB + LANES, LB + LANES, 640, 2, B_DK))
VO_W = 768

VMEM_LIMIT = 56 * 1024 * 1024


def _sigmoid(x):
    return 1.0 / (1.0 + jnp.exp(-x))


def _silu(x):
    return x * _sigmoid(x)


def _log_sigmoid(x):
    return jnp.minimum(x, 0.0) - jnp.log(1.0 + jnp.exp(-jnp.abs(x)))


def _gelu_tanh(x):
    return 0.5 * x * (1.0 + jnp.tanh(math.sqrt(2.0 / math.pi) * (x + 0.044715 * x * x * x)))


def _split3(x):
    hi = x.astype(BF16)
    r = x - hi.astype(F32)
    mid = r.astype(BF16)
    lo = (r - mid.astype(F32)).astype(BF16)
    return hi, mid, lo


def _dot(a, b):
    return jnp.dot(a, b, preferred_element_type=F32)


def _dot_nt(a, b):
    return lax.dot_general(a, b, (((1,), (1,)), ((), ())), preferred_element_type=F32)


def _dot_tn(a, b):
    return lax.dot_general(a, b, (((0,), (0,)), ((), ())), preferred_element_type=F32)


def _dot_exact_01(x, m01):
    hi, mid, lo = _split3(x)
    return _dot(hi, m01) + _dot(mid, m01) + _dot(lo, m01)


def _rms_rows(x):
    return x * lax.rsqrt(jnp.mean(x * x, axis=-1, keepdims=True) + EPS)


def _forget_gate(af, lbv):
    e = jnp.exp(-jnp.abs(af))
    log_sig = jnp.minimum(af, 0.0) - jnp.log(1.0 + e)
    a = lbv[0:1]
    b = lbv[1:2] + log_sig
    log_f = jnp.maximum(a, b) + jnp.log(1.0 + jnp.exp(-jnp.abs(a - b)))
    k = lbv[2:3] * (jnp.where(af >= 0.0, e, 1.0) / (1.0 + e))
    return log_f, k


def _chunk_cumsum(x, c):
    pos = lax.broadcasted_iota(jnp.int32, x.shape, 0) % c
    s = 1
    while s < c:
        x = x + jnp.where(pos >= s, pltpu.roll(x, s, 0), 0.0)
        s *= 2
    return x


def _head_rms(o, dv):
    w = o.shape[1]
    r = lax.broadcasted_iota(jnp.int32, (w, w), 0)
    cidx = lax.broadcasted_iota(jnp.int32, (w, w), 1)
    ones_blk = jnp.where(r // dv == cidx // dv, 1.0, 0.0).astype(BF16)
    return o * lax.rsqrt(_dot((o * o).astype(BF16), ones_blk) * (1.0 / dv) + EPS)


def _project_all(h, wint_ref, lbv_ref, walpha_ref, balpha_ref, chunk, p_scr):
    def proj(name):
        r0, n = W_SEGMENTS[name]
        return _dot_nt(h, wint_ref[r0:r0 + n, :])

    def cum(x):
        return x if chunk == 1 else _chunk_cumsum(x, chunk)

    aqf = proj("aq|af")
    p_scr[:, QA:QA + 384] = _silu(aqf[:, 0:384])
    log_f, ka = _forget_gate(aqf[:, 384:768], lbv_ref[...])
    p_scr[:, FA:FA + 384] = cum(log_f)
    p_scr[:, KA:KA + 384] = ka
    aig = proj("ai|ag")
    p_scr[:, VA:VA + 384] = aig[:, 0:384]
    p_scr[:, GA:GA + 384] = _silu(aig[:, 384:768])
    p_scr[:, QB:QB + 256] = proj("bq") * (B_DK ** -0.5)
    kb = proj("bk")
    p_scr[:, KB:KB + 256] = jnp.where(lax.broadcasted_iota(jnp.int32, kb.shape, 1) < B_H * B_DK, kb, 0.0)
    bvlr = proj("bv|lr")
    p_scr[:, VB:VB + 384] = bvlr[:, 0:384]
    z = _dot(bvlr[:, 384:512].astype(BF16), walpha_ref[...]) + balpha_ref[...]
    p_scr[:, LB:LB + 256] = cum(_log_sigmoid(z) * (1.0 / B_TAU))
    brcu = proj("br|cu")
    p_scr[:, RB:RB + 384] = _silu(brcu[:, 0:384])
    p_scr[:, CU:CU + 256] = brcu[:, 384:640]


def _layer_spec(arr, l):
    nd = arr.ndim - 1
    return pl.BlockSpec((None,) + arr.shape[1:], lambda *_, _n=nd: (l,) + (0,) * _n,
                        pipeline_mode=pl.Buffered(1))


def _ada_kernel(c_ref, w_ref, b_ref, o_ref):
    sc = _silu(c_ref[...]).astype(BF16)
    half = w_ref.shape[2] // 2
    for n0 in (0, half):
        o_ref[0, :, n0:n0 + half] = _dot(sc, w_ref[0, :, n0:n0 + half].astype(BF16)) + b_ref[0, :, n0:n0 + half]


def _ada_call(c_all, ada_w, ada_b):
    n = c_all.shape[0]
    tn = 1536
    return pl.pallas_call(
        _ada_kernel,
        grid=(DEPTH, 6 * D // tn),
        in_specs=[
            pl.BlockSpec((n, D), lambda l, j: (0, 0)),
            pl.BlockSpec((1, D, tn), lambda l, j: (l, 0, j)),
            pl.BlockSpec((1, 1, tn), lambda l, j: (l, 0, j)),
        ],
        out_specs=pl.BlockSpec((1, n, tn), lambda l, j: (l, 0, j)),
        out_shape=jax.ShapeDtypeStruct((DEPTH, n, 6 * D), F32),
        compiler_params=pltpu.CompilerParams(
            dimension_semantics=("arbitrary", "arbitrary"), vmem_limit_bytes=VMEM_LIMIT),
        name="ada_mod",
    )(c_all, ada_w, ada_b.reshape(DEPTH, 1, 6 * D))


def _mixer_prompt_kernel(x_ref, mod_ref, g1_ref, win_ref, lbv_ref, walpha_ref, balpha_ref,
                         hg_ref, gg_ref, abar_ref, bb_ref, cc_ref, dsk_ref, wglu_ref, bglu_ref,
                         wout_ref,
                         xo_ref, sta_ref, stb_ref, s5r_ref, s5i_ref,
                         h_scr, p_scr, o_scr, xr_scr, xi_scr, t_scr,
                         qt_scr, kt_scr, qd_scr, kd_scr, v_scr, of_scr, dec_scr, sc_scr):
    nb = x_ref.shape[0]
    c = CHUNK
    i = pl.program_id(0)

    @pl.when(i == 0)
    def _():
        sta_ref[...] = jnp.zeros_like(sta_ref)
        stb_ref[...] = jnp.zeros_like(stb_ref)
        s5r_ref[...] = jnp.zeros_like(s5r_ref)
        s5i_ref[...] = jnp.zeros_like(s5i_ref)

    for b in range(nb):
        hb = _rms_rows(x_ref[b]) * g1_ref[...]
        hb = hb * (1.0 + mod_ref[b:b + 1, D:2 * D]) + mod_ref[b:b + 1, 0:D]
        h_scr[b * c:(b + 1) * c, :] = hb.astype(BF16)

    _project_all(h_scr[...], win_ref, lbv_ref, walpha_ref, balpha_ref, c, p_scr)

    v_scr[:, 0:384] = p_scr[:, VA:VA + 384].astype(BF16)
    v_scr[:, 384:768] = p_scr[:, VB:VB + 384].astype(BF16)
    rng = jnp.zeros((1, LANES), F32)
    for b in range(nb):
        rs = slice(b * c, (b + 1) * c)
        for g, (qc, kc, bc, _, _, _) in enumerate(BLOCKS):
            ls = slice(g * LANES, (g + 1) * LANES)
            q = p_scr[rs, qc:qc + LANES]
            k = p_scr[rs, kc:kc + LANES]
            bcum = p_scr[rs, bc:bc + LANES]
            b_mid = bcum[c // 2 - 1:c // 2, :]
            b_end = bcum[c - 1:c, :]
            d = bcum - b_mid
            rng = jnp.maximum(rng, jnp.maximum(jnp.abs(d[0:1, :]), jnp.abs(d[c - 1:c, :])))
            qt_scr[rs, ls] = (q * jnp.exp(d)).astype(BF16)
            kt_scr[rs, ls] = (k * jnp.exp(-d)).astype(BF16)
            qd_scr[rs, ls] = (q * jnp.exp(bcum)).astype(BF16)
            kd_scr[rs, ls] = (k * jnp.exp(b_end - bcum)).astype(BF16)
            dec_scr[b, :, ls] = jnp.broadcast_to(jnp.exp(b_end), (SUBLANES, LANES))
    halves = C_W // LANES
    for b in range(nb):
        for j in range(halves):
            t_scr[j, b:b + nb * c:nb, :] = p_scr[b * c:(b + 1) * c, CU + LANES * j:CU + LANES * (j + 1)]
    u_bf = jnp.concatenate([t_scr[j] for j in range(halves)], axis=1).astype(BF16)
    xr_scr[...] = _dot(u_bf, bb_ref[:, 0:C_S])
    xi_scr[...] = _dot(u_bf, bb_ref[:, C_S:2 * C_S])

    in_range = jnp.max(rng) <= EXP_RANGE

    lane = lax.broadcasted_iota(jnp.int32, (c, LANES), 1)

    def scores_direct(r0, qc, kc, bc, nh, dk):
        q = p_scr[pl.ds(r0, c), qc:qc + LANES]
        bcum = p_scr[pl.ds(r0, c), bc:bc + LANES]
        krow = lax.broadcasted_iota(jnp.int32, (LANES, nh * c), 0)
        kcol = lax.broadcasted_iota(jnp.int32, (LANES, nh * c), 1)
        sub = lax.broadcasted_iota(jnp.int32, (SUBLANES, LANES), 0)

        def row_of(col0, s):
            r8 = pl.ds(pl.multiple_of(r0 + (s // SUBLANES) * SUBLANES, SUBLANES), SUBLANES)
            return jnp.sum(jnp.where(sub == s % SUBLANES, p_scr[r8, col0:col0 + LANES], 0.0),
                           axis=0, keepdims=True)

        def body(s, sc):
            ks = row_of(kc, s)
            bs = row_of(bc, s)
            w = q * ks * jnp.exp(jnp.minimum(bcum - bs, 0.0))
            place = jnp.where(kcol == (krow // dk) * c + s, 1.0, 0.0).astype(BF16)
            return sc + _dot_exact_01(w, place)

        return lax.fori_loop(0, c, body, jnp.zeros((c, nh * c), F32))

    def stage_scores(b, r0, direct):
        rs = pl.ds(r0, c)
        for g, (qc, kc, bc, vc, nh, dk) in enumerate(BLOCKS):
            ls = slice(g * LANES, (g + 1) * LANES)
            if direct:
                sc = scores_direct(r0, qc, kc, bc, nh, dk)
            else:
                kt = kt_scr[rs, ls]
                kcat = jnp.concatenate(
                    [jnp.where(lane // dk == h, kt, jnp.zeros_like(kt)) for h in range(nh)], axis=0)
                sc = _dot_nt(qt_scr[rs, ls], kcat)
            row = lax.broadcasted_iota(jnp.int32, (c, nh * c), 0)
            col = lax.broadcasted_iota(jnp.int32, (c, nh * c), 1)
            sc_scr[rs, vc:vc + nh * c] = jnp.where(col % c <= row, sc, 0.0).astype(BF16)

    def stage_readout(b, r0):
        rs = pl.ds(r0, c)
        for g, (_, _, _, vc, nh, dk) in enumerate(BLOCKS):
            ls = slice(g * LANES, (g + 1) * LANES)
            vw = nh * 64
            st_ref = sta_ref if g < 3 else stb_ref
            v0 = vc if g < 3 else vc - 384
            v = v_scr[rs, vc:vc + vw]
            vlane = lax.broadcasted_iota(jnp.int32, (c, vw), 1)
            vcat = jnp.concatenate(
                [jnp.where(vlane // 64 == h, v, jnp.zeros_like(v)) for h in range(nh)], axis=0)
            st = st_ref[b, v0:v0 + vw, :]
            of_scr[rs, vc:vc + vw] = (_dot(sc_scr[rs, vc:vc + nh * c], vcat)
                                      + _dot_nt(qd_scr[rs, ls], st.astype(BF16)))

    def stage_update(b, r0):
        rs = pl.ds(r0, c)
        for g, (_, _, _, vc, nh, dk) in enumerate(BLOCKS):
            ls = slice(g * LANES, (g + 1) * LANES)
            vw = nh * 64
            st_ref = sta_ref if g < 3 else stb_ref
            v0 = vc if g < 3 else vc - 384
            upd = _dot_tn(v_scr[rs, vc:vc + vw], kd_scr[rs, ls])
            srow = lax.broadcasted_iota(jnp.int32, (vw, LANES), 0)
            scol = lax.broadcasted_iota(jnp.int32, (vw, LANES), 1)
            st_ref[b, v0:v0 + vw, :] = (st_ref[b, v0:v0 + vw, :] * dec_scr[b, 0:1, ls]
                                        + jnp.where(srow // 64 == scol // dk, upd, 0.0))

    @pl.when(in_range)
    def _():
        for b in range(nb):
            stage_scores(b, b * c, False)
        for b in range(nb):
            stage_readout(b, b * c)
        for b in range(nb):
            stage_update(b, b * c)

    @pl.when(jnp.logical_not(in_range))
    def _():
        def body(b, carry):
            r0 = pl.multiple_of(b * c, c)
            stage_scores(b, r0, True)
            stage_readout(b, r0)
            stage_update(b, r0)
            return carry
        lax.fori_loop(0, nb, body, 0)

    oa = _head_rms(of_scr[:, 0:384], A_DV) * hg_ref[...] * p_scr[:, GA:GA + 384]
    o_scr[:, 0:384] = oa.astype(BF16)
    ob = _head_rms(of_scr[:, 384:768], B_DV) * gg_ref[...] * p_scr[:, RB:RB + 384]
    o_scr[:, 384:768] = ob.astype(BF16)

    ar = abar_ref[0:1, :]
    ai = abar_ref[1:2, :]

    def scan_t(t, carry):
        sr, si = carry
        ts = pl.ds(pl.multiple_of(t * nb, nb), nb)
        nr = ar * sr - ai * si + xr_scr[ts, :]
        ni = ar * si + ai * sr + xi_scr[ts, :]
        xr_scr[ts, :] = nr
        xi_scr[ts, :] = ni
        return nr, ni

    sr, si = lax.fori_loop(0, c, scan_t, (s5r_ref[...], s5i_ref[...]), unroll=2)
    s5r_ref[...] = sr
    s5i_ref[...] = si
    u = jnp.concatenate([t_scr[j] for j in range(halves)], axis=1)
    half = nb * c // 2
    y = jnp.concatenate(
        [_dot(xr_scr[r:r + half, :].astype(BF16), cc_ref[0:C_S, :])
         + _dot(xi_scr[r:r + half, :].astype(BF16), cc_ref[C_S:2 * C_S, :]) for r in (0, half)], axis=0)
    y = y + dsk_ref[...] * u
    z = _gelu_tanh(y)
    oc = z * _sigmoid(_dot(z.astype(BF16), wglu_ref[...]) + bglu_ref[...])
    for j in range(halves):
        t_scr[j] = oc[:, LANES * j:LANES * (j + 1)]
    for b in range(nb):
        for j in range(halves):
            o_scr[b * c:(b + 1) * c, 768 + LANES * j:768 + LANES * (j + 1)] = (
                t_scr[j, b:b + nb * c:nb, :].astype(BF16))

    o_all = o_scr[...]
    for n0 in (0, D // 2):
        out = _dot(o_all, wout_ref[:, n0:n0 + D // 2])
        for b in range(nb):
            xo_ref[b, :, n0:n0 + D // 2] = (x_ref[b, :, n0:n0 + D // 2]
                                            + mod_ref[b:b + 1, 2 * D + n0:2 * D + n0 + D // 2]
                                            * out[b * c:(b + 1) * c, :])


def _mixer_prompt_call(l, x, mod, ns, lp):
    nb, seq, _ = x.shape
    c = CHUNK
    rows = nb * c
    consts = [lp[k] for k in ("g1", "w_in", "lbv", "w_alpha", "b_alpha", "hgrn_g", "gla_g", "abar", "bb", "cc",
                              "d_skip", "w_glu", "b_glu", "w_out")]
    in_specs = ([pl.BlockSpec((nb, c, D), lambda i: (0, i, 0)),
                 pl.BlockSpec((None, nb, 6 * D), lambda i: (l, ns // nb, 0), pipeline_mode=pl.Buffered(1))]
                + [_layer_spec(a, l) for a in consts])
    out_shape = (jax.ShapeDtypeStruct((nb, seq, D), F32),
                 jax.ShapeDtypeStruct((nb, 384, LANES), F32),
                 jax.ShapeDtypeStruct((nb, 384, LANES), F32),
                 jax.ShapeDtypeStruct((nb, C_S), F32),
                 jax.ShapeDtypeStruct((nb, C_S), F32))
    out_specs = (pl.BlockSpec((nb, c, D), lambda i: (0, i, 0)),
                 pl.BlockSpec((nb, 384, LANES), lambda i: (0, 0, 0)),
                 pl.BlockSpec((nb, 384, LANES), lambda i: (0, 0, 0)),
                 pl.BlockSpec((nb, C_S), lambda i: (0, 0)),
                 pl.BlockSpec((nb, C_S), lambda i: (0, 0)))
    scratch = [pltpu.VMEM((rows, D), BF16), pltpu.VMEM((rows, PW), F32), pltpu.VMEM((rows, D), BF16),
               pltpu.VMEM((rows, C_S), F32), pltpu.VMEM((rows, C_S), F32),
               pltpu.VMEM((C_W // LANES, rows, LANES), F32),
               pltpu.VMEM((rows, LG_W), BF16), pltpu.VMEM((rows, LG_W), BF16),
               pltpu.VMEM((rows, LG_W), BF16), pltpu.VMEM((rows, LG_W), BF16),
               pltpu.VMEM((rows, VO_W), BF16), pltpu.VMEM((rows, VO_W), F32),
               pltpu.VMEM((nb, SUBLANES, LG_W), F32), pltpu.VMEM((rows, VO_W), BF16)]
    return pl.pallas_call(
        _mixer_prompt_kernel,
        grid=(seq // c,),
        in_specs=in_specs,
        out_specs=out_specs,
        out_shape=out_shape,
        scratch_shapes=scratch,
        compiler_params=pltpu.CompilerParams(
            dimension_semantics=("arbitrary",), vmem_limit_bytes=VMEM_LIMIT),
        name="mixer_prompt",
    )(x, mod, *consts)


def _step_head(s_ref, so_ref, t_scr, i, v_t):
    acc = jnp.zeros(v_t.shape, F32)
    for k in range(s_ref.shape[0]):
        new = s_ref[k] * t_scr[1, i, k:k + 1, :] + t_scr[2, i, k:k + 1, :] * v_t
        so_ref[k] = new
        acc = acc + new * t_scr[0, i, k:k + 1, :]
    return acc


def _mixer_sample_kernel(x_ref, mod_ref, g1_ref, win_ref, lbv_ref, walpha_ref, balpha_ref,
                         hgc_ref, ggc_ref, abarc_ref, bbt_ref, cc_ref, dsk_ref, wglu_ref, bglu_ref,
                         wout_ref, sa_ref, sb_ref, s5r_ref, s5i_ref, pa_ref, pb_ref, pr_ref, pi_ref,
                         xo_ref, sao_ref, sbo_ref, s5ro_ref, s5io_ref,
                         p_scr, at_scr, bt_scr, vt_scr, gate_scr, oh_scr, oc_scr):
    del pa_ref, pb_ref, pr_ref, pi_ref
    i = pl.program_id(0)

    @pl.when(i == 0)
    def _():
        h = _rms_rows(x_ref[...]) * g1_ref[...]
        h = (h * (1.0 + mod_ref[:, D:2 * D]) + mod_ref[:, 0:D]).astype(BF16)
        _project_all(h, win_ref, lbv_ref, walpha_ref, balpha_ref, 1, p_scr)
        for g, (qc, kc, bc, _, nh, dk) in enumerate(BLOCKS):
            t_scr, h0 = (at_scr, 2 * g) if g < 3 else (bt_scr, 4 * (g - 3))
            qt = p_scr[:, qc:qc + LANES].T
            dt = jnp.exp(p_scr[:, bc:bc + LANES]).T
            kt = p_scr[:, kc:kc + LANES].T
            for hh in range(nh):
                t_scr[0, h0 + hh] = qt[hh * dk:(hh + 1) * dk, :]
                t_scr[1, h0 + hh] = dt[hh * dk:(hh + 1) * dk, :]
                t_scr[2, h0 + hh] = kt[hh * dk:(hh + 1) * dk, :]
        for p in range(A_H // 2):
            for grp, vcol, gcol in ((0, VA, GA), (1, VB, RB)):
                vt = p_scr[:, vcol + LANES * p:vcol + LANES * (p + 1)].T
                gt = p_scr[:, gcol + LANES * p:gcol + LANES * (p + 1)].T
                for hh in range(2):
                    vt_scr[grp, 2 * p + hh] = vt[64 * hh:64 * (hh + 1), :]
                    gate_scr[grp, 2 * p + hh] = gt[64 * hh:64 * (hh + 1), :]
        u = p_scr[:, CU:CU + C_W]
        xri = _dot_nt(bbt_ref[...], u.astype(BF16))
        ar = abarc_ref[:, 0:1]
        ai = abarc_ref[:, 1:2]
        s0r = s5r_ref[...]
        s0i = s5i_ref[...]
        sr = ar * s0r - ai * s0i + xri[0:C_S, :]
        si = ar * s0i + ai * s0r + xri[C_S:2 * C_S, :]
        s5ro_ref[...] = sr
        s5io_ref[...] = si
        y = (_dot_tn(sr.astype(BF16), cc_ref[0:C_S, :]) + _dot_tn(si.astype(BF16), cc_ref[C_S:2 * C_S, :])
             + dsk_ref[...] * u)
        z = _gelu_tanh(y)
        oc_scr[...] = z * _sigmoid(_dot(z.astype(BF16), wglu_ref[...]) + bglu_ref[...])

    def norm_gate(o, gain_col, gate):
        return o * lax.rsqrt(jnp.mean(o * o, axis=0, keepdims=True) + EPS) * gain_col * gate

    oa = _step_head(sa_ref, sao_ref, at_scr, i, vt_scr[0, i])
    oh_scr[0, i] = norm_gate(oa, hgc_ref[...], gate_scr[0, i])
    ob = _step_head(sb_ref, sbo_ref, bt_scr, i, vt_scr[1, i])
    oh_scr[1, i] = norm_gate(ob, ggc_ref[...], gate_scr[1, i])

    @pl.when(i == A_H - 1)
    def _():
        parts = []
        for grp in range(2):
            for p in range(A_H // 2):
                two = jnp.concatenate([oh_scr[grp, 2 * p], oh_scr[grp, 2 * p + 1]], axis=0)
                parts.append(two.T)
        o = jnp.concatenate(parts + [oc_scr[...]], axis=1).astype(BF16)
        xo_ref[...] = x_ref[...] + mod_ref[:, 2 * D:3 * D] * _dot(o, wout_ref[...])


def _mixer_sample_call(l, x, mod, lp, states, prev):
    n = x.shape[0]
    consts = [lp[k] for k in ("g1", "w_in", "lbv", "w_alpha", "b_alpha", "hgrn_gc", "gla_gc", "abar_c", "bb_t",
                              "cc", "d_skip", "w_glu", "b_glu", "w_out")]
    state_specs = [pl.BlockSpec((None, None, A_DK, A_DV, n), lambda i: (l, i, 0, 0, 0)),
                   pl.BlockSpec((None, None, B_DK, B_DV, n), lambda i: (l, i, 0, 0, 0)),
                   pl.BlockSpec((None, C_S, n), lambda i: (l, 0, 0)),
                   pl.BlockSpec((None, C_S, n), lambda i: (l, 0, 0))]
    in_specs = ([pl.BlockSpec((n, D), lambda i: (0, 0), pipeline_mode=pl.Buffered(1)),
                 pl.BlockSpec((None, n, 6 * D), lambda i: (l, 0, 0), pipeline_mode=pl.Buffered(1))]
                + [_layer_spec(a, l) for a in consts] + state_specs
                + [pl.BlockSpec(memory_space=pl.ANY)] * 4)
    n_in = 2 + len(consts) + 4
    out_shape = (jax.ShapeDtypeStruct((n, D), F32),) + tuple(jax.ShapeDtypeStruct(s.shape, F32) for s in states)
    out_specs = (pl.BlockSpec((n, D), lambda i: (0, 0)),) + tuple(state_specs)
    scratch = [pltpu.VMEM((n, PW), F32), pltpu.VMEM((3, A_H, A_DK, n), F32), pltpu.VMEM((3, 8, B_DK, n), F32),
               pltpu.VMEM((2, A_H, 64, n), F32), pltpu.VMEM((2, A_H, 64, n), F32),
               pltpu.VMEM((2, A_H, 64, n), F32), pltpu.VMEM((n, C_W), F32)]
    return pl.pallas_call(
        _mixer_sample_kernel,
        grid=(A_H,),
        in_specs=in_specs,
        out_specs=out_specs,
        out_shape=out_shape,
        scratch_shapes=scratch,
        input_output_aliases={n_in + j: 1 + j for j in range(4)},
        compiler_params=pltpu.CompilerParams(
            dimension_semantics=("arbitrary",), vmem_limit_bytes=VMEM_LIMIT),
        name="mixer_sample",
    )(x, mod, *consts, *states, *prev)


def _top2_gates(logits, gidx, lane):
    neg = jnp.float32(-jnp.inf)
    big = jnp.float32(1 << 20)
    own = jnp.sum(jnp.where(lane == gidx, logits, 0.0), axis=1, keepdims=True)
    p_group = 1.0 / jnp.sum(jnp.where(lane < N_GROUPS, jnp.exp(logits - own), 0.0), axis=1, keepdims=True)
    e0 = N_GROUPS + N_PER_GROUP * gidx
    el = jnp.where((lane >= e0) & (lane < e0 + N_PER_GROUP), logits, neg)
    v1 = jnp.max(el, axis=1, keepdims=True)
    i1 = jnp.min(jnp.where(el == v1, lane, big), axis=1, keepdims=True)
    el2 = jnp.where(lane == i1, neg, el)
    v2 = jnp.max(el2, axis=1, keepdims=True)
    i2 = jnp.min(jnp.where(el2 == v2, lane, big), axis=1, keepdims=True)
    ex = jnp.exp(v2 - v1)
    w1 = 1.0 / (1.0 + ex)
    return jnp.where(lane == i1, p_group * w1, 0.0) + jnp.where(lane == i2, p_group * (ex * w1), 0.0)


def _group_experts(h, gates, lane, lane0, wg_ref, wu_ref, wd_ref):
    y = None
    for e in range(N_PER_GROUP):
        hid = _silu(_dot(h, wg_ref[e])) * _dot(h, wu_ref[e])
        gcol = jnp.sum(jnp.where(lane == lane0 + e, gates, 0.0), axis=1, keepdims=True)
        ye = _dot((hid * gcol).astype(BF16), wd_ref[e])
        y = ye if y is None else y + ye
    return y


def _moe_dense_kernel(x_ref, mod_ref, g2_ref, wr_ref, br_ref, wg_ref, wu_ref, wd_ref, fg_ref,
                      o_ref, h_scr, gate_scr, acc_scr, *, final_norm):
    g = pl.program_id(1)
    tm = x_ref.shape[0]
    lane = lax.broadcasted_iota(jnp.int32, (tm, LANES), 1).astype(F32)

    @pl.when(g == 0)
    def _():
        h = _rms_rows(x_ref[...]) * g2_ref[...]
        h = (h * (1.0 + mod_ref[:, 4 * D:5 * D]) + mod_ref[:, 3 * D:4 * D]).astype(BF16)
        h_scr[...] = h
        logits = _dot(h, wr_ref[...]) + br_ref[...]
        gl = jnp.where(lane < N_GROUPS, logits, -jnp.inf)
        gmax = jnp.max(gl, axis=1, keepdims=True)
        gidx = jnp.min(jnp.where(gl == gmax, lane, jnp.float32(1 << 20)), axis=1, keepdims=True)
        gate_scr[...] = _top2_gates(logits, gidx, lane)
        acc_scr[...] = jnp.zeros_like(acc_scr)

    lane0 = (N_GROUPS + N_PER_GROUP * g).astype(F32)
    acc_scr[...] += _group_experts(h_scr[...], gate_scr[...], lane, lane0, wg_ref, wu_ref, wd_ref)

    @pl.when(g == N_GROUPS - 1)
    def _():
        xn = x_ref[...] + mod_ref[:, 5 * D:6 * D] * acc_scr[...]
        if final_norm:
            xn = _rms_rows(xn) * fg_ref[...]
        o_ref[...] = xn


def _expert_specs(lp, l):
    def spec(arr):
        return pl.BlockSpec((None, N_PER_GROUP) + arr.shape[2:], lambda i, g, *_: (l, g, 0, 0))
    return [spec(lp["w_gate"]), spec(lp["w_up"]), spec(lp["w_down"])]


def _moe_dense_call(l, x2, mod, lp, final_g, *, final_norm):
    tm = x2.shape[0]
    kern = functools.partial(_moe_dense_kernel, final_norm=final_norm)
    return pl.pallas_call(
        kern,
        grid=(1, N_GROUPS),
        in_specs=[pl.BlockSpec((tm, D), lambda i, g: (0, 0)),
                  pl.BlockSpec((None, tm, 6 * D), lambda i, g: (l, 0, 0)),
                  _layer_spec(lp["g2"], l), _layer_spec(lp["w_router"], l), _layer_spec(lp["b_router"], l)]
        + _expert_specs(lp, l) + [pl.BlockSpec((1, D), lambda i, g: (0, 0))],
        out_specs=pl.BlockSpec((tm, D), lambda i, g: (0, 0)),
        out_shape=jax.ShapeDtypeStruct((tm, D), F32),
        scratch_shapes=[pltpu.VMEM((tm, D), BF16), pltpu.VMEM((tm, LANES), F32), pltpu.VMEM((tm, D), F32)],
        compiler_params=pltpu.CompilerParams(
            dimension_semantics=("arbitrary", "arbitrary"), vmem_limit_bytes=VMEM_LIMIT),
        name="moe_sample",
    )(x2, mod, lp["g2"], lp["w_router"], lp["b_router"], lp["w_gate"], lp["w_up"], lp["w_down"], final_g)


def _route_kernel(x_ref, mod_ref, g2_ref, wrt_ref, brc_ref, tri_ref, pos_ref, offs_ref, *, tiles_per_seq):
    tt = x_ref.shape[0]
    b = pl.program_id(0) // tiles_per_seq
    h = _rms_rows(x_ref[...]) * g2_ref[...]
    h = (h * (1.0 + mod_ref[pl.ds(b, 1), 4 * D:5 * D]) + mod_ref[pl.ds(b, 1), 3 * D:4 * D]).astype(BF16)
    lt = _dot_nt(wrt_ref[...], h) + brc_ref[...]
    row = lax.broadcasted_iota(jnp.int32, (SUBLANES, tt), 0).astype(F32)
    gl = jnp.where(row < N_GROUPS, lt[0:SUBLANES, :], -jnp.inf)
    gmax = jnp.max(gl, axis=0, keepdims=True)
    gidx = jnp.min(jnp.where(gl == gmax, row, jnp.float32(1 << 20)), axis=0, keepdims=True)
    onehot = jnp.where(row == gidx, 1.0, 0.0)
    rank = _dot(onehot.astype(BF16), tri_ref[...])
    counts = jnp.sum(onehot, axis=1, keepdims=True)
    row1 = lax.broadcasted_iota(jnp.int32, (SUBLANES, 1), 0)
    off = jnp.zeros((SUBLANES, 1), F32)
    for g in range(N_GROUPS - 1):
        off = off + jnp.where(row1 > g, counts[g:g + 1, :], 0.0)
    pos_ref[0] = (jnp.sum(onehot * (rank + off), axis=0, keepdims=True) * float(SUBLANES)).astype(jnp.int32)
    lane = lax.broadcasted_iota(jnp.int32, (1, LANES), 1)
    offs = jnp.where(lane == N_GROUPS, float(tt), 0.0)
    for g in range(1, N_GROUPS):
        offs = offs + jnp.where(lane == g, off[g:g + 1, :], 0.0)
    offs_ref[0] = offs.astype(jnp.int32)


def _route_call(l, x2, mod, ns, nb, lp, *, tt):
    t = x2.shape[0]
    nt = t // tt
    tiles_per_seq = nt // nb
    tri = jnp.asarray(np.triu(np.ones((tt, tt), np.float32), 1), BF16)
    kern = functools.partial(_route_kernel, tiles_per_seq=tiles_per_seq)
    return pl.pallas_call(
        kern,
        grid=(nt,),
        in_specs=[pl.BlockSpec((tt, D), lambda i: (i, 0)),
                  pl.BlockSpec((None, nb, 6 * D), lambda i: (l, ns // nb, 0), pipeline_mode=pl.Buffered(1)),
                  _layer_spec(lp["g2"], l), _layer_spec(lp["w_router_t"], l), _layer_spec(lp["b_router_c"], l),
                  pl.BlockSpec((tt, tt), lambda i: (0, 0), pipeline_mode=pl.Buffered(1))],
        out_specs=(pl.BlockSpec((1, 1, tt), lambda i: (i, 0, 0)),
                   pl.BlockSpec((1, 1, LANES), lambda i: (i, 0, 0))),
        out_shape=(jax.ShapeDtypeStruct((nt, 1, tt), jnp.int32),
                   jax.ShapeDtypeStruct((nt, 1, LANES), jnp.int32)),
        compiler_params=pltpu.CompilerParams(
            dimension_semantics=("arbitrary",), vmem_limit_bytes=VMEM_LIMIT),
        name="moe_route",
    )(x2, mod, lp["g2"], lp["w_router_t"], lp["b_router_c"], tri)


def _top2_gates_t(lt, gidx):
    rows = -(-(N_GROUPS + N_EXP) // SUBLANES) * SUBLANES
    n = lt.shape[1]
    neg = jnp.float32(-jnp.inf)
    big = jnp.float32(1 << 20)
    l = lt[0:rows, :]
    row = lax.broadcasted_iota(jnp.int32, (rows, n), 0).astype(F32)
    own = jnp.sum(jnp.where(row == gidx, l, 0.0), axis=0, keepdims=True)
    p_group = 1.0 / jnp.sum(jnp.where(row < N_GROUPS, jnp.exp(l - own), 0.0), axis=0, keepdims=True)
    e0 = N_GROUPS + N_PER_GROUP * gidx
    el = jnp.where((row >= e0) & (row < e0 + N_PER_GROUP), l, neg)
    v1 = jnp.max(el, axis=0, keepdims=True)
    i1 = jnp.min(jnp.where(el == v1, row, big), axis=0, keepdims=True)
    el2 = jnp.where(row == i1, neg, el)
    v2 = jnp.max(el2, axis=0, keepdims=True)
    i2 = jnp.min(jnp.where(el2 == v2, row, big), axis=0, keepdims=True)
    ex = jnp.exp(v2 - v1)
    w1 = 1.0 / (1.0 + ex)
    gates = jnp.where(row == i1, p_group * w1, 0.0) + jnp.where(row == i2, p_group * (ex * w1), 0.0)
    return jnp.concatenate([gates, jnp.zeros((LANES - rows, n), F32)], axis=0)


def _moe_sorted_kernel(pos_sm, offs_sm, x_ref, mod_ref, g2_ref, wrt_ref, brc_ref, wg_ref, wu_ref, wd_ref, fg_ref,
                       o_ref, a_scr, b_scr, hs_scr, gate_scr, acc_scr, *, tiles_per_seq, final_norm):
    i = pl.program_id(0)
    g = pl.program_id(1)
    tt = x_ref.shape[0]
    base = i * tt
    b = i // tiles_per_seq

    def mod(k):
        return mod_ref[pl.ds(b, 1), k * D:(k + 1) * D]

    def tile_rows(t):
        return pl.ds(pl.multiple_of(t * SUBLANES, SUBLANES), SUBLANES)

    def slot_rows(t):
        return pl.ds(pl.multiple_of(pos_sm[base + t], SUBLANES), SUBLANES)

    def to_token_tiles(v):
        for s in range(SUBLANES):
            a_scr[s:tt * SUBLANES:SUBLANES, :] = v[:, s * LANES:(s + 1) * LANES]

    def from_token_tiles():
        return jnp.concatenate([b_scr[s:tt * SUBLANES:SUBLANES, :] for s in range(SUBLANES)], axis=1)

    @pl.when(g == 0)
    def _():
        h = _rms_rows(x_ref[...]) * g2_ref[...]
        to_token_tiles(h * (1.0 + mod(4)) + mod(3))

        def scatter(t, carry):
            b_scr[slot_rows(t), :] = a_scr[tile_rows(t), :]
            return carry
        lax.fori_loop(0, tt, scatter, 0, unroll=32)
        hs = from_token_tiles().astype(BF16)
        hs_scr[0:tt, :] = hs
        hs_scr[tt:tt + MOE_BLOCK, :] = jnp.zeros((MOE_BLOCK, D), BF16)
        lt = _dot_nt(wrt_ref[...], hs) + brc_ref[...]
        slot = lax.broadcasted_iota(jnp.int32, (1, tt), 1)
        gidx = jnp.zeros((1, tt), F32)
        for gg in range(1, N_GROUPS):
            gidx = gidx + jnp.where(slot >= offs_sm[i * LANES + gg], 1.0, 0.0)
        gate_scr[0:tt, :] = _top2_gates_t(lt, gidx).T
        gate_scr[tt:tt + MOE_BLOCK, :] = jnp.zeros((MOE_BLOCK, LANES), F32)
        acc_scr[...] = jnp.zeros_like(acc_scr)

    lo = offs_sm[i * LANES + g]
    hi = offs_sm[i * LANES + g + 1]
    start = (lo // BF16_ROWS) * BF16_ROWS
    nblk = jnp.where(hi > lo, (hi - start + MOE_BLOCK - 1) // MOE_BLOCK, 0)
    lane0 = (N_GROUPS + N_PER_GROUP * g).astype(F32)
    blk_lane = lax.broadcasted_iota(jnp.int32, (MOE_BLOCK, LANES), 1).astype(F32)

    def block(k, carry):
        rs = pl.ds(pl.multiple_of(start + k * MOE_BLOCK, BF16_ROWS), MOE_BLOCK)
        acc_scr[rs, :] += _group_experts(hs_scr[rs, :], gate_scr[rs, :], blk_lane, lane0, wg_ref, wu_ref, wd_ref)
        return carry
    lax.fori_loop(0, nblk, block, 0)

    @pl.when(g == N_GROUPS - 1)
    def _():
        to_token_tiles(acc_scr[0:tt, :])

        def gather(t, carry):
            b_scr[tile_rows(t), :] = a_scr[slot_rows(t), :]
            return carry
        lax.fori_loop(0, tt, gather, 0, unroll=32)
        xn = x_ref[...] + mod(5) * from_token_tiles()
        if final_norm:
            xn = _rms_rows(xn) * fg_ref[...]
        o_ref[...] = xn


def _moe_sorted_call(l, x2, mod, ns, nb, lp, final_g, pos, offs, *, tt, final_norm):
    t = x2.shape[0]
    nt = t // tt
    kern = functools.partial(_moe_sorted_kernel, tiles_per_seq=nt // nb, final_norm=final_norm)
    grid_spec = pltpu.PrefetchScalarGridSpec(
        num_scalar_prefetch=2,
        grid=(nt, N_GROUPS),
        in_specs=[pl.BlockSpec((tt, D), lambda i, g, *_: (i, 0)),
                  pl.BlockSpec((None, nb, 6 * D), lambda i, g, *_: (l, ns // nb, 0), pipeline_mode=pl.Buffered(1)),
                  _layer_spec(lp["g2"], l), _layer_spec(lp["w_router_t"], l), _layer_spec(lp["b_router_c"], l)]
        + _expert_specs(lp, l) + [pl.BlockSpec((1, D), lambda i, g, *_: (0, 0))],
        out_specs=pl.BlockSpec((tt, D), lambda i, g, *_: (i, 0)),
        scratch_shapes=[pltpu.VMEM((tt * SUBLANES, LANES), F32), pltpu.VMEM((tt * SUBLANES, LANES), F32),
                        pltpu.VMEM((tt + MOE_BLOCK, D), BF16), pltpu.VMEM((tt + MOE_BLOCK, LANES), F32),
                        pltpu.VMEM((tt + MOE_BLOCK, D), F32)])
    return pl.pallas_call(
        kern,
        grid_spec=grid_spec,
        out_shape=jax.ShapeDtypeStruct((t, D), F32),
        compiler_params=pltpu.CompilerParams(
            dimension_semantics=("arbitrary", "arbitrary"), vmem_limit_bytes=VMEM_LIMIT),
        name="moe_prompt",
    )(pos, offs, x2, mod, lp["g2"], lp["w_router_t"], lp["b_router_c"], lp["w_gate"], lp["w_up"], lp["w_down"],
      final_g)


def _pad_to(a, axis, size):
    pad = [(0, 0)] * a.ndim
    pad[axis] = (0, size - a.shape[axis])
    return jnp.pad(a, pad)


def _prepare_params(p):
    lb_cum = jnp.cumsum(jax.nn.softmax(p["hgrn_lb_logits"].astype(F32), axis=0), axis=0)
    lb = lb_cum - lb_cum[:1]
    lbv = _pad_to(jnp.stack([jnp.log(lb), jnp.log1p(-lb), 1.0 - lb], axis=1), 1, SUBLANES)
    a_re, a_im = p["s5_a_re"].astype(F32), p["s5_a_im"].astype(F32)
    dt = jnp.exp(p["s5_log_dt"].astype(F32))[..., None]
    mag = jnp.exp(a_re * dt)
    abar_re = mag * jnp.cos(a_im * dt)
    abar_im = mag * jnp.sin(a_im * dt)
    den = a_re * a_re + a_im * a_im
    nr, ni = abar_re - 1.0, abar_im
    coef_re = ((nr * a_re + ni * a_im) / den)[..., None]
    coef_im = ((ni * a_re - nr * a_im) / den)[..., None]
    b_re, b_im = p["s5_b_re"].astype(F32), p["s5_b_im"].astype(F32)
    bb_re = coef_re * b_re - coef_im * b_im
    bb_im = coef_re * b_im + coef_im * b_re
    eye = jnp.eye(C_G, dtype=F32)
    blk_b = lambda m: jnp.einsum("lgph,gk->lghkp", m, eye).reshape(DEPTH, C_W, C_S)
    blk_bt = lambda m: jnp.einsum("lgph,gk->lgpkh", m, eye).reshape(DEPTH, C_S, C_W)
    blk_c = lambda m: jnp.einsum("lghp,gk->lgpkh", m, eye).reshape(DEPTH, C_S, C_W)
    bb = jnp.concatenate([blk_b(bb_re), blk_b(bb_im)], axis=2).astype(BF16)
    bb_t = jnp.concatenate([blk_bt(bb_re), blk_bt(bb_im)], axis=1).astype(BF16)
    cc = jnp.concatenate([blk_c(p["s5_c_re"].astype(F32)), -blk_c(p["s5_c_im"].astype(F32))],
                         axis=1).astype(BF16)
    abar_rows = jnp.stack([abar_re.reshape(DEPTH, C_S), abar_im.reshape(DEPTH, C_S)], axis=1)
    router = jnp.concatenate([p["moe_w_group"], p["moe_w_expert"]], axis=2)
    b_router = _pad_to(jnp.concatenate([p["moe_b_group"], p["moe_b_expert"]], axis=1), 1, LANES)
    row = lambda a: a.astype(F32)[:, None, :]
    col = lambda a: a.astype(F32)[:, :, None]
    return dict(
        g1=row(p["norm1_g"]), g2=row(p["norm2_g"]),
        w_in=jnp.transpose(p["w_in"], (0, 2, 1)).astype(BF16), lbv=lbv,
        w_alpha=_pad_to(_pad_to(p["gla_w_alpha"], 1, LANES), 2, 256).astype(BF16),
        b_alpha=_pad_to(row(p["gla_b_alpha"]), 2, 256),
        hgrn_g=row(jnp.tile(p["hgrn_norm_g"], (1, A_H))), gla_g=row(jnp.tile(p["gla_norm_g"], (1, B_H))),
        hgrn_gc=col(p["hgrn_norm_g"]), gla_gc=col(p["gla_norm_g"]),
        abar=_pad_to(abar_rows, 1, SUBLANES), abar_c=jnp.transpose(abar_rows, (0, 2, 1)),
        bb=bb, bb_t=bb_t, cc=cc, d_skip=row(p["s5_d"]),
        w_glu=p["s5_w_glu"].astype(BF16), b_glu=row(p["s5_b_glu"]), w_out=p["w_out"].astype(BF16),
        w_gate=p["moe_w_gate"].astype(BF16), w_up=p["moe_w_up"].astype(BF16),
        w_down=p["moe_w_down"].astype(BF16),
        w_router=_pad_to(router, 2, LANES).astype(BF16), b_router=b_router[:, None, :],
        w_router_t=_pad_to(jnp.transpose(router, (0, 2, 1)), 1, LANES).astype(BF16),
        b_router_c=b_router[:, :, None])


def kernel(x_prompt, x_sample, state_hgrn, state_gla, state_s5_re, state_s5_im, c_prompt, c_sample, ada_w, ada_b, norm1_g, norm2_g, w_in, hgrn_lb_logits, hgrn_norm_g, gla_w_alpha, gla_b_alpha, gla_norm_g, s5_a_re, s5_a_im, s5_b_re, s5_b_im, s5_c_re, s5_c_im, s5_d, s5_log_dt, s5_w_glu, s5_b_glu, w_out, moe_w_group, moe_b_group, moe_w_expert, moe_b_expert, moe_w_gate, moe_w_up, moe_w_down, final_norm_g):
    lp = _prepare_params(dict(
        norm1_g=norm1_g, norm2_g=norm2_g, w_in=w_in, hgrn_lb_logits=hgrn_lb_logits, hgrn_norm_g=hgrn_norm_g,
        gla_w_alpha=gla_w_alpha, gla_b_alpha=gla_b_alpha, gla_norm_g=gla_norm_g, s5_a_re=s5_a_re,
        s5_a_im=s5_a_im, s5_b_re=s5_b_re, s5_b_im=s5_b_im, s5_c_re=s5_c_re, s5_c_im=s5_c_im, s5_d=s5_d,
        s5_log_dt=s5_log_dt, s5_w_glu=s5_w_glu, s5_b_glu=s5_b_glu, w_out=w_out, moe_w_group=moe_w_group,
        moe_b_group=moe_b_group, moe_w_expert=moe_w_expert, moe_b_expert=moe_b_expert,
        moe_w_gate=moe_w_gate, moe_w_up=moe_w_up, moe_w_down=moe_w_down))
    nb, seq, _ = x_prompt.shape
    ns = x_sample.shape[0]
    assert ns % nb == 0 and seq % CHUNK == 0
    tt = min(MOE_TILE, seq)
    assert seq % tt == 0 and tt % BF16_ROWS == 0
    final_g = final_norm_g.reshape(1, D)

    mod = _ada_call(jnp.concatenate([c_sample, c_prompt], axis=0), ada_w, ada_b)
    states = (jnp.transpose(state_hgrn, (0, 2, 3, 4, 1)), jnp.transpose(state_gla, (0, 2, 3, 4, 1)),
              jnp.transpose(state_s5_re, (0, 2, 3, 1)).reshape(DEPTH, C_S, ns),
              jnp.transpose(state_s5_im, (0, 2, 3, 1)).reshape(DEPTH, C_S, ns))
    new_states = tuple(jnp.zeros_like(s) for s in states)

    xp = x_prompt
    xs = x_sample.reshape(ns, D)
    outs = {k: [] for k in ("hg_p", "gl_p", "sr_p", "si_p")}
    for l in range(DEPTH):
        last = l == DEPTH - 1
        xp, sta, stb, s5r, s5i = _mixer_prompt_call(l, xp, mod, ns, lp)
        x2 = xp.reshape(nb * seq, D)
        pos, offs = _route_call(l, x2, mod, ns, nb, lp, tt=tt)
        xp = _moe_sorted_call(l, x2, mod, ns, nb, lp, final_g, pos.reshape(-1), offs.reshape(-1),
                              tt=tt, final_norm=last).reshape(nb, seq, D)
        sta = sta.reshape(nb, A_H // 2, 2, A_DV, 2, A_DK)
        outs["hg_p"].append(jnp.stack([sta[:, h // 2, h % 2, :, h % 2, :] for h in range(A_H)], axis=1)
                            .transpose(0, 1, 3, 2))
        stb = stb.reshape(nb, B_H, B_DV, 4, B_DK)
        outs["gl_p"].append(jnp.stack([stb[:, h, :, h % 4, :] for h in range(B_H)], axis=1)
                            .transpose(0, 1, 3, 2))
        outs["sr_p"].append(s5r.reshape(nb, C_G, C_P))
        outs["si_p"].append(s5i.reshape(nb, C_G, C_P))
        xs, *new_states = _mixer_sample_call(l, xs, mod, lp, states, new_states)
        xs = _moe_dense_call(l, xs, mod, lp, final_g, final_norm=last)

    hg_s, gl_s, sr_s, si_s = new_states
    return (xp, xs.reshape(ns, 1, D),
            jnp.stack(outs["hg_p"]), jnp.stack(outs["gl_p"]), jnp.stack(outs["sr_p"]), jnp.stack(outs["si_p"]),
            jnp.transpose(hg_s, (0, 4, 1, 2, 3)), jnp.transpose(gl_s, (0, 4, 1, 2, 3)),
            jnp.transpose(sr_s.reshape(DEPTH, C_G, C_P, ns), (0, 3, 1, 2)),
            jnp.transpose(si_s.reshape(DEPTH, C_G, C_P, ns), (0, 3, 1, 2)))
```

```python
import functools
import math

import numpy as np
import jax
import jax.numpy as jnp
from jax import lax
from jax.experimental import pallas as pl
from jax.experimental.pallas import tpu as pltpu

F32 = jnp.float32
BF16 = jnp.bfloat16

D = 1024
DEPTH = 4
A_H, A_DK, A_DV = 6, 64, 64
B_H, B_DK, B_DV = 6, 32, 64
B_RANK = 16
B_TAU = 16.0
C_G, C_CH, C_P = 16, 16, 64
C_W = C_G * C_CH
C_S = C_G * C_P
N_GROUPS, N_PER_GROUP, D_EXP = 4, 4, 256
N_EXP = N_GROUPS * N_PER_GROUP
EPS = 1e-6
CHUNK = 64
LANES = 128
SUBLANES = 8
EXP_RANGE = 80.0
MOE_TILE = 1024
MOE_BLOCK = 384
MOE_TAIL_BLOCKS = (384, 256, 128)
BF16_ROWS = 16

W_AQ, W_AF, W_AI, W_AG = 0, 384, 768, 1152
W_BQ, W_BK, W_BV, W_LR, W_BR, W_CU = 1536, 1728, 1920, 2304, 2320, 2704
IN_COLS = 2960
W_SEGMENTS = {"aq|af": (W_AQ, 768), "ai|ag": (W_AI, 768), "bq": (W_BQ, 256), "bk": (W_BK, 256),
              "bv|lr": (W_BV, 512), "br|cu": (W_BR, 640)}
QA, FA, VA, GA = 0, 384, 768, 1152
QB, KB, VB, RB = 1536, 1792, 2048, 2432
CU, LB, KA = 2816, 3072, 3328
PW = 3712
N_LG = 5
LG_W = N_LG * LANES
BLOCKS = tuple((QA + LANES * p, KA + LANES * p, FA + LANES * p, LANES * p, 2, A_DK) for p in range(3)) + (
    (QB, KB, LB, 384, 4, B_DK), (QB + LANES, KB + LANES, LB + LANES, 640, 2, B_DK))
VO_W = 768

VMEM_LIMIT = 56 * 1024 * 1024


def _sigmoid(x):
    return 1.0 / (1.0 + jnp.exp(-x))


def _silu(x):
    return x * _sigmoid(x)


def _log_sigmoid(x):
    return jnp.minimum(x, 0.0) - jnp.log(1.0 + jnp.exp(-jnp.abs(x)))


def _gelu_tanh(x):
    return 0.5 * x * (1.0 + jnp.tanh(math.sqrt(2.0 / math.pi) * (x + 0.044715 * x * x * x)))


def _split3(x):
    hi = x.astype(BF16)
    r = x - hi.astype(F32)
    mid = r.astype(BF16)
    lo = (r - mid.astype(F32)).astype(BF16)
    return hi, mid, lo


def _dot(a, b):
    return jnp.dot(a, b, preferred_element_type=F32)


def _dot_nt(a, b):
    return lax.dot_general(a, b, (((1,), (1,)), ((), ())), preferred_element_type=F32)


def _dot_tn(a, b):
    return lax.dot_general(a, b, (((0,), (0,)), ((), ())), preferred_element_type=F32)


def _dot_exact_01(x, m01):
    hi, mid, lo = _split3(x)
    return _dot(hi, m01) + _dot(mid, m01) + _dot(lo, m01)


def _rms_rows(x):
    return x * lax.rsqrt(jnp.mean(x * x, axis=-1, keepdims=True) + EPS)


def _forget_gate(af, lbv):
    e = jnp.exp(-jnp.abs(af))
    log_sig = jnp.minimum(af, 0.0) - jnp.log(1.0 + e)
    a = lbv[0:1]
    b = lbv[1:2] + log_sig
    log_f = jnp.maximum(a, b) + jnp.log(1.0 + jnp.exp(-jnp.abs(a - b)))
    k = lbv[2:3] * (jnp.where(af >= 0.0, e, 1.0) / (1.0 + e))
    return log_f, k


def _chunk_cumsum(x, c):
    pos = lax.broadcasted_iota(jnp.int32, x.shape, 0) % c
    s = 1
    while s < c:
        x = x + jnp.where(pos >= s, pltpu.roll(x, s, 0), 0.0)
        s *= 2
    return x


def _head_rms(o, dv):
    w = o.shape[1]
    r = lax.broadcasted_iota(jnp.int32, (w, w), 0)
    cidx = lax.broadcasted_iota(jnp.int32, (w, w), 1)
    ones_blk = jnp.where(r // dv == cidx // dv, 1.0, 0.0).astype(BF16)
    return o * lax.rsqrt(_dot((o * o).astype(BF16), ones_blk) * (1.0 / dv) + EPS)


def _project_all(h, wint_ref, lbv_ref, walpha_ref, balpha_ref, chunk, p_scr):
    def proj(name):
        r0, n = W_SEGMENTS[name]
        return _dot_nt(h, wint_ref[r0:r0 + n, :])

    def cum(x):
        return x if chunk == 1 else _chunk_cumsum(x, chunk)

    aqf = proj("aq|af")
    p_scr[:, QA:QA + 384] = _silu(aqf[:, 0:384])
    log_f, ka = _forget_gate(aqf[:, 384:768], lbv_ref[...])
    p_scr[:, FA:FA + 384] = cum(log_f)
    p_scr[:, KA:KA + 384] = ka
    aig = proj("ai|ag")
    p_scr[:, VA:VA + 384] = aig[:, 0:384]
    p_scr[:, GA:GA + 384] = _silu(aig[:, 384:768])
    p_scr[:, QB:QB + 256] = proj("bq") * (B_DK ** -0.5)
    kb = proj("bk")
    p_scr[:, KB:KB + 256] = jnp.where(lax.broadcasted_iota(jnp.int32, kb.shape, 1) < B_H * B_DK, kb, 0.0)
    bvlr = proj("bv|lr")
    p_scr[:, VB:VB + 384] = bvlr[:, 0:384]
    z = _dot(bvlr[:, 384:512].astype(BF16), walpha_ref[...]) + balpha_ref[...]
    p_scr[:, LB:LB + 256] = cum(_log_sigmoid(z) * (1.0 / B_TAU))
    brcu = proj("br|cu")
    p_scr[:, RB:RB + 384] = _silu(brcu[:, 0:384])
    p_scr[:, CU:CU + 256] = brcu[:, 384:640]


def _layer_spec(arr, l):
    nd = arr.ndim - 1
    return pl.BlockSpec((None,) + arr.shape[1:], lambda *_, _n=nd: (l,) + (0,) * _n,
                        pipeline_mode=pl.Buffered(1))


def _ada_kernel(c_ref, w_ref, b_ref, o_ref):
    sc = _silu(c_ref[...]).astype(BF16)
    half = w_ref.shape[2] // 2
    for n0 in (0, half):
        o_ref[0, :, n0:n0 + half] = _dot(sc, w_ref[0, :, n0:n0 + half].astype(BF16)) + b_ref[0, :, n0:n0 + half]


def _ada_call(c_all, ada_w, ada_b):
    n = c_all.shape[0]
    tn = 1536
    return pl.pallas_call(
        _ada_kernel,
        grid=(DEPTH, 6 * D // tn),
        in_specs=[
            pl.BlockSpec((n, D), lambda l, j: (0, 0)),
            pl.BlockSpec((1, D, tn), lambda l, j: (l, 0, j)),
            pl.BlockSpec((1, 1, tn), lambda l, j: (l, 0, j)),
        ],
        out_specs=pl.BlockSpec((1, n, tn), lambda l, j: (l, 0, j)),
        out_shape=jax.ShapeDtypeStruct((DEPTH, n, 6 * D), F32),
        compiler_params=pltpu.CompilerParams(
            dimension_semantics=("arbitrary", "arbitrary"), vmem_limit_bytes=VMEM_LIMIT),
        name="ada_mod",
    )(c_all, ada_w, ada_b.reshape(DEPTH, 1, 6 * D))


def _mixer_prompt_kernel(x_ref, mod_ref, g1_ref, win_ref, lbv_ref, walpha_ref, balpha_ref,
                         hg_ref, gg_ref, abar_ref, bb_ref, cc_ref, dsk_ref, wglu_ref, bglu_ref,
                         wout_ref,
                         xo_ref, sta_ref, stb_ref, s5r_ref, s5i_ref,
                         h_scr, p_scr, o_scr, xr_scr, xi_scr, t_scr,
                         qt_scr, kt_scr, qd_scr, kd_scr, v_scr, of_scr, dec_scr, sc_scr):
    nb = x_ref.shape[0]
    c = CHUNK
    i = pl.program_id(0)

    @pl.when(i == 0)
    def _():
        sta_ref[...] = jnp.zeros_like(sta_ref)
        stb_ref[...] = jnp.zeros_like(stb_ref)
        s5r_ref[...] = jnp.zeros_like(s5r_ref)
        s5i_ref[...] = jnp.zeros_like(s5i_ref)

    for b in range(nb):
        hb = _rms_rows(x_ref[b]) * g1_ref[...]
        hb = hb * (1.0 + mod_ref[b:b + 1, D:2 * D]) + mod_ref[b:b + 1, 0:D]
        h_scr[b * c:(b + 1) * c, :] = hb.astype(BF16)

    _project_all(h_scr[...], win_ref, lbv_ref, walpha_ref, balpha_ref, c, p_scr)

    v_scr[:, 0:384] = p_scr[:, VA:VA + 384].astype(BF16)
    v_scr[:, 384:768] = p_scr[:, VB:VB + 384].astype(BF16)
    rng = jnp.zeros((1, LANES), F32)
    for b in range(nb):
        rs = slice(b * c, (b + 1) * c)
        for g, (qc, kc, bc, _, _, _) in enumerate(BLOCKS):
            ls = slice(g * LANES, (g + 1) * LANES)
            q = p_scr[rs, qc:qc + LANES]
            k = p_scr[rs, kc:kc + LANES]
            bcum = p_scr[rs, bc:bc + LANES]
            b_mid = bcum[c // 2 - 1:c // 2, :]
            b_end = bcum[c - 1:c, :]
            d = bcum - b_mid
            rng = jnp.maximum(rng, jnp.maximum(jnp.abs(d[0:1, :]), jnp.abs(d[c - 1:c, :])))
            qt_scr[rs, ls] = (q * jnp.exp(d)).astype(BF16)
            kt_scr[rs, ls] = (k * jnp.exp(-d)).astype(BF16)
            qd_scr[rs, ls] = (q * jnp.exp(bcum)).astype(BF16)
            kd_scr[rs, ls] = (k * jnp.exp(b_end - bcum)).astype(BF16)
            dec_scr[b, :, ls] = jnp.broadcast_to(jnp.exp(b_end), (SUBLANES, LANES))
    halves = C_W // LANES
    for b in range(nb):
        for j in range(halves):
            t_scr[j, b:b + nb * c:nb, :] = p_scr[b * c:(b + 1) * c, CU + LANES * j:CU + LANES * (j + 1)]
    u_bf = jnp.concatenate([t_scr[j] for j in range(halves)], axis=1).astype(BF16)
    xr_scr[...] = _dot(u_bf, bb_ref[:, 0:C_S])
    xi_scr[...] = _dot(u_bf, bb_ref[:, C_S:2 * C_S])

    in_range = jnp.max(rng) <= EXP_RANGE

    lane = lax.broadcasted_iota(jnp.int32, (c, LANES), 1)

    def scores_direct(r0, qc, kc, bc, nh, dk):
        q = p_scr[pl.ds(r0, c), qc:qc + LANES]
        bcum = p_scr[pl.ds(r0, c), bc:bc + LANES]
        krow = lax.broadcasted_iota(jnp.int32, (LANES, nh * c), 0)
        kcol = lax.broadcasted_iota(jnp.int32, (LANES, nh * c), 1)
        sub = lax.broadcasted_iota(jnp.int32, (SUBLANES, LANES), 0)

        def row_of(col0, s):
            r8 = pl.ds(pl.multiple_of(r0 + (s // SUBLANES) * SUBLANES, SUBLANES), SUBLANES)
            return jnp.sum(jnp.where(sub == s % SUBLANES, p_scr[r8, col0:col0 + LANES], 0.0),
                           axis=0, keepdims=True)

        def body(s, sc):
            ks = row_of(kc, s)
            bs = row_of(bc, s)
            w = q * ks * jnp.exp(jnp.minimum(bcum - bs, 0.0))
            place = jnp.where(kcol == (krow // dk) * c + s, 1.0, 0.0).astype(BF16)
            return sc + _dot_exact_01(w, place)

        return lax.fori_loop(0, c, body, jnp.zeros((c, nh * c), F32))

    def stage_scores(b, r0, direct):
        rs = pl.ds(r0, c)
        for g, (qc, kc, bc, vc, nh, dk) in enumerate(BLOCKS):
            ls = slice(g * LANES, (g + 1) * LANES)
            if direct:
                sc = scores_direct(r0, qc, kc, bc, nh, dk)
            else:
                kt = kt_scr[rs, ls]
                kcat = jnp.concatenate(
                    [jnp.where(lane // dk == h, kt, jnp.zeros_like(kt)) for h in range(nh)], axis=0)
                sc = _dot_nt(qt_scr[rs, ls], kcat)
            row = lax.broadcasted_iota(jnp.int32, (c, nh * c), 0)
            col = lax.broadcasted_iota(jnp.int32, (c, nh * c), 1)
            sc_scr[rs, vc:vc + nh * c] = jnp.where(col % c <= row, sc, 0.0).astype(BF16)

    def stage_readout(b, r0):
        rs = pl.ds(r0, c)
        for g, (_, _, _, vc, nh, dk) in enumerate(BLOCKS):
            ls = slice(g * LANES, (g + 1) * LANES)
            vw = nh * 64
            st_ref = sta_ref if g < 3 else stb_ref
            v0 = vc if g < 3 else vc - 384
            v = v_scr[rs, vc:vc + vw]
            vlane = lax.broadcasted_iota(jnp.int32, (c, vw), 1)
            vcat = jnp.concatenate(
                [jnp.where(vlane // 64 == h, v, jnp.zeros_like(v)) for h in range(nh)], axis=0)
            st = st_ref[b, v0:v0 + vw, :]
            of_scr[rs, vc:vc + vw] = (_dot(sc_scr[rs, vc:vc + nh * c], vcat)
                                      + _dot_nt(qd_scr[rs, ls], st.astype(BF16)))

    def stage_update(b, r0):
        rs = pl.ds(r0, c)
        for g, (_, _, _, vc, nh, dk) in enumerate(BLOCKS):
            ls = slice(g * LANES, (g + 1) * LANES)
            vw = nh * 64
            st_ref = sta_ref if g < 3 else stb_ref
            v0 = vc if g < 3 else vc - 384
            upd = _dot_tn(v_scr[rs, vc:vc + vw], kd_scr[rs, ls])
            srow = lax.broadcasted_iota(jnp.int32, (vw, LANES), 0)
            scol = lax.broadcasted_iota(jnp.int32, (vw, LANES), 1)
            st_ref[b, v0:v0 + vw, :] = (st_ref[b, v0:v0 + vw, :] * dec_scr[b, 0:1, ls]
                                        + jnp.where(srow // 64 == scol // dk, upd, 0.0))

    @pl.when(in_range)
    def _():
        for b in range(nb):
            stage_scores(b, b * c, False)
        for b in range(nb):
            stage_readout(b, b * c)
        for b in range(nb):
            stage_update(b, b * c)

    @pl.when(jnp.logical_not(in_range))
    def _():
        def body(b, carry):
            r0 = pl.multiple_of(b * c, c)
            stage_scores(b, r0, True)
            stage_readout(b, r0)
            stage_update(b, r0)
            return carry
        lax.fori_loop(0, nb, body, 0)

    oa = _head_rms(of_scr[:, 0:384], A_DV) * hg_ref[...] * p_scr[:, GA:GA + 384]
    o_scr[:, 0:384] = oa.astype(BF16)
    ob = _head_rms(of_scr[:, 384:768], B_DV) * gg_ref[...] * p_scr[:, RB:RB + 384]
    o_scr[:, 384:768] = ob.astype(BF16)

    ar = abar_ref[0:1, :]
    ai = abar_ref[1:2, :]

    def scan_t(t, carry):
        sr, si = carry
        ts = pl.ds(pl.multiple_of(t * nb, nb), nb)
        nr = ar * sr - ai * si + xr_scr[ts, :]
        ni = ar * si + ai * sr + xi_scr[ts, :]
        xr_scr[ts, :] = nr
        xi_scr[ts, :] = ni
        return nr, ni

    sr, si = lax.fori_loop(0, c, scan_t, (s5r_ref[...], s5i_ref[...]), unroll=2)
    s5r_ref[...] = sr
    s5i_ref[...] = si
    u = jnp.concatenate([t_scr[j] for j in range(halves)], axis=1)
    half = nb * c // 2
    y = jnp.concatenate(
        [_dot(xr_scr[r:r + half, :].astype(BF16), cc_ref[0:C_S, :])
         + _dot(xi_scr[r:r + half, :].astype(BF16), cc_ref[C_S:2 * C_S, :]) for r in (0, half)], axis=0)
    y = y + dsk_ref[...] * u
    z = _gelu_tanh(y)
    oc = z * _sigmoid(_dot(z.astype(BF16), wglu_ref[...]) + bglu_ref[...])
    for j in range(halves):
        t_scr[j] = oc[:, LANES * j:LANES * (j + 1)]
    for b in range(nb):
        for j in range(halves):
            o_scr[b * c:(b + 1) * c, 768 + LANES * j:768 + LANES * (j + 1)] = (
                t_scr[j, b:b + nb * c:nb, :].astype(BF16))

    o_all = o_scr[...]
    for n0 in (0, D // 2):
        out = _dot(o_all, wout_ref[:, n0:n0 + D // 2])
        for b in range(nb):
            xo_ref[b, :, n0:n0 + D // 2] = (x_ref[b, :, n0:n0 + D // 2]
                                            + mod_ref[b:b + 1, 2 * D + n0:2 * D + n0 + D // 2]
                                            * out[b * c:(b + 1) * c, :])


def _mixer_prompt_call(l, x, mod, ns, lp):
    nb, seq, _ = x.shape
    c = CHUNK
    rows = nb * c
    consts = [lp[k] for k in ("g1", "w_in", "lbv", "w_alpha", "b_alpha", "hgrn_g", "gla_g", "abar", "bb", "cc",
                              "d_skip", "w_glu", "b_glu", "w_out")]
    in_specs = ([pl.BlockSpec((nb, c, D), lambda i: (0, i, 0)),
                 pl.BlockSpec((None, nb, 6 * D), lambda i: (l, ns // nb, 0), pipeline_mode=pl.Buffered(1))]
                + [_layer_spec(a, l) for a in consts])
    out_shape = (jax.ShapeDtypeStruct((nb, seq, D), F32),
                 jax.ShapeDtypeStruct((nb, 384, LANES), F32),
                 jax.ShapeDtypeStruct((nb, 384, LANES), F32),
                 jax.ShapeDtypeStruct((nb, C_S), F32),
                 jax.ShapeDtypeStruct((nb, C_S), F32))
    out_specs = (pl.BlockSpec((nb, c, D), lambda i: (0, i, 0)),
                 pl.BlockSpec((nb, 384, LANES), lambda i: (0, 0, 0)),
                 pl.BlockSpec((nb, 384, LANES), lambda i: (0, 0, 0)),
                 pl.BlockSpec((nb, C_S), lambda i: (0, 0)),
                 pl.BlockSpec((nb, C_S), lambda i: (0, 0)))
    scratch = [pltpu.VMEM((rows, D), BF16), pltpu.VMEM((rows, PW), F32), pltpu.VMEM((rows, D), BF16),
               pltpu.VMEM((rows, C_S), F32), pltpu.VMEM((rows, C_S), F32),
               pltpu.VMEM((C_W // LANES, rows, LANES), F32),
               pltpu.VMEM((rows, LG_W), BF16), pltpu.VMEM((rows, LG_W), BF16),
               pltpu.VMEM((rows, LG_W), BF16), pltpu.VMEM((rows, LG_W), BF16),
               pltpu.VMEM((rows, VO_W), BF16), pltpu.VMEM((rows, VO_W), F32),
               pltpu.VMEM((nb, SUBLANES, LG_W), F32), pltpu.VMEM((rows, VO_W), BF16)]
    return pl.pallas_call(
        _mixer_prompt_kernel,
        grid=(seq // c,),
        in_specs=in_specs,
        out_specs=out_specs,
        out_shape=out_shape,
        scratch_shapes=scratch,
        compiler_params=pltpu.CompilerParams(
            dimension_semantics=("arbitrary",), vmem_limit_bytes=VMEM_LIMIT),
        name="mixer_prompt",
    )(x, mod, *consts)


def _step_head(s_ref, so_ref, t_scr, i, v_t):
    acc = jnp.zeros(v_t.shape, F32)
    for k in range(s_ref.shape[0]):
        new = s_ref[k] * t_scr[1, i, k:k + 1, :] + t_scr[2, i, k:k + 1, :] * v_t
        so_ref[k] = new
        acc = acc + new * t_scr[0, i, k:k + 1, :]
    return acc


def _mixer_sample_kernel(x_ref, mod_ref, g1_ref, win_ref, lbv_ref, walpha_ref, balpha_ref,
                         hgc_ref, ggc_ref, abarc_ref, bbt_ref, cc_ref, dsk_ref, wglu_ref, bglu_ref,
                         wout_ref, sa_ref, sb_ref, s5r_ref, s5i_ref, pa_ref, pb_ref, pr_ref, pi_ref,
                         xo_ref, sao_ref, sbo_ref, s5ro_ref, s5io_ref,
                         p_scr, at_scr, bt_scr, vt_scr, gate_scr, oh_scr, oc_scr):
    del pa_ref, pb_ref, pr_ref, pi_ref
    i = pl.program_id(0)

    @pl.when(i == 0)
    def _():
        h = _rms_rows(x_ref[...]) * g1_ref[...]
        h = (h * (1.0 + mod_ref[:, D:2 * D]) + mod_ref[:, 0:D]).astype(BF16)
        _project_all(h, win_ref, lbv_ref, walpha_ref, balpha_ref, 1, p_scr)
        for g, (qc, kc, bc, _, nh, dk) in enumerate(BLOCKS):
            t_scr, h0 = (at_scr, 2 * g) if g < 3 else (bt_scr, 4 * (g - 3))
            qt = p_scr[:, qc:qc + LANES].T
            dt = jnp.exp(p_scr[:, bc:bc + LANES]).T
            kt = p_scr[:, kc:kc + LANES].T
            for hh in range(nh):
                t_scr[0, h0 + hh] = qt[hh * dk:(hh + 1) * dk, :]
                t_scr[1, h0 + hh] = dt[hh * dk:(hh + 1) * dk, :]
                t_scr[2, h0 + hh] = kt[hh * dk:(hh + 1) * dk, :]
        for p in range(A_H // 2):
            for grp, vcol, gcol in ((0, VA, GA), (1, VB, RB)):
                vt = p_scr[:, vcol + LANES * p:vcol + LANES * (p + 1)].T
                gt = p_scr[:, gcol + LANES * p:gcol + LANES * (p + 1)].T
                for hh in range(2):
                    vt_scr[grp, 2 * p + hh] = vt[64 * hh:64 * (hh + 1), :]
                    gate_scr[grp, 2 * p + hh] = gt[64 * hh:64 * (hh + 1), :]
        u = p_scr[:, CU:CU + C_W]
        xri = _dot_nt(bbt_ref[...], u.astype(BF16))
        ar = abarc_ref[:, 0:1]
        ai = abarc_ref[:, 1:2]
        s0r = s5r_ref[...]
        s0i = s5i_ref[...]
        sr = ar * s0r - ai * s0i + xri[0:C_S, :]
        si = ar * s0i + ai * s0r + xri[C_S:2 * C_S, :]
        s5ro_ref[...] = sr
        s5io_ref[...] = si
        y = (_dot_tn(sr.astype(BF16), cc_ref[0:C_S, :]) + _dot_tn(si.astype(BF16), cc_ref[C_S:2 * C_S, :])
             + dsk_ref[...] * u)
        z = _gelu_tanh(y)
        oc_scr[...] = z * _sigmoid(_dot(z.astype(BF16), wglu_ref[...]) + bglu_ref[...])

    def norm_gate(o, gain_col, gate):
        return o * lax.rsqrt(jnp.mean(o * o, axis=0, keepdims=True) + EPS) * gain_col * gate

    oa = _step_head(sa_ref, sao_ref, at_scr, i, vt_scr[0, i])
    oh_scr[0, i] = norm_gate(oa, hgc_ref[...], gate_scr[0, i])
    ob = _step_head(sb_ref, sbo_ref, bt_scr, i, vt_scr[1, i])
    oh_scr[1, i] = norm_gate(ob, ggc_ref[...], gate_scr[1, i])

    @pl.when(i == A_H - 1)
    def _():
        parts = []
        for grp in range(2):
            for p in range(A_H // 2):
                two = jnp.concatenate([oh_scr[grp, 2 * p], oh_scr[grp, 2 * p + 1]], axis=0)
                parts.append(two.T)
        o = jnp.concatenate(parts + [oc_scr[...]], axis=1).astype(BF16)
        xo_ref[...] = x_ref[...] + mod_ref[:, 2 * D:3 * D] * _dot(o, wout_ref[...])


def _mixer_sample_call(l, x, mod, lp, states, prev):
    n = x.shape[0]
    consts = [lp[k] for k in ("g1", "w_in", "lbv", "w_alpha", "b_alpha", "hgrn_gc", "gla_gc", "abar_c", "bb_t",
                              "cc", "d_skip", "w_glu", "b_glu", "w_out")]
    state_specs = [pl.BlockSpec((None, None, A_DK, A_DV, n), lambda i: (l, i, 0, 0, 0)),
                   pl.BlockSpec((None, None, B_DK, B_DV, n), lambda i: (l, i, 0, 0, 0)),
                   pl.BlockSpec((None, C_S, n), lambda i: (l, 0, 0)),
                   pl.BlockSpec((None, C_S, n), lambda i: (l, 0, 0))]
    in_specs = ([pl.BlockSpec((n, D), lambda i: (0, 0), pipeline_mode=pl.Buffered(1)),
                 pl.BlockSpec((None, n, 6 * D), lambda i: (l, 0, 0), pipeline_mode=pl.Buffered(1))]
                + [_layer_spec(a, l) for a in consts] + state_specs
                + [pl.BlockSpec(memory_space=pl.ANY)] * 4)
    n_in = 2 + len(consts) + 4
    out_shape = (jax.ShapeDtypeStruct((n, D), F32),) + tuple(jax.ShapeDtypeStruct(s.shape, F32) for s in states)
    out_specs = (pl.BlockSpec((n, D), lambda i: (0, 0)),) + tuple(state_specs)
    scratch = [pltpu.VMEM((n, PW), F32), pltpu.VMEM((3, A_H, A_DK, n), F32), pltpu.VMEM((3, 8, B_DK, n), F32),
               pltpu.VMEM((2, A_H, 64, n), F32), pltpu.VMEM((2, A_H, 64, n), F32),
               pltpu.VMEM((2, A_H, 64, n), F32), pltpu.VMEM((n, C_W), F32)]
    return pl.pallas_call(
        _mixer_sample_kernel,
        grid=(A_H,),
        in_specs=in_specs,
        out_specs=out_specs,
        out_shape=out_shape,
        scratch_shapes=scratch,
        input_output_aliases={n_in + j: 1 + j for j in range(4)},
        compiler_params=pltpu.CompilerParams(
            dimension_semantics=("arbitrary",), vmem_limit_bytes=VMEM_LIMIT),
        name="mixer_sample",
    )(x, mod, *consts, *states, *prev)


def _top2_gates(logits, gidx, lane):
    neg = jnp.float32(-jnp.inf)
    big = jnp.float32(1 << 20)
    own = jnp.sum(jnp.where(lane == gidx, logits, 0.0), axis=1, keepdims=True)
    p_group = 1.0 / jnp.sum(jnp.where(lane < N_GROUPS, jnp.exp(logits - own), 0.0), axis=1, keepdims=True)
    e0 = N_GROUPS + N_PER_GROUP * gidx
    el = jnp.where((lane >= e0) & (lane < e0 + N_PER_GROUP), logits, neg)
    v1 = jnp.max(el, axis=1, keepdims=True)
    i1 = jnp.min(jnp.where(el == v1, lane, big), axis=1, keepdims=True)
    el2 = jnp.where(lane == i1, neg, el)
    v2 = jnp.max(el2, axis=1, keepdims=True)
    i2 = jnp.min(jnp.where(el2 == v2, lane, big), axis=1, keepdims=True)
    ex = jnp.exp(v2 - v1)
    w1 = 1.0 / (1.0 + ex)
    return jnp.where(lane == i1, p_group * w1, 0.0) + jnp.where(lane == i2, p_group * (ex * w1), 0.0)


def _group_experts(h, gates, lane, lane0, wg_ref, wu_ref, wd_ref):
    y = None
    for e in range(N_PER_GROUP):
        hid = _silu(_dot(h, wg_ref[e])) * _dot(h, wu_ref[e])
        gcol = jnp.sum(jnp.where(lane == lane0 + e, gates, 0.0), axis=1, keepdims=True)
        ye = _dot((hid * gcol).astype(BF16), wd_ref[e])
        y = ye if y is None else y + ye
    return y


def _moe_dense_kernel(x_ref, mod_ref, g2_ref, wr_ref, br_ref, wg_ref, wu_ref, wd_ref, fg_ref,
                      o_ref, h_scr, gate_scr, acc_scr, *, final_norm):
    g = pl.program_id(1)
    tm = x_ref.shape[0]
    lane = lax.broadcasted_iota(jnp.int32, (tm, LANES), 1).astype(F32)

    @pl.when(g == 0)
    def _():
        h = _rms_rows(x_ref[...]) * g2_ref[...]
        h = (h * (1.0 + mod_ref[:, 4 * D:5 * D]) + mod_ref[:, 3 * D:4 * D]).astype(BF16)
        h_scr[...] = h
        logits = _dot(h, wr_ref[...]) + br_ref[...]
        gl = jnp.where(lane < N_GROUPS, logits, -jnp.inf)
        gmax = jnp.max(gl, axis=1, keepdims=True)
        gidx = jnp.min(jnp.where(gl == gmax, lane, jnp.float32(1 << 20)), axis=1, keepdims=True)
        gate_scr[...] = _top2_gates(logits, gidx, lane)
        acc_scr[...] = jnp.zeros_like(acc_scr)

    lane0 = (N_GROUPS + N_PER_GROUP * g).astype(F32)
    acc_scr[...] += _group_experts(h_scr[...], gate_scr[...], lane, lane0, wg_ref, wu_ref, wd_ref)

    @pl.when(g == N_GROUPS - 1)
    def _():
        xn = x_ref[...] + mod_ref[:, 5 * D:6 * D] * acc_scr[...]
        if final_norm:
            xn = _rms_rows(xn) * fg_ref[...]
        o_ref[...] = xn


def _expert_specs(lp, l):
    def spec(arr):
        return pl.BlockSpec((None, N_PER_GROUP) + arr.shape[2:], lambda i, g, *_: (l, g, 0, 0))
    return [spec(lp["w_gate"]), spec(lp["w_up"]), spec(lp["w_down"])]


def _moe_dense_call(l, x2, mod, lp, final_g, *, final_norm):
    tm = x2.shape[0]
    kern = functools.partial(_moe_dense_kernel, final_norm=final_norm)
    return pl.pallas_call(
        kern,
        grid=(1, N_GROUPS),
        in_specs=[pl.BlockSpec((tm, D), lambda i, g: (0, 0)),
                  pl.BlockSpec((None, tm, 6 * D), lambda i, g: (l, 0, 0)),
                  _layer_spec(lp["g2"], l), _layer_spec(lp["w_router"], l), _layer_spec(lp["b_router"], l)]
        + _expert_specs(lp, l) + [pl.BlockSpec((1, D), lambda i, g: (0, 0))],
        out_specs=pl.BlockSpec((tm, D), lambda i, g: (0, 0)),
        out_shape=jax.ShapeDtypeStruct((tm, D), F32),
        scratch_shapes=[pltpu.VMEM((tm, D), BF16), pltpu.VMEM((tm, LANES), F32), pltpu.VMEM((tm, D), F32)],
        compiler_params=pltpu.CompilerParams(
            dimension_semantics=("arbitrary", "arbitrary"), vmem_limit_bytes=VMEM_LIMIT),
        name="moe_sample",
    )(x2, mod, lp["g2"], lp["w_router"], lp["b_router"], lp["w_gate"], lp["w_up"], lp["w_down"], final_g)


def _route_kernel(x_ref, mod_ref, g2_ref, wrt_ref, brc_ref, tri_ref, pos_ref, offs_ref, *, tiles_per_seq):
    tt = x_ref.shape[0]
    b = pl.program_id(0) // tiles_per_seq
    h = _rms_rows(x_ref[...]) * g2_ref[...]
    h = (h * (1.0 + mod_ref[pl.ds(b, 1), 4 * D:5 * D]) + mod_ref[pl.ds(b, 1), 3 * D:4 * D]).astype(BF16)
    lt = _dot_nt(wrt_ref[...], h) + brc_ref[...]
    row = lax.broadcasted_iota(jnp.int32, (SUBLANES, tt), 0).astype(F32)
    gl = jnp.where(row < N_GROUPS, lt[0:SUBLANES, :], -jnp.inf)
    gmax = jnp.max(gl, axis=0, keepdims=True)
    gidx = jnp.min(jnp.where(gl == gmax, row, jnp.float32(1 << 20)), axis=0, keepdims=True)
    onehot = jnp.where(row == gidx, 1.0, 0.0)
    rank = _dot(onehot.astype(BF16), tri_ref[...])
    counts = jnp.sum(onehot, axis=1, keepdims=True)
    row1 = lax.broadcasted_iota(jnp.int32, (SUBLANES, 1), 0)
    off = jnp.zeros((SUBLANES, 1), F32)
    for g in range(N_GROUPS - 1):
        off = off + jnp.where(row1 > g, counts[g:g + 1, :], 0.0)
    pos_ref[0] = (jnp.sum(onehot * (rank + off), axis=0, keepdims=True) * float(SUBLANES)).astype(jnp.int32)
    lane = lax.broadcasted_iota(jnp.int32, (1, LANES), 1)
    offs = jnp.where(lane == N_GROUPS, float(tt), 0.0)
    for g in range(1, N_GROUPS):
        offs = offs + jnp.where(lane == g, off[g:g + 1, :], 0.0)
    offs_ref[0] = offs.astype(jnp.int32)


def _route_call(l, x2, mod, ns, nb, lp, *, tt):
    t = x2.shape[0]
    nt = t // tt
    tiles_per_seq = nt // nb
    tri = jnp.asarray(np.triu(np.ones((tt, tt), np.float32), 1), BF16)
    kern = functools.partial(_route_kernel, tiles_per_seq=tiles_per_seq)
    return pl.pallas_call(
        kern,
        grid=(nt,),
        in_specs=[pl.BlockSpec((tt, D), lambda i: (i, 0)),
                  pl.BlockSpec((None, nb, 6 * D), lambda i: (l, ns // nb, 0), pipeline_mode=pl.Buffered(1)),
                  _layer_spec(lp["g2"], l), _layer_spec(lp["w_router_t"], l), _layer_spec(lp["b_router_c"], l),
                  pl.BlockSpec((tt, tt), lambda i: (0, 0), pipeline_mode=pl.Buffered(1))],
        out_specs=(pl.BlockSpec((1, 1, tt), lambda i: (i, 0, 0)),
                   pl.BlockSpec((1, 1, LANES), lambda i: (i, 0, 0))),
        out_shape=(jax.ShapeDtypeStruct((nt, 1, tt), jnp.int32),
                   jax.ShapeDtypeStruct((nt, 1, LANES), jnp.int32)),
        compiler_params=pltpu.CompilerParams(
            dimension_semantics=("arbitrary",), vmem_limit_bytes=VMEM_LIMIT),
        name="moe_route",
    )(x2, mod, lp["g2"], lp["w_router_t"], lp["b_router_c"], tri)


def _top2_gates_t(lt, gidx):
    rows = -(-(N_GROUPS + N_EXP) // SUBLANES) * SUBLANES
    n = lt.shape[1]
    neg = jnp.float32(-jnp.inf)
    big = jnp.float32(1 << 20)
    l = lt[0:rows, :]
    row = lax.broadcasted_iota(jnp.int32, (rows, n), 0).astype(F32)
    own = jnp.sum(jnp.where(row == gidx, l, 0.0), axis=0, keepdims=True)
    p_group = 1.0 / jnp.sum(jnp.where(row < N_GROUPS, jnp.exp(l - own), 0.0), axis=0, keepdims=True)
    e0 = N_GROUPS + N_PER_GROUP * gidx
    el = jnp.where((row >= e0) & (row < e0 + N_PER_GROUP), l, neg)
    v1 = jnp.max(el, axis=0, keepdims=True)
    i1 = jnp.min(jnp.where(el == v1, row, big), axis=0, keepdims=True)
    el2 = jnp.where(row == i1, neg, el)
    v2 = jnp.max(el2, axis=0, keepdims=True)
    i2 = jnp.min(jnp.where(el2 == v2, row, big), axis=0, keepdims=True)
    ex = jnp.exp(v2 - v1)
    w1 = 1.0 / (1.0 + ex)
    gates = jnp.where(row == i1, p_group * w1, 0.0) + jnp.where(row == i2, p_group * (ex * w1), 0.0)
    return jnp.concatenate([gates, jnp.zeros((LANES - rows, n), F32)], axis=0)


def _moe_sorted_kernel(pos_sm, offs_sm, x_ref, mod_ref, g2_ref, wrt_ref, brc_ref, wg_ref, wu_ref, wd_ref, fg_ref,
                       o_ref, a_scr, b_scr, hs_scr, gate_scr, acc_scr, *, tiles_per_seq, final_norm):
    i = pl.program_id(0)
    g = pl.program_id(1)
    tt = x_ref.shape[0]
    base = i * tt
    b = i // tiles_per_seq

    def mod(k):
        return mod_ref[pl.ds(b, 1), k * D:(k + 1) * D]

    def tile_rows(t):
        return pl.ds(pl.multiple_of(t * SUBLANES, SUBLANES), SUBLANES)

    def slot_rows(t):
        return pl.ds(pl.multiple_of(pos_sm[base + t], SUBLANES), SUBLANES)

    def to_token_tiles(v):
        for s in range(SUBLANES):
            a_scr[s:tt * SUBLANES:SUBLANES, :] = v[:, s * LANES:(s + 1) * LANES]

    def from_token_tiles():
        return jnp.concatenate([b_scr[s:tt * SUBLANES:SUBLANES, :] for s in range(SUBLANES)], axis=1)

    @pl.when(g == 0)
    def _():
        h = _rms_rows(x_ref[...]) * g2_ref[...]
        to_token_tiles(h * (1.0 + mod(4)) + mod(3))

        def scatter(t, carry):
            b_scr[slot_rows(t), :] = a_scr[tile_rows(t), :]
            return carry
        lax.fori_loop(0, tt, scatter, 0, unroll=32)
        hs = from_token_tiles().astype(BF16)
        hs_scr[0:tt, :] = hs
        hs_scr[tt:tt + MOE_BLOCK, :] = jnp.zeros((MOE_BLOCK, D), BF16)
        lt = _dot_nt(wrt_ref[...], hs) + brc_ref[...]
        slot = lax.broadcasted_iota(jnp.int32, (1, tt), 1)
        gidx = jnp.zeros((1, tt), F32)
        for gg in range(1, N_GROUPS):
            gidx = gidx + jnp.where(slot >= offs_sm[i * LANES + gg], 1.0, 0.0)
        gate_scr[0:tt, :] = _top2_gates_t(lt, gidx).T
        gate_scr[tt:tt + MOE_BLOCK, :] = jnp.zeros((MOE_BLOCK, LANES), F32)
        acc_scr[...] = jnp.zeros_like(acc_scr)

    lo = offs_sm[i * LANES + g]
    hi = offs_sm[i * LANES + g + 1]
    start = (lo // BF16_ROWS) * BF16_ROWS
    span = jnp.where(hi > lo, hi - start, 0)
    lane0 = (N_GROUPS + N_PER_GROUP * g).astype(F32)

    def run_block(r0, m):
        rs = pl.ds(pl.multiple_of(r0, BF16_ROWS), m)
        blk_lane = lax.broadcasted_iota(jnp.int32, (m, LANES), 1).astype(F32)
        acc_scr[rs, :] += _group_experts(hs_scr[rs, :], gate_scr[rs, :], blk_lane, lane0, wg_ref, wu_ref, wd_ref)

    n_full = span // MOE_BLOCK

    def full_block(k, carry):
        run_block(start + k * MOE_BLOCK, MOE_BLOCK)
        return carry
    lax.fori_loop(0, n_full, full_block, 0)
    rem = span - n_full * MOE_BLOCK
    for m, below in zip(MOE_TAIL_BLOCKS, MOE_TAIL_BLOCKS[1:] + (0,)):
        @pl.when((rem > below) & (rem <= m))
        def _(m=m):
            run_block(start + n_full * MOE_BLOCK, m)

    @pl.when(g == N_GROUPS - 1)
    def _():
        to_token_tiles(acc_scr[0:tt, :])

        def gather(t, carry):
            b_scr[tile_rows(t), :] = a_scr[slot_rows(t), :]
            return carry
        lax.fori_loop(0, tt, gather, 0, unroll=32)
        xn = x_ref[...] + mod(5) * from_token_tiles()
        if final_norm:
            xn = _rms_rows(xn) * fg_ref[...]
        o_ref[...] = xn


def _moe_sorted_call(l, x2, mod, ns, nb, lp, final_g, pos, offs, *, tt, final_norm):
    t = x2.shape[0]
    nt = t // tt
    kern = functools.partial(_moe_sorted_kernel, tiles_per_seq=nt // nb, final_norm=final_norm)
    grid_spec = pltpu.PrefetchScalarGridSpec(
        num_scalar_prefetch=2,
        grid=(nt, N_GROUPS),
        in_specs=[pl.BlockSpec((tt, D), lambda i, g, *_: (i, 0)),
                  pl.BlockSpec((None, nb, 6 * D), lambda i, g, *_: (l, ns // nb, 0), pipeline_mode=pl.Buffered(1)),
                  _layer_spec(lp["g2"], l), _layer_spec(lp["w_router_t"], l), _layer_spec(lp["b_router_c"], l)]
        + _expert_specs(lp, l) + [pl.BlockSpec((1, D), lambda i, g, *_: (0, 0))],
        out_specs=pl.BlockSpec((tt, D), lambda i, g, *_: (i, 0)),
        scratch_shapes=[pltpu.VMEM((tt * SUBLANES, LANES), F32), pltpu.VMEM((tt * SUBLANES, LANES), F32),
                        pltpu.VMEM((tt + MOE_BLOCK, D), BF16), pltpu.VMEM((tt + MOE_BLOCK, LANES), F32),
                        pltpu.VMEM((tt + MOE_BLOCK, D), F32)])
    return pl.pallas_call(
        kern,
        grid_spec=grid_spec,
        out_shape=jax.ShapeDtypeStruct((t, D), F32),
        compiler_params=pltpu.CompilerParams(
            dimension_semantics=("arbitrary", "arbitrary"), vmem_limit_bytes=VMEM_LIMIT),
        name="moe_prompt",
    )(pos, offs, x2, mod, lp["g2"], lp["w_router_t"], lp["b_router_c"], lp["w_gate"], lp["w_up"], lp["w_down"],
      final_g)


def _pad_to(a, axis, size):
    pad = [(0, 0)] * a.ndim
    pad[axis] = (0, size - a.shape[axis])
    return jnp.pad(a, pad)


def _prepare_params(p):
    lb_cum = jnp.cumsum(jax.nn.softmax(p["hgrn_lb_logits"].astype(F32), axis=0), axis=0)
    lb = lb_cum - lb_cum[:1]
    lbv = _pad_to(jnp.stack([jnp.log(lb), jnp.log1p(-lb), 1.0 - lb], axis=1), 1, SUBLANES)
    a_re, a_im = p["s5_a_re"].astype(F32), p["s5_a_im"].astype(F32)
    dt = jnp.exp(p["s5_log_dt"].astype(F32))[..., None]
    mag = jnp.exp(a_re * dt)
    abar_re = mag * jnp.cos(a_im * dt)
    abar_im = mag * jnp.sin(a_im * dt)
    den = a_re * a_re + a_im * a_im
    nr, ni = abar_re - 1.0, abar_im
    coef_re = ((nr * a_re + ni * a_im) / den)[..., None]
    coef_im = ((ni * a_re - nr * a_im) / den)[..., None]
    b_re, b_im = p["s5_b_re"].astype(F32), p["s5_b_im"].astype(F32)
    bb_re = coef_re * b_re - coef_im * b_im
    bb_im = coef_re * b_im + coef_im * b_re
    eye = jnp.eye(C_G, dtype=F32)
    blk_b = lambda m: jnp.einsum("lgph,gk->lghkp", m, eye).reshape(DEPTH, C_W, C_S)
    blk_bt = lambda m: jnp.einsum("lgph,gk->lgpkh", m, eye).reshape(DEPTH, C_S, C_W)
    blk_c = lambda m: jnp.einsum("lghp,gk->lgpkh", m, eye).reshape(DEPTH, C_S, C_W)
    bb = jnp.concatenate([blk_b(bb_re), blk_b(bb_im)], axis=2).astype(BF16)
    bb_t = jnp.concatenate([blk_bt(bb_re), blk_bt(bb_im)], axis=1).astype(BF16)
    cc = jnp.concatenate([blk_c(p["s5_c_re"].astype(F32)), -blk_c(p["s5_c_im"].astype(F32))],
                         axis=1).astype(BF16)
    abar_rows = jnp.stack([abar_re.reshape(DEPTH, C_S), abar_im.reshape(DEPTH, C_S)], axis=1)
    router = jnp.concatenate([p["moe_w_group"], p["moe_w_expert"]], axis=2)
    b_router = _pad_to(jnp.concatenate([p["moe_b_group"], p["moe_b_expert"]], axis=1), 1, LANES)
    row = lambda a: a.astype(F32)[:, None, :]
    col = lambda a: a.astype(F32)[:, :, None]
    return dict(
        g1=row(p["norm1_g"]), g2=row(p["norm2_g"]),
        w_in=jnp.transpose(p["w_in"], (0, 2, 1)).astype(BF16), lbv=lbv,
        w_alpha=_pad_to(_pad_to(p["gla_w_alpha"], 1, LANES), 2, 256).astype(BF16),
        b_alpha=_pad_to(row(p["gla_b_alpha"]), 2, 256),
        hgrn_g=row(jnp.tile(p["hgrn_norm_g"], (1, A_H))), gla_g=row(jnp.tile(p["gla_norm_g"], (1, B_H))),
        hgrn_gc=col(p["hgrn_norm_g"]), gla_gc=col(p["gla_norm_g"]),
        abar=_pad_to(abar_rows, 1, SUBLANES), abar_c=jnp.transpose(abar_rows, (0, 2, 1)),
        bb=bb, bb_t=bb_t, cc=cc, d_skip=row(p["s5_d"]),
        w_glu=p["s5_w_glu"].astype(BF16), b_glu=row(p["s5_b_glu"]), w_out=p["w_out"].astype(BF16),
        w_gate=p["moe_w_gate"].astype(BF16), w_up=p["moe_w_up"].astype(BF16),
        w_down=p["moe_w_down"].astype(BF16),
        w_router=_pad_to(router, 2, LANES).astype(BF16), b_router=b_router[:, None, :],
        w_router_t=_pad_to(jnp.transpose(router, (0, 2, 1)), 1, LANES).astype(BF16),
        b_router_c=b_router[:, :, None])


def kernel(x_prompt, x_sample, state_hgrn, state_gla, state_s5_re, state_s5_im, c_prompt, c_sample, ada_w, ada_b, norm1_g, norm2_g, w_in, hgrn_lb_logits, hgrn_norm_g, gla_w_alpha, gla_b_alpha, gla_norm_g, s5_a_re, s5_a_im, s5_b_re, s5_b_im, s5_c_re, s5_c_im, s5_d, s5_log_dt, s5_w_glu, s5_b_glu, w_out, moe_w_group, moe_b_group, moe_w_expert, moe_b_expert, moe_w_gate, moe_w_up, moe_w_down, final_norm_g):
    lp = _prepare_params(dict(
        norm1_g=norm1_g, norm2_g=norm2_g, w_in=w_in, hgrn_lb_logits=hgrn_lb_logits, hgrn_norm_g=hgrn_norm_g,
        gla_w_alpha=gla_w_alpha, gla_b_alpha=gla_b_alpha, gla_norm_g=gla_norm_g, s5_a_re=s5_a_re,
        s5_a_im=s5_a_im, s5_b_re=s5_b_re, s5_b_im=s5_b_im, s5_c_re=s5_c_re, s5_c_im=s5_c_im, s5_d=s5_d,
        s5_log_dt=s5_log_dt, s5_w_glu=s5_w_glu, s5_b_glu=s5_b_glu, w_out=w_out, moe_w_group=moe_w_group,
        moe_b_group=moe_b_group, moe_w_expert=moe_w_expert, moe_b_expert=moe_b_expert,
        moe_w_gate=moe_w_gate, moe_w_up=moe_w_up, moe_w_down=moe_w_down))
    nb, seq, _ = x_prompt.shape
    ns = x_sample.shape[0]
    assert ns % nb == 0 and seq % CHUNK == 0
    tt = min(MOE_TILE, seq)
    assert seq % tt == 0 and tt % BF16_ROWS == 0
    final_g = final_norm_g.reshape(1, D)

    mod = _ada_call(jnp.concatenate([c_sample, c_prompt], axis=0), ada_w, ada_b)
    states = (jnp.transpose(state_hgrn, (0, 2, 3, 4, 1)), jnp.transpose(state_gla, (0, 2, 3, 4, 1)),
              jnp.transpose(state_s5_re, (0, 2, 3, 1)).reshape(DEPTH, C_S, ns),
              jnp.transpose(state_s5_im, (0, 2, 3, 1)).reshape(DEPTH, C_S, ns))
    new_states = tuple(jnp.zeros_like(s) for s in states)

    xp = x_prompt
    xs = x_sample.reshape(ns, D)
    outs = {k: [] for k in ("hg_p", "gl_p", "sr_p", "si_p")}
    for l in range(DEPTH):
        last = l == DEPTH - 1
        xp, sta, stb, s5r, s5i = _mixer_prompt_call(l, xp, mod, ns, lp)
        x2 = xp.reshape(nb * seq, D)
        pos, offs = _route_call(l, x2, mod, ns, nb, lp, tt=tt)
        xp = _moe_sorted_call(l, x2, mod, ns, nb, lp, final_g, pos.reshape(-1), offs.reshape(-1),
                              tt=tt, final_norm=last).reshape(nb, seq, D)
        sta = sta.reshape(nb, A_H // 2, 2, A_DV, 2, A_DK)
        outs["hg_p"].append(jnp.stack([sta[:, h // 2, h % 2, :, h % 2, :] for h in range(A_H)], axis=1)
                            .transpose(0, 1, 3, 2))
        stb = stb.reshape(nb, B_H, B_DV, 4, B_DK)
        outs["gl_p"].append(jnp.stack([stb[:, h, :, h % 4, :] for h in range(B_H)], axis=1)
                            .transpose(0, 1, 3, 2))
        outs["sr_p"].append(s5r.reshape(nb, C_G, C_P))
        outs["si_p"].append(s5i.reshape(nb, C_G, C_P))
        xs, *new_states = _mixer_sample_call(l, xs, mod, lp, states, new_states)
        xs = _moe_dense_call(l, xs, mod, lp, final_g, final_norm=last)

    hg_s, gl_s, sr_s, si_s = new_states
    return (xp, xs.reshape(ns, 1, D),
            jnp.stack(outs["hg_p"]), jnp.stack(outs["gl_p"]), jnp.stack(outs["sr_p"]), jnp.stack(outs["si_p"]),
            jnp.transpose(hg_s, (0, 4, 1, 2, 3)), jnp.transpose(gl_s, (0, 4, 1, 2, 3)),
            jnp.transpose(sr_s.reshape(DEPTH, C_G, C_P, ns), (0, 3, 1, 2)),
            jnp.transpose(si_s.reshape(DEPTH, C_G, C_P, ns), (0, 3, 1, 2)))
```

```python
import functools
import math

import numpy as np
import jax
import jax.numpy as jnp
from jax import lax
from jax.experimental import pallas as pl
from jax.experimental.pallas import tpu as pltpu

F32 = jnp.float32
BF16 = jnp.bfloat16

D = 1024
DEPTH = 4
A_H, A_DK, A_DV = 6, 64, 64
B_H, B_DK, B_DV = 6, 32, 64
B_RANK = 16
B_TAU = 16.0
C_G, C_CH, C_P = 16, 16, 64
C_W = C_G * C_CH
C_S = C_G * C_P
N_GROUPS, N_PER_GROUP, D_EXP = 4, 4, 256
N_EXP = N_GROUPS * N_PER_GROUP
EPS = 1e-6
CHUNK = 64
LANES = 128
SUBLANES = 8
EXP_RANGE = 80.0
MOE_TILE = 1024
MOE_BLOCK = 384
MOE_TAIL_BLOCKS = (384, 256, 128)
BF16_ROWS = 16

W_AQ, W_AF, W_AI, W_AG = 0, 384, 768, 1152
W_BQ, W_BK, W_BV, W_LR, W_BR, W_CU = 1536, 1728, 1920, 2304, 2320, 2704
IN_COLS = 2960
W_SEGMENTS = {"aq|af": (W_AQ, 768), "ai|ag": (W_AI, 768), "bq": (W_BQ, 256), "bk": (W_BK, 256),
              "bv|lr": (W_BV, 512), "br|cu": (W_BR, 640)}
QA, FA, VA, GA = 0, 384, 768, 1152
QB, KB, VB, RB = 1536, 1792, 2048, 2432
CU, LB, KA = 2816, 3072, 3328
PW = 3712
N_LG = 5
LG_W = N_LG * LANES
BLOCKS = tuple((QA + LANES * p, KA + LANES * p, FA + LANES * p, LANES * p, 2, A_DK) for p in range(3)) + (
    (QB, KB, LB, 384, 4, B_DK), (QB + LANES, KB + LANES, LB + LANES, 640, 2, B_DK))
VO_W = 768

VMEM_LIMIT = 56 * 1024 * 1024


def _sigmoid(x):
    return 1.0 / (1.0 + jnp.exp(-x))


def _silu(x):
    return x * _sigmoid(x)


def _log_sigmoid(x):
    return jnp.minimum(x, 0.0) - jnp.log(1.0 + jnp.exp(-jnp.abs(x)))


def _gelu_tanh(x):
    return 0.5 * x * (1.0 + jnp.tanh(math.sqrt(2.0 / math.pi) * (x + 0.044715 * x * x * x)))


def _split3(x):
    hi = x.astype(BF16)
    r = x - hi.astype(F32)
    mid = r.astype(BF16)
    lo = (r - mid.astype(F32)).astype(BF16)
    return hi, mid, lo


def _dot(a, b):
    return jnp.dot(a, b, preferred_element_type=F32)


def _dot_nt(a, b):
    return lax.dot_general(a, b, (((1,), (1,)), ((), ())), preferred_element_type=F32)


def _dot_tn(a, b):
    return lax.dot_general(a, b, (((0,), (0,)), ((), ())), preferred_element_type=F32)


def _dot_exact_01(x, m01):
    hi, mid, lo = _split3(x)
    return _dot(hi, m01) + _dot(mid, m01) + _dot(lo, m01)


def _rms_rows(x):
    return x * lax.rsqrt(jnp.mean(x * x, axis=-1, keepdims=True) + EPS)


def _forget_gate(af, lbv):
    e = jnp.exp(-jnp.abs(af))
    log_sig = jnp.minimum(af, 0.0) - jnp.log(1.0 + e)
    a = lbv[0:1]
    b = lbv[1:2] + log_sig
    log_f = jnp.maximum(a, b) + jnp.log(1.0 + jnp.exp(-jnp.abs(a - b)))
    k = lbv[2:3] * (jnp.where(af >= 0.0, e, 1.0) / (1.0 + e))
    return log_f, k


def _chunk_cumsum(x, c):
    pos = lax.broadcasted_iota(jnp.int32, x.shape, 0) % c
    s = 1
    while s < c:
        x = x + jnp.where(pos >= s, pltpu.roll(x, s, 0), 0.0)
        s *= 2
    return x


def _head_rms(o, dv):
    w = o.shape[1]
    r = lax.broadcasted_iota(jnp.int32, (w, w), 0)
    cidx = lax.broadcasted_iota(jnp.int32, (w, w), 1)
    ones_blk = jnp.where(r // dv == cidx // dv, 1.0, 0.0).astype(BF16)
    return o * lax.rsqrt(_dot((o * o).astype(BF16), ones_blk) * (1.0 / dv) + EPS)


def _project_all(h, wint_ref, lbv_ref, walpha_ref, balpha_ref, chunk, p_scr):
    def proj(name):
        r0, n = W_SEGMENTS[name]
        return _dot_nt(h, wint_ref[r0:r0 + n, :])

    def cum(x):
        return x if chunk == 1 else _chunk_cumsum(x, chunk)

    brcu = proj("br|cu")
    p_scr[:, RB:RB + 384] = _silu(brcu[:, 0:384])
    p_scr[:, CU:CU + 256] = brcu[:, 384:640]
    aqf = proj("aq|af")
    p_scr[:, QA:QA + 384] = _silu(aqf[:, 0:384])
    log_f, ka = _forget_gate(aqf[:, 384:768], lbv_ref[...])
    p_scr[:, FA:FA + 384] = cum(log_f)
    p_scr[:, KA:KA + 384] = ka
    aig = proj("ai|ag")
    p_scr[:, VA:VA + 384] = aig[:, 0:384]
    p_scr[:, GA:GA + 384] = _silu(aig[:, 384:768])
    p_scr[:, QB:QB + 256] = proj("bq") * (B_DK ** -0.5)
    kb = proj("bk")
    p_scr[:, KB:KB + 256] = jnp.where(lax.broadcasted_iota(jnp.int32, kb.shape, 1) < B_H * B_DK, kb, 0.0)
    bvlr = proj("bv|lr")
    p_scr[:, VB:VB + 384] = bvlr[:, 0:384]
    z = _dot(bvlr[:, 384:512].astype(BF16), walpha_ref[...]) + balpha_ref[...]
    p_scr[:, LB:LB + 256] = cum(_log_sigmoid(z) * (1.0 / B_TAU))


def _layer_spec(arr, l):
    nd = arr.ndim - 1
    return pl.BlockSpec((None,) + arr.shape[1:], lambda *_, _n=nd: (l,) + (0,) * _n,
                        pipeline_mode=pl.Buffered(1))


def _ada_kernel(c_ref, w_ref, b_ref, o_ref):
    sc = _silu(c_ref[...]).astype(BF16)
    half = w_ref.shape[2] // 2
    for n0 in (0, half):
        o_ref[0, :, n0:n0 + half] = _dot(sc, w_ref[0, :, n0:n0 + half].astype(BF16)) + b_ref[0, :, n0:n0 + half]


def _ada_call(c_all, ada_w, ada_b):
    n = c_all.shape[0]
    tn = 1536
    return pl.pallas_call(
        _ada_kernel,
        grid=(DEPTH, 6 * D // tn),
        in_specs=[
            pl.BlockSpec((n, D), lambda l, j: (0, 0)),
            pl.BlockSpec((1, D, tn), lambda l, j: (l, 0, j)),
            pl.BlockSpec((1, 1, tn), lambda l, j: (l, 0, j)),
        ],
        out_specs=pl.BlockSpec((1, n, tn), lambda l, j: (l, 0, j)),
        out_shape=jax.ShapeDtypeStruct((DEPTH, n, 6 * D), F32),
        compiler_params=pltpu.CompilerParams(
            dimension_semantics=("arbitrary", "arbitrary"), vmem_limit_bytes=VMEM_LIMIT),
        name="ada_mod",
    )(c_all, ada_w, ada_b.reshape(DEPTH, 1, 6 * D))


def _mixer_prompt_kernel(x_ref, mod_ref, g1_ref, win_ref, lbv_ref, walpha_ref, balpha_ref,
                         hg_ref, gg_ref, abar_ref, bb_ref, cc_ref, dsk_ref, wglu_ref, bglu_ref,
                         wout_ref,
                         xo_ref, sta_ref, stb_ref, s5r_ref, s5i_ref,
                         h_scr, p_scr, o_scr, xr_scr, xi_scr, t_scr,
                         qt_scr, kt_scr, qd_scr, kd_scr, v_scr, of_scr, dec_scr, sc_scr):
    nb = x_ref.shape[0]
    c = CHUNK
    i = pl.program_id(0)

    @pl.when(i == 0)
    def _():
        sta_ref[...] = jnp.zeros_like(sta_ref)
        stb_ref[...] = jnp.zeros_like(stb_ref)
        s5r_ref[...] = jnp.zeros_like(s5r_ref)
        s5i_ref[...] = jnp.zeros_like(s5i_ref)

    for b in range(nb):
        hb = _rms_rows(x_ref[b]) * g1_ref[...]
        hb = hb * (1.0 + mod_ref[b:b + 1, D:2 * D]) + mod_ref[b:b + 1, 0:D]
        h_scr[b * c:(b + 1) * c, :] = hb.astype(BF16)

    halves = C_W // LANES

    def s5_input():
        for b in range(nb):
            for j in range(halves):
                t_scr[j, b:b + nb * c:nb, :] = p_scr[b * c:(b + 1) * c, CU + LANES * j:CU + LANES * (j + 1)]
        u_bf = jnp.concatenate([t_scr[j] for j in range(halves)], axis=1).astype(BF16)
        xr_scr[...] = _dot(u_bf, bb_ref[:, 0:C_S])
        xi_scr[...] = _dot(u_bf, bb_ref[:, C_S:2 * C_S])

    def operands(groups, rng):
        for b in range(nb):
            rs = slice(b * c, (b + 1) * c)
            for g in groups:
                qc, kc, bc = BLOCKS[g][0:3]
                ls = slice(g * LANES, (g + 1) * LANES)
                q = p_scr[rs, qc:qc + LANES]
                k = p_scr[rs, kc:kc + LANES]
                bcum = p_scr[rs, bc:bc + LANES]
                b_mid = bcum[c // 2 - 1:c // 2, :]
                b_end = bcum[c - 1:c, :]
                d = bcum - b_mid
                rng = jnp.maximum(rng, jnp.maximum(jnp.abs(d[0:1, :]), jnp.abs(d[c - 1:c, :])))
                qt_scr[rs, ls] = (q * jnp.exp(d)).astype(BF16)
                kt_scr[rs, ls] = (k * jnp.exp(-d)).astype(BF16)
                qd_scr[rs, ls] = (q * jnp.exp(bcum)).astype(BF16)
                kd_scr[rs, ls] = (k * jnp.exp(b_end - bcum)).astype(BF16)
                dec_scr[b, :, ls] = jnp.broadcast_to(jnp.exp(b_end), (SUBLANES, LANES))
        return rng

    _project_all(h_scr[...], win_ref, lbv_ref, walpha_ref, balpha_ref, c, p_scr)
    s5_input()
    v_scr[:, 0:384] = p_scr[:, VA:VA + 384].astype(BF16)
    v_scr[:, 384:768] = p_scr[:, VB:VB + 384].astype(BF16)
    rng = operands(range(N_LG), jnp.zeros((1, LANES), F32))
    in_range = jnp.max(rng) <= EXP_RANGE

    lane = lax.broadcasted_iota(jnp.int32, (c, LANES), 1)

    def scores_direct(r0, qc, kc, bc, nh, dk):
        q = p_scr[pl.ds(r0, c), qc:qc + LANES]
        bcum = p_scr[pl.ds(r0, c), bc:bc + LANES]
        krow = lax.broadcasted_iota(jnp.int32, (LANES, nh * c), 0)
        kcol = lax.broadcasted_iota(jnp.int32, (LANES, nh * c), 1)
        sub = lax.broadcasted_iota(jnp.int32, (SUBLANES, LANES), 0)

        def row_of(col0, s):
            r8 = pl.ds(pl.multiple_of(r0 + (s // SUBLANES) * SUBLANES, SUBLANES), SUBLANES)
            return jnp.sum(jnp.where(sub == s % SUBLANES, p_scr[r8, col0:col0 + LANES], 0.0),
                           axis=0, keepdims=True)

        def body(s, sc):
            ks = row_of(kc, s)
            bs = row_of(bc, s)
            w = q * ks * jnp.exp(jnp.minimum(bcum - bs, 0.0))
            place = jnp.where(kcol == (krow // dk) * c + s, 1.0, 0.0).astype(BF16)
            return sc + _dot_exact_01(w, place)

        return lax.fori_loop(0, c, body, jnp.zeros((c, nh * c), F32))

    def stage_scores(b, r0, direct):
        rs = pl.ds(r0, c)
        for g, (qc, kc, bc, vc, nh, dk) in enumerate(BLOCKS):
            ls = slice(g * LANES, (g + 1) * LANES)
            if direct:
                sc = scores_direct(r0, qc, kc, bc, nh, dk)
            else:
                kt = kt_scr[rs, ls]
                kcat = jnp.concatenate(
                    [jnp.where(lane // dk == h, kt, jnp.zeros_like(kt)) for h in range(nh)], axis=0)
                sc = _dot_nt(qt_scr[rs, ls], kcat)
            row = lax.broadcasted_iota(jnp.int32, (c, nh * c), 0)
            col = lax.broadcasted_iota(jnp.int32, (c, nh * c), 1)
            sc_scr[rs, vc:vc + nh * c] = jnp.where(col % c <= row, sc, 0.0).astype(BF16)

    def stage_readout(b, r0):
        rs = pl.ds(r0, c)
        for g, (_, _, _, vc, nh, dk) in enumerate(BLOCKS):
            ls = slice(g * LANES, (g + 1) * LANES)
            vw = nh * 64
            st_ref = sta_ref if g < 3 else stb_ref
            v0 = vc if g < 3 else vc - 384
            v = v_scr[rs, vc:vc + vw]
            vlane = lax.broadcasted_iota(jnp.int32, (c, vw), 1)
            vcat = jnp.concatenate(
                [jnp.where(vlane // 64 == h, v, jnp.zeros_like(v)) for h in range(nh)], axis=0)
            st = st_ref[b, v0:v0 + vw, :]
            of_scr[rs, vc:vc + vw] = (_dot(sc_scr[rs, vc:vc + nh * c], vcat)
                                      + _dot_nt(qd_scr[rs, ls], st.astype(BF16)))

    def stage_update(b, r0):
        rs = pl.ds(r0, c)
        for g, (_, _, _, vc, nh, dk) in enumerate(BLOCKS):
            ls = slice(g * LANES, (g + 1) * LANES)
            vw = nh * 64
            st_ref = sta_ref if g < 3 else stb_ref
            v0 = vc if g < 3 else vc - 384
            upd = _dot_tn(v_scr[rs, vc:vc + vw], kd_scr[rs, ls])
            srow = lax.broadcasted_iota(jnp.int32, (vw, LANES), 0)
            scol = lax.broadcasted_iota(jnp.int32, (vw, LANES), 1)
            st_ref[b, v0:v0 + vw, :] = (st_ref[b, v0:v0 + vw, :] * dec_scr[b, 0:1, ls]
                                        + jnp.where(srow // 64 == scol // dk, upd, 0.0))

    @pl.when(in_range)
    def _():
        for b in range(nb):
            stage_scores(b, b * c, False)
        for b in range(nb):
            stage_readout(b, b * c)
        for b in range(nb):
            stage_update(b, b * c)

    @pl.when(jnp.logical_not(in_range))
    def _():
        def body(b, carry):
            r0 = pl.multiple_of(b * c, c)
            stage_scores(b, r0, True)
            stage_readout(b, r0)
            stage_update(b, r0)
            return carry
        lax.fori_loop(0, nb, body, 0)

    oa = _head_rms(of_scr[:, 0:384], A_DV) * hg_ref[...] * p_scr[:, GA:GA + 384]
    o_scr[:, 0:384] = oa.astype(BF16)
    ob = _head_rms(of_scr[:, 384:768], B_DV) * gg_ref[...] * p_scr[:, RB:RB + 384]
    o_scr[:, 384:768] = ob.astype(BF16)

    ar = abar_ref[0:1, :]
    ai = abar_ref[1:2, :]

    def scan_t(t, carry):
        sr, si = carry
        ts = pl.ds(pl.multiple_of(t * nb, nb), nb)
        nr = ar * sr - ai * si + xr_scr[ts, :]
        ni = ar * si + ai * sr + xi_scr[ts, :]
        xr_scr[ts, :] = nr
        xi_scr[ts, :] = ni
        return nr, ni

    sr, si = lax.fori_loop(0, c, scan_t, (s5r_ref[...], s5i_ref[...]), unroll=2)
    s5r_ref[...] = sr
    s5i_ref[...] = si
    u = jnp.concatenate([t_scr[j] for j in range(halves)], axis=1)
    half = nb * c // 2
    y = jnp.concatenate(
        [_dot(xr_scr[r:r + half, :].astype(BF16), cc_ref[0:C_S, :])
         + _dot(xi_scr[r:r + half, :].astype(BF16), cc_ref[C_S:2 * C_S, :]) for r in (0, half)], axis=0)
    y = y + dsk_ref[...] * u
    z = _gelu_tanh(y)
    oc = z * _sigmoid(_dot(z.astype(BF16), wglu_ref[...]) + bglu_ref[...])
    for j in range(halves):
        t_scr[j] = oc[:, LANES * j:LANES * (j + 1)]
    for b in range(nb):
        for j in range(halves):
            o_scr[b * c:(b + 1) * c, 768 + LANES * j:768 + LANES * (j + 1)] = (
                t_scr[j, b:b + nb * c:nb, :].astype(BF16))

    o_all = o_scr[...]
    for n0 in (0, D // 2):
        out = _dot(o_all, wout_ref[:, n0:n0 + D // 2])
        for b in range(nb):
            xo_ref[b, :, n0:n0 + D // 2] = (x_ref[b, :, n0:n0 + D // 2]
                                            + mod_ref[b:b + 1, 2 * D + n0:2 * D + n0 + D // 2]
                                            * out[b * c:(b + 1) * c, :])


def _mixer_prompt_call(l, x, mod, ns, lp):
    nb, seq, _ = x.shape
    c = CHUNK
    rows = nb * c
    consts = [lp[k] for k in ("g1", "w_in", "lbv", "w_alpha", "b_alpha", "hgrn_g", "gla_g", "abar", "bb", "cc",
                              "d_skip", "w_glu", "b_glu", "w_out")]
    in_specs = ([pl.BlockSpec((nb, c, D), lambda i: (0, i, 0)),
                 pl.BlockSpec((None, nb, 6 * D), lambda i: (l, ns // nb, 0), pipeline_mode=pl.Buffered(1))]
                + [_layer_spec(a, l) for a in consts])
    out_shape = (jax.ShapeDtypeStruct((nb, seq, D), F32),
                 jax.ShapeDtypeStruct((nb, 384, LANES), F32),
                 jax.ShapeDtypeStruct((nb, 384, LANES), F32),
                 jax.ShapeDtypeStruct((nb, C_S), F32),
                 jax.ShapeDtypeStruct((nb, C_S), F32))
    out_specs = (pl.BlockSpec((nb, c, D), lambda i: (0, i, 0)),
                 pl.BlockSpec((nb, 384, LANES), lambda i: (0, 0, 0)),
                 pl.BlockSpec((nb, 384, LANES), lambda i: (0, 0, 0)),
                 pl.BlockSpec((nb, C_S), lambda i: (0, 0)),
                 pl.BlockSpec((nb, C_S), lambda i: (0, 0)))
    scratch = [pltpu.VMEM((rows, D), BF16), pltpu.VMEM((rows, PW), F32), pltpu.VMEM((rows, D), BF16),
               pltpu.VMEM((rows, C_S), F32), pltpu.VMEM((rows, C_S), F32),
               pltpu.VMEM((C_W // LANES, rows, LANES), F32),
               pltpu.VMEM((rows, LG_W), BF16), pltpu.VMEM((rows, LG_W), BF16),
               pltpu.VMEM((rows, LG_W), BF16), pltpu.VMEM((rows, LG_W), BF16),
               pltpu.VMEM((rows, VO_W), BF16), pltpu.VMEM((rows, VO_W), F32),
               pltpu.VMEM((nb, SUBLANES, LG_W), F32), pltpu.VMEM((rows, VO_W), BF16)]
    return pl.pallas_call(
        _mixer_prompt_kernel,
        grid=(seq // c,),
        in_specs=in_specs,
        out_specs=out_specs,
        out_shape=out_shape,
        scratch_shapes=scratch,
        compiler_params=pltpu.CompilerParams(
            dimension_semantics=("arbitrary",), vmem_limit_bytes=VMEM_LIMIT),
        name="mixer_prompt",
    )(x, mod, *consts)


def _step_head(s_ref, so_ref, t_scr, i, v_t):
    acc = jnp.zeros(v_t.shape, F32)
    for k in range(s_ref.shape[0]):
        new = s_ref[k] * t_scr[1, i, k:k + 1, :] + t_scr[2, i, k:k + 1, :] * v_t
        so_ref[k] = new
        acc = acc + new * t_scr[0, i, k:k + 1, :]
    return acc


def _mixer_sample_kernel(x_ref, mod_ref, g1_ref, win_ref, lbv_ref, walpha_ref, balpha_ref,
                         hgc_ref, ggc_ref, abarc_ref, bbt_ref, cc_ref, dsk_ref, wglu_ref, bglu_ref,
                         wout_ref, sa_ref, sb_ref, s5r_ref, s5i_ref, pa_ref, pb_ref, pr_ref, pi_ref,
                         xo_ref, sao_ref, sbo_ref, s5ro_ref, s5io_ref,
                         p_scr, at_scr, bt_scr, vt_scr, gate_scr, oh_scr, oc_scr):
    del pa_ref, pb_ref, pr_ref, pi_ref
    i = pl.program_id(0)

    @pl.when(i == 0)
    def _():
        h = _rms_rows(x_ref[...]) * g1_ref[...]
        h = (h * (1.0 + mod_ref[:, D:2 * D]) + mod_ref[:, 0:D]).astype(BF16)
        _project_all(h, win_ref, lbv_ref, walpha_ref, balpha_ref, 1, p_scr)
        for g, (qc, kc, bc, _, nh, dk) in enumerate(BLOCKS):
            t_scr, h0 = (at_scr, 2 * g) if g < 3 else (bt_scr, 4 * (g - 3))
            qt = p_scr[:, qc:qc + LANES].T
            dt = jnp.exp(p_scr[:, bc:bc + LANES]).T
            kt = p_scr[:, kc:kc + LANES].T
            for hh in range(nh):
                t_scr[0, h0 + hh] = qt[hh * dk:(hh + 1) * dk, :]
                t_scr[1, h0 + hh] = dt[hh * dk:(hh + 1) * dk, :]
                t_scr[2, h0 + hh] = kt[hh * dk:(hh + 1) * dk, :]
        for p in range(A_H // 2):
            for grp, vcol, gcol in ((0, VA, GA), (1, VB, RB)):
                vt = p_scr[:, vcol + LANES * p:vcol + LANES * (p + 1)].T
                gt = p_scr[:, gcol + LANES * p:gcol + LANES * (p + 1)].T
                for hh in range(2):
                    vt_scr[grp, 2 * p + hh] = vt[64 * hh:64 * (hh + 1), :]
                    gate_scr[grp, 2 * p + hh] = gt[64 * hh:64 * (hh + 1), :]
        u = p_scr[:, CU:CU + C_W]
        xri = _dot_nt(bbt_ref[...], u.astype(BF16))
        ar = abarc_ref[:, 0:1]
        ai = abarc_ref[:, 1:2]
        s0r = s5r_ref[...]
        s0i = s5i_ref[...]
        sr = ar * s0r - ai * s0i + xri[0:C_S, :]
        si = ar * s0i + ai * s0r + xri[C_S:2 * C_S, :]
        s5ro_ref[...] = sr
        s5io_ref[...] = si
        y = (_dot_tn(sr.astype(BF16), cc_ref[0:C_S, :]) + _dot_tn(si.astype(BF16), cc_ref[C_S:2 * C_S, :])
             + dsk_ref[...] * u)
        z = _gelu_tanh(y)
        oc_scr[...] = z * _sigmoid(_dot(z.astype(BF16), wglu_ref[...]) + bglu_ref[...])

    def norm_gate(o, gain_col, gate):
        return o * lax.rsqrt(jnp.mean(o * o, axis=0, keepdims=True) + EPS) * gain_col * gate

    oa = _step_head(sa_ref, sao_ref, at_scr, i, vt_scr[0, i])
    oh_scr[0, i] = norm_gate(oa, hgc_ref[...], gate_scr[0, i])
    ob = _step_head(sb_ref, sbo_ref, bt_scr, i, vt_scr[1, i])
    oh_scr[1, i] = norm_gate(ob, ggc_ref[...], gate_scr[1, i])

    @pl.when(i == A_H - 1)
    def _():
        parts = []
        for grp in range(2):
            for p in range(A_H // 2):
                two = jnp.concatenate([oh_scr[grp, 2 * p], oh_scr[grp, 2 * p + 1]], axis=0)
                parts.append(two.T)
        o = jnp.concatenate(parts + [oc_scr[...]], axis=1).astype(BF16)
        xo_ref[...] = x_ref[...] + mod_ref[:, 2 * D:3 * D] * _dot(o, wout_ref[...])


def _mixer_sample_call(l, x, mod, lp, states, prev):
    n = x.shape[0]
    consts = [lp[k] for k in ("g1", "w_in", "lbv", "w_alpha", "b_alpha", "hgrn_gc", "gla_gc", "abar_c", "bb_t",
                              "cc", "d_skip", "w_glu", "b_glu", "w_out")]
    state_specs = [pl.BlockSpec((None, None, A_DK, A_DV, n), lambda i: (l, i, 0, 0, 0)),
                   pl.BlockSpec((None, None, B_DK, B_DV, n), lambda i: (l, i, 0, 0, 0)),
                   pl.BlockSpec((None, C_S, n), lambda i: (l, 0, 0)),
                   pl.BlockSpec((None, C_S, n), lambda i: (l, 0, 0))]
    in_specs = ([pl.BlockSpec((n, D), lambda i: (0, 0), pipeline_mode=pl.Buffered(1)),
                 pl.BlockSpec((None, n, 6 * D), lambda i: (l, 0, 0), pipeline_mode=pl.Buffered(1))]
                + [_layer_spec(a, l) for a in consts] + state_specs
                + [pl.BlockSpec(memory_space=pl.ANY)] * 4)
    n_in = 2 + len(consts) + 4
    out_shape = (jax.ShapeDtypeStruct((n, D), F32),) + tuple(jax.ShapeDtypeStruct(s.shape, F32) for s in states)
    out_specs = (pl.BlockSpec((n, D), lambda i: (0, 0)),) + tuple(state_specs)
    scratch = [pltpu.VMEM((n, PW), F32), pltpu.VMEM((3, A_H, A_DK, n), F32), pltpu.VMEM((3, 8, B_DK, n), F32),
               pltpu.VMEM((2, A_H, 64, n), F32), pltpu.VMEM((2, A_H, 64, n), F32),
               pltpu.VMEM((2, A_H, 64, n), F32), pltpu.VMEM((n, C_W), F32)]
    return pl.pallas_call(
        _mixer_sample_kernel,
        grid=(A_H,),
        in_specs=in_specs,
        out_specs=out_specs,
        out_shape=out_shape,
        scratch_shapes=scratch,
        input_output_aliases={n_in + j: 1 + j for j in range(4)},
        compiler_params=pltpu.CompilerParams(
            dimension_semantics=("arbitrary",), vmem_limit_bytes=VMEM_LIMIT),
        name="mixer_sample",
    )(x, mod, *consts, *states, *prev)


def _top2_gates(logits, gidx, lane):
    neg = jnp.float32(-jnp.inf)
    big = jnp.float32(1 << 20)
    own = jnp.sum(jnp.where(lane == gidx, logits, 0.0), axis=1, keepdims=True)
    p_group = 1.0 / jnp.sum(jnp.where(lane < N_GROUPS, jnp.exp(logits - own), 0.0), axis=1, keepdims=True)
    e0 = N_GROUPS + N_PER_GROUP * gidx
    el = jnp.where((lane >= e0) & (lane < e0 + N_PER_GROUP), logits, neg)
    v1 = jnp.max(el, axis=1, keepdims=True)
    i1 = jnp.min(jnp.where(el == v1, lane, big), axis=1, keepdims=True)
    el2 = jnp.where(lane == i1, neg, el)
    v2 = jnp.max(el2, axis=1, keepdims=True)
    i2 = jnp.min(jnp.where(el2 == v2, lane, big), axis=1, keepdims=True)
    ex = jnp.exp(v2 - v1)
    w1 = 1.0 / (1.0 + ex)
    return jnp.where(lane == i1, p_group * w1, 0.0) + jnp.where(lane == i2, p_group * (ex * w1), 0.0)


def _group_experts(h, gates, lane, lane0, wg_ref, wu_ref, wd_ref):
    y = None
    for e in range(N_PER_GROUP):
        hid = _silu(_dot(h, wg_ref[e])) * _dot(h, wu_ref[e])
        gcol = jnp.sum(jnp.where(lane == lane0 + e, gates, 0.0), axis=1, keepdims=True)
        ye = _dot((hid * gcol).astype(BF16), wd_ref[e])
        y = ye if y is None else y + ye
    return y


def _moe_dense_kernel(x_ref, mod_ref, g2_ref, wr_ref, br_ref, wg_ref, wu_ref, wd_ref, fg_ref,
                      o_ref, h_scr, gate_scr, acc_scr, *, final_norm):
    g = pl.program_id(1)
    tm = x_ref.shape[0]
    lane = lax.broadcasted_iota(jnp.int32, (tm, LANES), 1).astype(F32)

    @pl.when(g == 0)
    def _():
        h = _rms_rows(x_ref[...]) * g2_ref[...]
        h = (h * (1.0 + mod_ref[:, 4 * D:5 * D]) + mod_ref[:, 3 * D:4 * D]).astype(BF16)
        h_scr[...] = h
        logits = _dot(h, wr_ref[...]) + br_ref[...]
        gl = jnp.where(lane < N_GROUPS, logits, -jnp.inf)
        gmax = jnp.max(gl, axis=1, keepdims=True)
        gidx = jnp.min(jnp.where(gl == gmax, lane, jnp.float32(1 << 20)), axis=1, keepdims=True)
        gate_scr[...] = _top2_gates(logits, gidx, lane)
        acc_scr[...] = jnp.zeros_like(acc_scr)

    lane0 = (N_GROUPS + N_PER_GROUP * g).astype(F32)
    acc_scr[...] += _group_experts(h_scr[...], gate_scr[...], lane, lane0, wg_ref, wu_ref, wd_ref)

    @pl.when(g == N_GROUPS - 1)
    def _():
        xn = x_ref[...] + mod_ref[:, 5 * D:6 * D] * acc_scr[...]
        if final_norm:
            xn = _rms_rows(xn) * fg_ref[...]
        o_ref[...] = xn


def _expert_specs(lp, l):
    def spec(arr):
        return pl.BlockSpec((None, N_PER_GROUP) + arr.shape[2:], lambda i, g, *_: (l, g, 0, 0))
    return [spec(lp["w_gate"]), spec(lp["w_up"]), spec(lp["w_down"])]


def _moe_dense_call(l, x2, mod, lp, final_g, *, final_norm):
    tm = x2.shape[0]
    kern = functools.partial(_moe_dense_kernel, final_norm=final_norm)
    return pl.pallas_call(
        kern,
        grid=(1, N_GROUPS),
        in_specs=[pl.BlockSpec((tm, D), lambda i, g: (0, 0)),
                  pl.BlockSpec((None, tm, 6 * D), lambda i, g: (l, 0, 0)),
                  _layer_spec(lp["g2"], l), _layer_spec(lp["w_router"], l), _layer_spec(lp["b_router"], l)]
        + _expert_specs(lp, l) + [pl.BlockSpec((1, D), lambda i, g: (0, 0))],
        out_specs=pl.BlockSpec((tm, D), lambda i, g: (0, 0)),
        out_shape=jax.ShapeDtypeStruct((tm, D), F32),
        scratch_shapes=[pltpu.VMEM((tm, D), BF16), pltpu.VMEM((tm, LANES), F32), pltpu.VMEM((tm, D), F32)],
        compiler_params=pltpu.CompilerParams(
            dimension_semantics=("arbitrary", "arbitrary"), vmem_limit_bytes=VMEM_LIMIT),
        name="moe_sample",
    )(x2, mod, lp["g2"], lp["w_router"], lp["b_router"], lp["w_gate"], lp["w_up"], lp["w_down"], final_g)


def _route_kernel(x_ref, mod_ref, g2_ref, wrt_ref, brc_ref, tri_ref, pos_ref, offs_ref, *, tiles_per_seq):
    tt = x_ref.shape[0]
    b = pl.program_id(0) // tiles_per_seq
    h = _rms_rows(x_ref[...]) * g2_ref[...]
    h = (h * (1.0 + mod_ref[pl.ds(b, 1), 4 * D:5 * D]) + mod_ref[pl.ds(b, 1), 3 * D:4 * D]).astype(BF16)
    lt = _dot_nt(wrt_ref[...], h) + brc_ref[...]
    row = lax.broadcasted_iota(jnp.int32, (SUBLANES, tt), 0).astype(F32)
    gl = jnp.where(row < N_GROUPS, lt[0:SUBLANES, :], -jnp.inf)
    gmax = jnp.max(gl, axis=0, keepdims=True)
    gidx = jnp.min(jnp.where(gl == gmax, row, jnp.float32(1 << 20)), axis=0, keepdims=True)
    onehot = jnp.where(row == gidx, 1.0, 0.0)
    rank = _dot(onehot.astype(BF16), tri_ref[...])
    counts = jnp.sum(onehot, axis=1, keepdims=True)
    row1 = lax.broadcasted_iota(jnp.int32, (SUBLANES, 1), 0)
    off = jnp.zeros((SUBLANES, 1), F32)
    for g in range(N_GROUPS - 1):
        off = off + jnp.where(row1 > g, counts[g:g + 1, :], 0.0)
    pos_ref[0] = (jnp.sum(onehot * (rank + off), axis=0, keepdims=True) * float(SUBLANES)).astype(jnp.int32)
    lane = lax.broadcasted_iota(jnp.int32, (1, LANES), 1)
    offs = jnp.where(lane == N_GROUPS, float(tt), 0.0)
    for g in range(1, N_GROUPS):
        offs = offs + jnp.where(lane == g, off[g:g + 1, :], 0.0)
    offs_ref[0] = offs.astype(jnp.int32)


def _route_call(l, x2, mod, ns, nb, lp, *, tt):
    t = x2.shape[0]
    nt = t // tt
    tiles_per_seq = nt // nb
    tri = jnp.asarray(np.triu(np.ones((tt, tt), np.float32), 1), BF16)
    kern = functools.partial(_route_kernel, tiles_per_seq=tiles_per_seq)
    return pl.pallas_call(
        kern,
        grid=(nt,),
        in_specs=[pl.BlockSpec((tt, D), lambda i: (i, 0)),
                  pl.BlockSpec((None, nb, 6 * D), lambda i: (l, ns // nb, 0), pipeline_mode=pl.Buffered(1)),
                  _layer_spec(lp["g2"], l), _layer_spec(lp["w_router_t"], l), _layer_spec(lp["b_router_c"], l),
                  pl.BlockSpec((tt, tt), lambda i: (0, 0), pipeline_mode=pl.Buffered(1))],
        out_specs=(pl.BlockSpec((1, 1, tt), lambda i: (i, 0, 0)),
                   pl.BlockSpec((1, 1, LANES), lambda i: (i, 0, 0))),
        out_shape=(jax.ShapeDtypeStruct((nt, 1, tt), jnp.int32),
                   jax.ShapeDtypeStruct((nt, 1, LANES), jnp.int32)),
        compiler_params=pltpu.CompilerParams(
            dimension_semantics=("arbitrary",), vmem_limit_bytes=VMEM_LIMIT),
        name="moe_route",
    )(x2, mod, lp["g2"], lp["w_router_t"], lp["b_router_c"], tri)


def _top2_gates_t(lt, gidx):
    rows = -(-(N_GROUPS + N_EXP) // SUBLANES) * SUBLANES
    n = lt.shape[1]
    neg = jnp.float32(-jnp.inf)
    big = jnp.float32(1 << 20)
    l = lt[0:rows, :]
    row = lax.broadcasted_iota(jnp.int32, (rows, n), 0).astype(F32)
    own = jnp.sum(jnp.where(row == gidx, l, 0.0), axis=0, keepdims=True)
    p_group = 1.0 / jnp.sum(jnp.where(row < N_GROUPS, jnp.exp(l - own), 0.0), axis=0, keepdims=True)
    e0 = N_GROUPS + N_PER_GROUP * gidx
    el = jnp.where((row >= e0) & (row < e0 + N_PER_GROUP), l, neg)
    v1 = jnp.max(el, axis=0, keepdims=True)
    i1 = jnp.min(jnp.where(el == v1, row, big), axis=0, keepdims=True)
    el2 = jnp.where(row == i1, neg, el)
    v2 = jnp.max(el2, axis=0, keepdims=True)
    i2 = jnp.min(jnp.where(el2 == v2, row, big), axis=0, keepdims=True)
    ex = jnp.exp(v2 - v1)
    w1 = 1.0 / (1.0 + ex)
    gates = jnp.where(row == i1, p_group * w1, 0.0) + jnp.where(row == i2, p_group * (ex * w1), 0.0)
    return jnp.concatenate([gates, jnp.zeros((LANES - rows, n), F32)], axis=0)


def _moe_sorted_kernel(pos_sm, offs_sm, x_ref, mod_ref, g2_ref, wrt_ref, brc_ref, wg_ref, wu_ref, wd_ref, fg_ref,
                       o_ref, a_scr, b_scr, hs_scr, gate_scr, acc_scr, *, tiles_per_seq, final_norm):
    i = pl.program_id(0)
    g = pl.program_id(1)
    tt = x_ref.shape[0]
    base = i * tt
    b = i // tiles_per_seq

    def mod(k):
        return mod_ref[pl.ds(b, 1), k * D:(k + 1) * D]

    def tile_rows(t):
        return pl.ds(pl.multiple_of(t * SUBLANES, SUBLANES), SUBLANES)

    def slot_rows(t):
        return pl.ds(pl.multiple_of(pos_sm[base + t], SUBLANES), SUBLANES)

    def to_token_tiles(v):
        for s in range(SUBLANES):
            a_scr[s:tt * SUBLANES:SUBLANES, :] = v[:, s * LANES:(s + 1) * LANES]

    def from_token_tiles():
        return jnp.concatenate([b_scr[s:tt * SUBLANES:SUBLANES, :] for s in range(SUBLANES)], axis=1)

    @pl.when(g == 0)
    def _():
        h = _rms_rows(x_ref[...]) * g2_ref[...]
        to_token_tiles(h * (1.0 + mod(4)) + mod(3))

        def scatter(t, carry):
            b_scr[slot_rows(t), :] = a_scr[tile_rows(t), :]
            return carry
        lax.fori_loop(0, tt, scatter, 0, unroll=32)
        hs = from_token_tiles().astype(BF16)
        hs_scr[0:tt, :] = hs
        hs_scr[tt:tt + MOE_BLOCK, :] = jnp.zeros((MOE_BLOCK, D), BF16)
        lt = _dot_nt(wrt_ref[...], hs) + brc_ref[...]
        slot = lax.broadcasted_iota(jnp.int32, (1, tt), 1)
        gidx = jnp.zeros((1, tt), F32)
        for gg in range(1, N_GROUPS):
            gidx = gidx + jnp.where(slot >= offs_sm[i * LANES + gg], 1.0, 0.0)
        gate_scr[0:tt, :] = _top2_gates_t(lt, gidx).T
        gate_scr[tt:tt + MOE_BLOCK, :] = jnp.zeros((MOE_BLOCK, LANES), F32)
        acc_scr[...] = jnp.zeros_like(acc_scr)

    lo = offs_sm[i * LANES + g]
    hi = offs_sm[i * LANES + g + 1]
    start = (lo // BF16_ROWS) * BF16_ROWS
    span = jnp.where(hi > lo, hi - start, 0)
    lane0 = (N_GROUPS + N_PER_GROUP * g).astype(F32)

    def run_block(r0, m):
        rs = pl.ds(pl.multiple_of(r0, BF16_ROWS), m)
        blk_lane = lax.broadcasted_iota(jnp.int32, (m, LANES), 1).astype(F32)
        acc_scr[rs, :] += _group_experts(hs_scr[rs, :], gate_scr[rs, :], blk_lane, lane0, wg_ref, wu_ref, wd_ref)

    n_full = span // MOE_BLOCK

    def full_block(k, carry):
        run_block(start + k * MOE_BLOCK, MOE_BLOCK)
        return carry
    lax.fori_loop(0, n_full, full_block, 0)
    rem = span - n_full * MOE_BLOCK
    for m, below in zip(MOE_TAIL_BLOCKS, MOE_TAIL_BLOCKS[1:] + (0,)):
        @pl.when((rem > below) & (rem <= m))
        def _(m=m):
            run_block(start + n_full * MOE_BLOCK, m)

    @pl.when(g == N_GROUPS - 1)
    def _():
        to_token_tiles(acc_scr[0:tt, :])

        def gather(t, carry):
            b_scr[tile_rows(t), :] = a_scr[slot_rows(t), :]
            return carry
        lax.fori_loop(0, tt, gather, 0, unroll=32)
        xn = x_ref[...] + mod(5) * from_token_tiles()
        if final_norm:
            xn = _rms_rows(xn) * fg_ref[...]
        o_ref[...] = xn


def _moe_sorted_call(l, x2, mod, ns, nb, lp, final_g, pos, offs, *, tt, final_norm):
    t = x2.shape[0]
    nt = t // tt
    kern = functools.partial(_moe_sorted_kernel, tiles_per_seq=nt // nb, final_norm=final_norm)
    grid_spec = pltpu.PrefetchScalarGridSpec(
        num_scalar_prefetch=2,
        grid=(nt, N_GROUPS),
        in_specs=[pl.BlockSpec((tt, D), lambda i, g, *_: (i, 0)),
                  pl.BlockSpec((None, nb, 6 * D), lambda i, g, *_: (l, ns // nb, 0), pipeline_mode=pl.Buffered(1)),
                  _layer_spec(lp["g2"], l), _layer_spec(lp["w_router_t"], l), _layer_spec(lp["b_router_c"], l)]
        + _expert_specs(lp, l) + [pl.BlockSpec((1, D), lambda i, g, *_: (0, 0))],
        out_specs=pl.BlockSpec((tt, D), lambda i, g, *_: (i, 0)),
        scratch_shapes=[pltpu.VMEM((tt * SUBLANES, LANES), F32), pltpu.VMEM((tt * SUBLANES, LANES), F32),
                        pltpu.VMEM((tt + MOE_BLOCK, D), BF16), pltpu.VMEM((tt + MOE_BLOCK, LANES), F32),
                        pltpu.VMEM((tt + MOE_BLOCK, D), F32)])
    return pl.pallas_call(
        kern,
        grid_spec=grid_spec,
        out_shape=jax.ShapeDtypeStruct((t, D), F32),
        compiler_params=pltpu.CompilerParams(
            dimension_semantics=("arbitrary", "arbitrary"), vmem_limit_bytes=VMEM_LIMIT),
        name="moe_prompt",
    )(pos, offs, x2, mod, lp["g2"], lp["w_router_t"], lp["b_router_c"], lp["w_gate"], lp["w_up"], lp["w_down"],
      final_g)


def _pad_to(a, axis, size):
    pad = [(0, 0)] * a.ndim
    pad[axis] = (0, size - a.shape[axis])
    return jnp.pad(a, pad)


def _prepare_params(p):
    lb_cum = jnp.cumsum(jax.nn.softmax(p["hgrn_lb_logits"].astype(F32), axis=0), axis=0)
    lb = lb_cum - lb_cum[:1]
    lbv = _pad_to(jnp.stack([jnp.log(lb), jnp.log1p(-lb), 1.0 - lb], axis=1), 1, SUBLANES)
    a_re, a_im = p["s5_a_re"].astype(F32), p["s5_a_im"].astype(F32)
    dt = jnp.exp(p["s5_log_dt"].astype(F32))[..., None]
    mag = jnp.exp(a_re * dt)
    abar_re = mag * jnp.cos(a_im * dt)
    abar_im = mag * jnp.sin(a_im * dt)
    den = a_re * a_re + a_im * a_im
    nr, ni = abar_re - 1.0, abar_im
    coef_re = ((nr * a_re + ni * a_im) / den)[..., None]
    coef_im = ((ni * a_re - nr * a_im) / den)[..., None]
    b_re, b_im = p["s5_b_re"].astype(F32), p["s5_b_im"].astype(F32)
    bb_re = coef_re * b_re - coef_im * b_im
    bb_im = coef_re * b_im + coef_im * b_re
    eye = jnp.eye(C_G, dtype=F32)
    blk_b = lambda m: jnp.einsum("lgph,gk->lghkp", m, eye).reshape(DEPTH, C_W, C_S)
    blk_bt = lambda m: jnp.einsum("lgph,gk->lgpkh", m, eye).reshape(DEPTH, C_S, C_W)
    blk_c = lambda m: jnp.einsum("lghp,gk->lgpkh", m, eye).reshape(DEPTH, C_S, C_W)
    bb = jnp.concatenate([blk_b(bb_re), blk_b(bb_im)], axis=2).astype(BF16)
    bb_t = jnp.concatenate([blk_bt(bb_re), blk_bt(bb_im)], axis=1).astype(BF16)
    cc = jnp.concatenate([blk_c(p["s5_c_re"].astype(F32)), -blk_c(p["s5_c_im"].astype(F32))],
                         axis=1).astype(BF16)
    abar_rows = jnp.stack([abar_re.reshape(DEPTH, C_S), abar_im.reshape(DEPTH, C_S)], axis=1)
    router = jnp.concatenate([p["moe_w_group"], p["moe_w_expert"]], axis=2)
    b_router = _pad_to(jnp.concatenate([p["moe_b_group"], p["moe_b_expert"]], axis=1), 1, LANES)
    row = lambda a: a.astype(F32)[:, None, :]
    col = lambda a: a.astype(F32)[:, :, None]
    return dict(
        g1=row(p["norm1_g"]), g2=row(p["norm2_g"]),
        w_in=jnp.transpose(p["w_in"], (0, 2, 1)).astype(BF16), lbv=lbv,
        w_alpha=_pad_to(_pad_to(p["gla_w_alpha"], 1, LANES), 2, 256).astype(BF16),
        b_alpha=_pad_to(row(p["gla_b_alpha"]), 2, 256),
        hgrn_g=row(jnp.tile(p["hgrn_norm_g"], (1, A_H))), gla_g=row(jnp.tile(p["gla_norm_g"], (1, B_H))),
        hgrn_gc=col(p["hgrn_norm_g"]), gla_gc=col(p["gla_norm_g"]),
        abar=_pad_to(abar_rows, 1, SUBLANES), abar_c=jnp.transpose(abar_rows, (0, 2, 1)),
        bb=bb, bb_t=bb_t, cc=cc, d_skip=row(p["s5_d"]),
        w_glu=p["s5_w_glu"].astype(BF16), b_glu=row(p["s5_b_glu"]), w_out=p["w_out"].astype(BF16),
        w_gate=p["moe_w_gate"].astype(BF16), w_up=p["moe_w_up"].astype(BF16),
        w_down=p["moe_w_down"].astype(BF16),
        w_router=_pad_to(router, 2, LANES).astype(BF16), b_router=b_router[:, None, :],
        w_router_t=_pad_to(jnp.transpose(router, (0, 2, 1)), 1, LANES).astype(BF16),
        b_router_c=b_router[:, :, None])


def kernel(x_prompt, x_sample, state_hgrn, state_gla, state_s5_re, state_s5_im, c_prompt, c_sample, ada_w, ada_b, norm1_g, norm2_g, w_in, hgrn_lb_logits, hgrn_norm_g, gla_w_alpha, gla_b_alpha, gla_norm_g, s5_a_re, s5_a_im, s5_b_re, s5_b_im, s5_c_re, s5_c_im, s5_d, s5_log_dt, s5_w_glu, s5_b_glu, w_out, moe_w_group, moe_b_group, moe_w_expert, moe_b_expert, moe_w_gate, moe_w_up, moe_w_down, final_norm_g):
    lp = _prepare_params(dict(
        norm1_g=norm1_g, norm2_g=norm2_g, w_in=w_in, hgrn_lb_logits=hgrn_lb_logits, hgrn_norm_g=hgrn_norm_g,
        gla_w_alpha=gla_w_alpha, gla_b_alpha=gla_b_alpha, gla_norm_g=gla_norm_g, s5_a_re=s5_a_re,
        s5_a_im=s5_a_im, s5_b_re=s5_b_re, s5_b_im=s5_b_im, s5_c_re=s5_c_re, s5_c_im=s5_c_im, s5_d=s5_d,
        s5_log_dt=s5_log_dt, s5_w_glu=s5_w_glu, s5_b_glu=s5_b_glu, w_out=w_out, moe_w_group=moe_w_group,
        moe_b_group=moe_b_group, moe_w_expert=moe_w_expert, moe_b_expert=moe_b_expert,
        moe_w_gate=moe_w_gate, moe_w_up=moe_w_up, moe_w_down=moe_w_down))
    nb, seq, _ = x_prompt.shape
    ns = x_sample.shape[0]
    assert ns % nb == 0 and seq % CHUNK == 0
    tt = min(MOE_TILE, seq)
    assert seq % tt == 0 and tt % BF16_ROWS == 0
    final_g = final_norm_g.reshape(1, D)

    mod = _ada_call(jnp.concatenate([c_sample, c_prompt], axis=0), ada_w, ada_b)
    states = (jnp.transpose(state_hgrn, (0, 2, 3, 4, 1)), jnp.transpose(state_gla, (0, 2, 3, 4, 1)),
              jnp.transpose(state_s5_re, (0, 2, 3, 1)).reshape(DEPTH, C_S, ns),
              jnp.transpose(state_s5_im, (0, 2, 3, 1)).reshape(DEPTH, C_S, ns))
    new_states = tuple(jnp.zeros_like(s) for s in states)

    xp = x_prompt
    xs = x_sample.reshape(ns, D)
    outs = {k: [] for k in ("hg_p", "gl_p", "sr_p", "si_p")}
    for l in range(DEPTH):
        last = l == DEPTH - 1
        xp, sta, stb, s5r, s5i = _mixer_prompt_call(l, xp, mod, ns, lp)
        x2 = xp.reshape(nb * seq, D)
        pos, offs = _route_call(l, x2, mod, ns, nb, lp, tt=tt)
        xp = _moe_sorted_call(l, x2, mod, ns, nb, lp, final_g, pos.reshape(-1), offs.reshape(-1),
                              tt=tt, final_norm=last).reshape(nb, seq, D)
        sta = sta.reshape(nb, A_H // 2, 2, A_DV, 2, A_DK)
        outs["hg_p"].append(jnp.stack([sta[:, h // 2, h % 2, :, h % 2, :] for h in range(A_H)], axis=1)
                            .transpose(0, 1, 3, 2))
        stb = stb.reshape(nb, B_H, B_DV, 4, B_DK)
        outs["gl_p"].append(jnp.stack([stb[:, h, :, h % 4, :] for h in range(B_H)], axis=1)
                            .transpose(0, 1, 3, 2))
        outs["sr_p"].append(s5r.reshape(nb, C_G, C_P))
        outs["si_p"].append(s5i.reshape(nb, C_G, C_P))
        xs, *new_states = _mixer_sample_call(l, xs, mod, lp, states, new_states)
        xs = _moe_dense_call(l, xs, mod, lp, final_g, final_norm=last)

    hg_s, gl_s, sr_s, si_s = new_states
    return (xp, xs.reshape(ns, 1, D),
            jnp.stack(outs["hg_p"]), jnp.stack(outs["gl_p"]), jnp.stack(outs["sr_p"]), jnp.stack(outs["si_p"]),
            jnp.transpose(hg_s, (0, 4, 1, 2, 3)), jnp.transpose(gl_s, (0, 4, 1, 2, 3)),
            jnp.transpose(sr_s.reshape(DEPTH, C_G, C_P, ns), (0, 3, 1, 2)),
            jnp.transpose(si_s.reshape(DEPTH, C_G, C_P, ns), (0, 3, 1, 2)))
```

```python
import functools
import math

import numpy as np
import jax
import jax.numpy as jnp
from jax import lax
from jax.experimental import pallas as pl
from jax.experimental.pallas import tpu as pltpu

F32 = jnp.float32
BF16 = jnp.bfloat16

D = 1024
DEPTH = 4
A_H, A_DK, A_DV = 6, 64, 64
B_H, B_DK, B_DV = 6, 32, 64
B_RANK = 16
B_TAU = 16.0
C_G, C_CH, C_P = 16, 16, 64
C_W = C_G * C_CH
C_S = C_G * C_P
N_GROUPS, N_PER_GROUP, D_EXP = 4, 4, 256
N_EXP = N_GROUPS * N_PER_GROUP
EPS = 1e-6
CHUNK = 64
LANES = 128
SUBLANES = 8
EXP_RANGE = 80.0
MOE_TILE = 1024
MOE_BLOCK = 512
MOE_TAIL_BLOCKS = (512, 384, 256, 128)
BF16_ROWS = 16

W_AQ, W_AF, W_AI, W_AG = 0, 384, 768, 1152
W_BQ, W_BK, W_BV, W_LR, W_BR, W_CU = 1536, 1728, 1920, 2304, 2320, 2704
IN_COLS = 2960
W_SEGMENTS = {"aq|af": (W_AQ, 768), "ai|ag": (W_AI, 768), "bq": (W_BQ, 256), "bk": (W_BK, 256),
              "bv|lr": (W_BV, 512), "br|cu": (W_BR, 640)}
QA, FA, VA, GA = 0, 384, 768, 1152
QB, KB, VB, RB = 1536, 1792, 2048, 2432
CU, LB, KA = 2816, 3072, 3328
PW = 3712
N_LG = 5
LG_W = N_LG * LANES
BLOCKS = tuple((QA + LANES * p, KA + LANES * p, FA + LANES * p, LANES * p, 2, A_DK) for p in range(3)) + (
    (QB, KB, LB, 384, 4, B_DK), (QB + LANES, KB + LANES, LB + LANES, 640, 2, B_DK))
VO_W = 768

VMEM_LIMIT = 56 * 1024 * 1024


def _sigmoid(x):
    return 1.0 / (1.0 + jnp.exp(-x))


def _silu(x):
    return x * _sigmoid(x)


def _log_sigmoid(x):
    return jnp.minimum(x, 0.0) - jnp.log(1.0 + jnp.exp(-jnp.abs(x)))


def _gelu_tanh(x):
    return 0.5 * x * (1.0 + jnp.tanh(math.sqrt(2.0 / math.pi) * (x + 0.044715 * x * x * x)))


def _split3(x):
    hi = x.astype(BF16)
    r = x - hi.astype(F32)
    mid = r.astype(BF16)
    lo = (r - mid.astype(F32)).astype(BF16)
    return hi, mid, lo


def _dot(a, b):
    return jnp.dot(a, b, preferred_element_type=F32)


def _dot_nt(a, b):
    return lax.dot_general(a, b, (((1,), (1,)), ((), ())), preferred_element_type=F32)


def _dot_tn(a, b):
    return lax.dot_general(a, b, (((0,), (0,)), ((), ())), preferred_element_type=F32)


def _dot_exact_01(x, m01):
    hi, mid, lo = _split3(x)
    return _dot(hi, m01) + _dot(mid, m01) + _dot(lo, m01)


def _rms_rows(x):
    return x * lax.rsqrt(jnp.mean(x * x, axis=-1, keepdims=True) + EPS)


def _forget_gate(af, lbv):
    e = jnp.exp(-jnp.abs(af))
    log_sig = jnp.minimum(af, 0.0) - jnp.log(1.0 + e)
    a = lbv[0:1]
    b = lbv[1:2] + log_sig
    log_f = jnp.maximum(a, b) + jnp.log(1.0 + jnp.exp(-jnp.abs(a - b)))
    k = lbv[2:3] * (jnp.where(af >= 0.0, e, 1.0) / (1.0 + e))
    return log_f, k


def _chunk_cumsum(x, c):
    pos = lax.broadcasted_iota(jnp.int32, x.shape, 0) % c
    s = 1
    while s < c:
        x = x + jnp.where(pos >= s, pltpu.roll(x, s, 0), 0.0)
        s *= 2
    return x


def _head_rms(o, dv):
    w = o.shape[1]
    r = lax.broadcasted_iota(jnp.int32, (w, w), 0)
    cidx = lax.broadcasted_iota(jnp.int32, (w, w), 1)
    ones_blk = jnp.where(r // dv == cidx // dv, 1.0, 0.0).astype(BF16)
    return o * lax.rsqrt(_dot((o * o).astype(BF16), ones_blk) * (1.0 / dv) + EPS)


def _project_all(h, wint_ref, lbv_ref, walpha_ref, balpha_ref, chunk, p_scr):
    def proj(name):
        r0, n = W_SEGMENTS[name]
        return _dot_nt(h, wint_ref[r0:r0 + n, :])

    def cum(x):
        return x if chunk == 1 else _chunk_cumsum(x, chunk)

    brcu = proj("br|cu")
    p_scr[:, RB:RB + 384] = _silu(brcu[:, 0:384])
    p_scr[:, CU:CU + 256] = brcu[:, 384:640]
    aqf = proj("aq|af")
    p_scr[:, QA:QA + 384] = _silu(aqf[:, 0:384])
    log_f, ka = _forget_gate(aqf[:, 384:768], lbv_ref[...])
    p_scr[:, FA:FA + 384] = cum(log_f)
    p_scr[:, KA:KA + 384] = ka
    aig = proj("ai|ag")
    p_scr[:, VA:VA + 384] = aig[:, 0:384]
    p_scr[:, GA:GA + 384] = _silu(aig[:, 384:768])
    p_scr[:, QB:QB + 256] = proj("bq") * (B_DK ** -0.5)
    kb = proj("bk")
    p_scr[:, KB:KB + 256] = jnp.where(lax.broadcasted_iota(jnp.int32, kb.shape, 1) < B_H * B_DK, kb, 0.0)
    bvlr = proj("bv|lr")
    p_scr[:, VB:VB + 384] = bvlr[:, 0:384]
    z = _dot(bvlr[:, 384:512].astype(BF16), walpha_ref[...]) + balpha_ref[...]
    p_scr[:, LB:LB + 256] = cum(_log_sigmoid(z) * (1.0 / B_TAU))


def _layer_spec(arr, l):
    nd = arr.ndim - 1
    return pl.BlockSpec((None,) + arr.shape[1:], lambda *_, _n=nd: (l,) + (0,) * _n,
                        pipeline_mode=pl.Buffered(1))


def _ada_kernel(c_ref, w_ref, b_ref, o_ref):
    sc = _silu(c_ref[...]).astype(BF16)
    half = w_ref.shape[2] // 2
    for n0 in (0, half):
        o_ref[0, :, n0:n0 + half] = _dot(sc, w_ref[0, :, n0:n0 + half].astype(BF16)) + b_ref[0, :, n0:n0 + half]


def _ada_call(c_all, ada_w, ada_b):
    n = c_all.shape[0]
    tn = 1536
    return pl.pallas_call(
        _ada_kernel,
        grid=(DEPTH, 6 * D // tn),
        in_specs=[
            pl.BlockSpec((n, D), lambda l, j: (0, 0)),
            pl.BlockSpec((1, D, tn), lambda l, j: (l, 0, j)),
            pl.BlockSpec((1, 1, tn), lambda l, j: (l, 0, j)),
        ],
        out_specs=pl.BlockSpec((1, n, tn), lambda l, j: (l, 0, j)),
        out_shape=jax.ShapeDtypeStruct((DEPTH, n, 6 * D), F32),
        compiler_params=pltpu.CompilerParams(
            dimension_semantics=("arbitrary", "arbitrary"), vmem_limit_bytes=VMEM_LIMIT),
        name="ada_mod",
    )(c_all, ada_w, ada_b.reshape(DEPTH, 1, 6 * D))


def _mixer_prompt_kernel(x_ref, mod_ref, g1_ref, win_ref, lbv_ref, walpha_ref, balpha_ref,
                         hg_ref, gg_ref, abar_ref, bb_ref, cc_ref, dsk_ref, wglu_ref, bglu_ref,
                         wout_ref,
                         xo_ref, sta_ref, stb_ref, s5r_ref, s5i_ref,
                         h_scr, p_scr, o_scr, xr_scr, xi_scr, t_scr,
                         qt_scr, kt_scr, qd_scr, kd_scr, v_scr, of_scr, dec_scr, sc_scr):
    nb = x_ref.shape[0]
    c = CHUNK
    i = pl.program_id(0)

    @pl.when(i == 0)
    def _():
        sta_ref[...] = jnp.zeros_like(sta_ref)
        stb_ref[...] = jnp.zeros_like(stb_ref)
        s5r_ref[...] = jnp.zeros_like(s5r_ref)
        s5i_ref[...] = jnp.zeros_like(s5i_ref)

    for b in range(nb):
        hb = _rms_rows(x_ref[b]) * g1_ref[...]
        hb = hb * (1.0 + mod_ref[b:b + 1, D:2 * D]) + mod_ref[b:b + 1, 0:D]
        h_scr[b * c:(b + 1) * c, :] = hb.astype(BF16)

    halves = C_W // LANES

    def s5_input():
        for b in range(nb):
            for j in range(halves):
                t_scr[j, b:b + nb * c:nb, :] = p_scr[b * c:(b + 1) * c, CU + LANES * j:CU + LANES * (j + 1)]
        u_bf = jnp.concatenate([t_scr[j] for j in range(halves)], axis=1).astype(BF16)
        xr_scr[...] = _dot(u_bf, bb_ref[:, 0:C_S])
        xi_scr[...] = _dot(u_bf, bb_ref[:, C_S:2 * C_S])

    def operands(groups, rng):
        for b in range(nb):
            rs = slice(b * c, (b + 1) * c)
            for g in groups:
                qc, kc, bc = BLOCKS[g][0:3]
                ls = slice(g * LANES, (g + 1) * LANES)
                q = p_scr[rs, qc:qc + LANES]
                k = p_scr[rs, kc:kc + LANES]
                bcum = p_scr[rs, bc:bc + LANES]
                b_mid = bcum[c // 2 - 1:c // 2, :]
                b_end = bcum[c - 1:c, :]
                d = bcum - b_mid
                rng = jnp.maximum(rng, jnp.maximum(jnp.abs(d[0:1, :]), jnp.abs(d[c - 1:c, :])))
                qt_scr[rs, ls] = (q * jnp.exp(d)).astype(BF16)
                kt_scr[rs, ls] = (k * jnp.exp(-d)).astype(BF16)
                qd_scr[rs, ls] = (q * jnp.exp(bcum)).astype(BF16)
                kd_scr[rs, ls] = (k * jnp.exp(b_end - bcum)).astype(BF16)
                dec_scr[b, :, ls] = jnp.broadcast_to(jnp.exp(b_end), (SUBLANES, LANES))
        return rng

    _project_all(h_scr[...], win_ref, lbv_ref, walpha_ref, balpha_ref, c, p_scr)
    s5_input()
    v_scr[:, 0:384] = p_scr[:, VA:VA + 384].astype(BF16)
    v_scr[:, 384:768] = p_scr[:, VB:VB + 384].astype(BF16)
    rng = operands(range(N_LG), jnp.zeros((1, LANES), F32))
    in_range = jnp.max(rng) <= EXP_RANGE

    lane = lax.broadcasted_iota(jnp.int32, (c, LANES), 1)

    def scores_direct(r0, qc, kc, bc, nh, dk):
        q = p_scr[pl.ds(r0, c), qc:qc + LANES]
        bcum = p_scr[pl.ds(r0, c), bc:bc + LANES]
        krow = lax.broadcasted_iota(jnp.int32, (LANES, nh * c), 0)
        kcol = lax.broadcasted_iota(jnp.int32, (LANES, nh * c), 1)
        sub = lax.broadcasted_iota(jnp.int32, (SUBLANES, LANES), 0)

        def row_of(col0, s):
            r8 = pl.ds(pl.multiple_of(r0 + (s // SUBLANES) * SUBLANES, SUBLANES), SUBLANES)
            return jnp.sum(jnp.where(sub == s % SUBLANES, p_scr[r8, col0:col0 + LANES], 0.0),
                           axis=0, keepdims=True)

        def body(s, sc):
            ks = row_of(kc, s)
            bs = row_of(bc, s)
            w = q * ks * jnp.exp(jnp.minimum(bcum - bs, 0.0))
            place = jnp.where(kcol == (krow // dk) * c + s, 1.0, 0.0).astype(BF16)
            return sc + _dot_exact_01(w, place)

        return lax.fori_loop(0, c, body, jnp.zeros((c, nh * c), F32))

    def stage_scores(b, r0, direct):
        rs = pl.ds(r0, c)
        for g, (qc, kc, bc, vc, nh, dk) in enumerate(BLOCKS):
            ls = slice(g * LANES, (g + 1) * LANES)
            if direct:
                sc = scores_direct(r0, qc, kc, bc, nh, dk)
            else:
                kt = kt_scr[rs, ls]
                kcat = jnp.concatenate(
                    [jnp.where(lane // dk == h, kt, jnp.zeros_like(kt)) for h in range(nh)], axis=0)
                sc = _dot_nt(qt_scr[rs, ls], kcat)
            row = lax.broadcasted_iota(jnp.int32, (c, nh * c), 0)
            col = lax.broadcasted_iota(jnp.int32, (c, nh * c), 1)
            sc_scr[rs, vc:vc + nh * c] = jnp.where(col % c <= row, sc, 0.0).astype(BF16)

    def stage_readout(b, r0):
        rs = pl.ds(r0, c)
        for g, (_, _, _, vc, nh, dk) in enumerate(BLOCKS):
            ls = slice(g * LANES, (g + 1) * LANES)
            vw = nh * 64
            st_ref = sta_ref if g < 3 else stb_ref
            v0 = vc if g < 3 else vc - 384
            v = v_scr[rs, vc:vc + vw]
            vlane = lax.broadcasted_iota(jnp.int32, (c, vw), 1)
            vcat = jnp.concatenate(
                [jnp.where(vlane // 64 == h, v, jnp.zeros_like(v)) for h in range(nh)], axis=0)
            st = st_ref[b, v0:v0 + vw, :]
            of_scr[rs, vc:vc + vw] = (_dot(sc_scr[rs, vc:vc + nh * c], vcat)
                                      + _dot_nt(qd_scr[rs, ls], st.astype(BF16)))

    def stage_update(b, r0):
        rs = pl.ds(r0, c)
        for g, (_, _, _, vc, nh, dk) in enumerate(BLOCKS):
            ls = slice(g * LANES, (g + 1) * LANES)
            vw = nh * 64
            st_ref = sta_ref if g < 3 else stb_ref
            v0 = vc if g < 3 else vc - 384
            upd = _dot_tn(v_scr[rs, vc:vc + vw], kd_scr[rs, ls])
            srow = lax.broadcasted_iota(jnp.int32, (vw, LANES), 0)
            scol = lax.broadcasted_iota(jnp.int32, (vw, LANES), 1)
            st_ref[b, v0:v0 + vw, :] = (st_ref[b, v0:v0 + vw, :] * dec_scr[b, 0:1, ls]
                                        + jnp.where(srow // 64 == scol // dk, upd, 0.0))

    @pl.when(in_range)
    def _():
        for b in range(nb):
            stage_scores(b, b * c, False)
        for b in range(nb):
            stage_readout(b, b * c)
        for b in range(nb):
            stage_update(b, b * c)

    @pl.when(jnp.logical_not(in_range))
    def _():
        def body(b, carry):
            r0 = pl.multiple_of(b * c, c)
            stage_scores(b, r0, True)
            stage_readout(b, r0)
            stage_update(b, r0)
            return carry
        lax.fori_loop(0, nb, body, 0)

    oa = _head_rms(of_scr[:, 0:384], A_DV) * hg_ref[...] * p_scr[:, GA:GA + 384]
    o_scr[:, 0:384] = oa.astype(BF16)
    ob = _head_rms(of_scr[:, 384:768], B_DV) * gg_ref[...] * p_scr[:, RB:RB + 384]
    o_scr[:, 384:768] = ob.astype(BF16)

    ar = abar_ref[0:1, :]
    ai = abar_ref[1:2, :]

    def scan_t(t, carry):
        sr, si = carry
        ts = pl.ds(pl.multiple_of(t * nb, nb), nb)
        nr = ar * sr - ai * si + xr_scr[ts, :]
        ni = ar * si + ai * sr + xi_scr[ts, :]
        xr_scr[ts, :] = nr
        xi_scr[ts, :] = ni
        return nr, ni

    sr, si = lax.fori_loop(0, c, scan_t, (s5r_ref[...], s5i_ref[...]), unroll=2)
    s5r_ref[...] = sr
    s5i_ref[...] = si
    u = jnp.concatenate([t_scr[j] for j in range(halves)], axis=1)
    half = nb * c // 2
    y = jnp.concatenate(
        [_dot(xr_scr[r:r + half, :].astype(BF16), cc_ref[0:C_S, :])
         + _dot(xi_scr[r:r + half, :].astype(BF16), cc_ref[C_S:2 * C_S, :]) for r in (0, half)], axis=0)
    y = y + dsk_ref[...] * u
    z = _gelu_tanh(y)
    oc = z * _sigmoid(_dot(z.astype(BF16), wglu_ref[...]) + bglu_ref[...])
    for j in range(halves):
        t_scr[j] = oc[:, LANES * j:LANES * (j + 1)]
    for b in range(nb):
        for j in range(halves):
            o_scr[b * c:(b + 1) * c, 768 + LANES * j:768 + LANES * (j + 1)] = (
                t_scr[j, b:b + nb * c:nb, :].astype(BF16))

    o_all = o_scr[...]
    for n0 in (0, D // 2):
        out = _dot(o_all, wout_ref[:, n0:n0 + D // 2])
        for b in range(nb):
            xo_ref[b, :, n0:n0 + D // 2] = (x_ref[b, :, n0:n0 + D // 2]
                                            + mod_ref[b:b + 1, 2 * D + n0:2 * D + n0 + D // 2]
                                            * out[b * c:(b + 1) * c, :])


def _mixer_prompt_call(l, x, mod, ns, lp):
    nb, seq, _ = x.shape
    c = CHUNK
    rows = nb * c
    consts = [lp[k] for k in ("g1", "w_in", "lbv", "w_alpha", "b_alpha", "hgrn_g", "gla_g", "abar", "bb", "cc",
                              "d_skip", "w_glu", "b_glu", "w_out")]
    in_specs = ([pl.BlockSpec((nb, c, D), lambda i: (0, i, 0)),
                 pl.BlockSpec((None, nb, 6 * D), lambda i: (l, ns // nb, 0), pipeline_mode=pl.Buffered(1))]
                + [_layer_spec(a, l) for a in consts])
    out_shape = (jax.ShapeDtypeStruct((nb, seq, D), F32),
                 jax.ShapeDtypeStruct((nb, 384, LANES), F32),
                 jax.ShapeDtypeStruct((nb, 384, LANES), F32),
                 jax.ShapeDtypeStruct((nb, C_S), F32),
                 jax.ShapeDtypeStruct((nb, C_S), F32))
    out_specs = (pl.BlockSpec((nb, c, D), lambda i: (0, i, 0)),
                 pl.BlockSpec((nb, 384, LANES), lambda i: (0, 0, 0)),
                 pl.BlockSpec((nb, 384, LANES), lambda i: (0, 0, 0)),
                 pl.BlockSpec((nb, C_S), lambda i: (0, 0)),
                 pl.BlockSpec((nb, C_S), lambda i: (0, 0)))
    scratch = [pltpu.VMEM((rows, D), BF16), pltpu.VMEM((rows, PW), F32), pltpu.VMEM((rows, D), BF16),
               pltpu.VMEM((rows, C_S), F32), pltpu.VMEM((rows, C_S), F32),
               pltpu.VMEM((C_W // LANES, rows, LANES), F32),
               pltpu.VMEM((rows, LG_W), BF16), pltpu.VMEM((rows, LG_W), BF16),
               pltpu.VMEM((rows, LG_W), BF16), pltpu.VMEM((rows, LG_W), BF16),
               pltpu.VMEM((rows, VO_W), BF16), pltpu.VMEM((rows, VO_W), F32),
               pltpu.VMEM((nb, SUBLANES, LG_W), F32), pltpu.VMEM((rows, VO_W), BF16)]
    return pl.pallas_call(
        _mixer_prompt_kernel,
        grid=(seq // c,),
        in_specs=in_specs,
        out_specs=out_specs,
        out_shape=out_shape,
        scratch_shapes=scratch,
        compiler_params=pltpu.CompilerParams(
            dimension_semantics=("arbitrary",), vmem_limit_bytes=VMEM_LIMIT),
        name="mixer_prompt",
    )(x, mod, *consts)


def _step_head(s_ref, so_ref, t_scr, i, v_t):
    acc = jnp.zeros(v_t.shape, F32)
    for k in range(s_ref.shape[0]):
        new = s_ref[k] * t_scr[1, i, k:k + 1, :] + t_scr[2, i, k:k + 1, :] * v_t
        so_ref[k] = new
        acc = acc + new * t_scr[0, i, k:k + 1, :]
    return acc


def _mixer_sample_kernel(x_ref, mod_ref, g1_ref, win_ref, lbv_ref, walpha_ref, balpha_ref,
                         hgc_ref, ggc_ref, abarc_ref, bbt_ref, cc_ref, dsk_ref, wglu_ref, bglu_ref,
                         wout_ref, sa_ref, sb_ref, s5r_ref, s5i_ref, pa_ref, pb_ref, pr_ref, pi_ref,
                         xo_ref, sao_ref, sbo_ref, s5ro_ref, s5io_ref,
                         p_scr, at_scr, bt_scr, vt_scr, gate_scr, oh_scr, oc_scr):
    del pa_ref, pb_ref, pr_ref, pi_ref
    i = pl.program_id(0)

    @pl.when(i == 0)
    def _():
        h = _rms_rows(x_ref[...]) * g1_ref[...]
        h = (h * (1.0 + mod_ref[:, D:2 * D]) + mod_ref[:, 0:D]).astype(BF16)
        _project_all(h, win_ref, lbv_ref, walpha_ref, balpha_ref, 1, p_scr)
        for g, (qc, kc, bc, _, nh, dk) in enumerate(BLOCKS):
            t_scr, h0 = (at_scr, 2 * g) if g < 3 else (bt_scr, 4 * (g - 3))
            qt = p_scr[:, qc:qc + LANES].T
            dt = jnp.exp(p_scr[:, bc:bc + LANES]).T
            kt = p_scr[:, kc:kc + LANES].T
            for hh in range(nh):
                t_scr[0, h0 + hh] = qt[hh * dk:(hh + 1) * dk, :]
                t_scr[1, h0 + hh] = dt[hh * dk:(hh + 1) * dk, :]
                t_scr[2, h0 + hh] = kt[hh * dk:(hh + 1) * dk, :]
        for p in range(A_H // 2):
            for grp, vcol, gcol in ((0, VA, GA), (1, VB, RB)):
                vt = p_scr[:, vcol + LANES * p:vcol + LANES * (p + 1)].T
                gt = p_scr[:, gcol + LANES * p:gcol + LANES * (p + 1)].T
                for hh in range(2):
                    vt_scr[grp, 2 * p + hh] = vt[64 * hh:64 * (hh + 1), :]
                    gate_scr[grp, 2 * p + hh] = gt[64 * hh:64 * (hh + 1), :]
        u = p_scr[:, CU:CU + C_W]
        xri = _dot_nt(bbt_ref[...], u.astype(BF16))
        ar = abarc_ref[:, 0:1]
        ai = abarc_ref[:, 1:2]
        s0r = s5r_ref[...]
        s0i = s5i_ref[...]
        sr = ar * s0r - ai * s0i + xri[0:C_S, :]
        si = ar * s0i + ai * s0r + xri[C_S:2 * C_S, :]
        s5ro_ref[...] = sr
        s5io_ref[...] = si
        y = (_dot_tn(sr.astype(BF16), cc_ref[0:C_S, :]) + _dot_tn(si.astype(BF16), cc_ref[C_S:2 * C_S, :])
             + dsk_ref[...] * u)
        z = _gelu_tanh(y)
        oc_scr[...] = z * _sigmoid(_dot(z.astype(BF16), wglu_ref[...]) + bglu_ref[...])

    def norm_gate(o, gain_col, gate):
        return o * lax.rsqrt(jnp.mean(o * o, axis=0, keepdims=True) + EPS) * gain_col * gate

    oa = _step_head(sa_ref, sao_ref, at_scr, i, vt_scr[0, i])
    oh_scr[0, i] = norm_gate(oa, hgc_ref[...], gate_scr[0, i])
    ob = _step_head(sb_ref, sbo_ref, bt_scr, i, vt_scr[1, i])
    oh_scr[1, i] = norm_gate(ob, ggc_ref[...], gate_scr[1, i])

    @pl.when(i == A_H - 1)
    def _():
        parts = []
        for grp in range(2):
            for p in range(A_H // 2):
                two = jnp.concatenate([oh_scr[grp, 2 * p], oh_scr[grp, 2 * p + 1]], axis=0)
                parts.append(two.T)
        o = jnp.concatenate(parts + [oc_scr[...]], axis=1).astype(BF16)
        xo_ref[...] = x_ref[...] + mod_ref[:, 2 * D:3 * D] * _dot(o, wout_ref[...])


def _mixer_sample_call(l, x, mod, lp, states, prev):
    n = x.shape[0]
    consts = [lp[k] for k in ("g1", "w_in", "lbv", "w_alpha", "b_alpha", "hgrn_gc", "gla_gc", "abar_c", "bb_t",
                              "cc", "d_skip", "w_glu", "b_glu", "w_out")]
    state_specs = [pl.BlockSpec((None, None, A_DK, A_DV, n), lambda i: (l, i, 0, 0, 0)),
                   pl.BlockSpec((None, None, B_DK, B_DV, n), lambda i: (l, i, 0, 0, 0)),
                   pl.BlockSpec((None, C_S, n), lambda i: (l, 0, 0)),
                   pl.BlockSpec((None, C_S, n), lambda i: (l, 0, 0))]
    in_specs = ([pl.BlockSpec((n, D), lambda i: (0, 0), pipeline_mode=pl.Buffered(1)),
                 pl.BlockSpec((None, n, 6 * D), lambda i: (l, 0, 0), pipeline_mode=pl.Buffered(1))]
                + [_layer_spec(a, l) for a in consts] + state_specs
                + [pl.BlockSpec(memory_space=pl.ANY)] * 4)
    n_in = 2 + len(consts) + 4
    out_shape = (jax.ShapeDtypeStruct((n, D), F32),) + tuple(jax.ShapeDtypeStruct(s.shape, F32) for s in states)
    out_specs = (pl.BlockSpec((n, D), lambda i: (0, 0)),) + tuple(state_specs)
    scratch = [pltpu.VMEM((n, PW), F32), pltpu.VMEM((3, A_H, A_DK, n), F32), pltpu.VMEM((3, 8, B_DK, n), F32),
               pltpu.VMEM((2, A_H, 64, n), F32), pltpu.VMEM((2, A_H, 64, n), F32),
               pltpu.VMEM((2, A_H, 64, n), F32), pltpu.VMEM((n, C_W), F32)]
    return pl.pallas_call(
        _mixer_sample_kernel,
        grid=(A_H,),
        in_specs=in_specs,
        out_specs=out_specs,
        out_shape=out_shape,
        scratch_shapes=scratch,
        input_output_aliases={n_in + j: 1 + j for j in range(4)},
        compiler_params=pltpu.CompilerParams(
            dimension_semantics=("arbitrary",), vmem_limit_bytes=VMEM_LIMIT),
        name="mixer_sample",
    )(x, mod, *consts, *states, *prev)


def _top2_gates(logits, gidx, lane):
    neg = jnp.float32(-jnp.inf)
    big = jnp.float32(1 << 20)
    own = jnp.sum(jnp.where(lane == gidx, logits, 0.0), axis=1, keepdims=True)
    p_group = 1.0 / jnp.sum(jnp.where(lane < N_GROUPS, jnp.exp(logits - own), 0.0), axis=1, keepdims=True)
    e0 = N_GROUPS + N_PER_GROUP * gidx
    el = jnp.where((lane >= e0) & (lane < e0 + N_PER_GROUP), logits, neg)
    v1 = jnp.max(el, axis=1, keepdims=True)
    i1 = jnp.min(jnp.where(el == v1, lane, big), axis=1, keepdims=True)
    el2 = jnp.where(lane == i1, neg, el)
    v2 = jnp.max(el2, axis=1, keepdims=True)
    i2 = jnp.min(jnp.where(el2 == v2, lane, big), axis=1, keepdims=True)
    ex = jnp.exp(v2 - v1)
    w1 = 1.0 / (1.0 + ex)
    return jnp.where(lane == i1, p_group * w1, 0.0) + jnp.where(lane == i2, p_group * (ex * w1), 0.0)


def _group_experts(h, gates, lane, lane0, wg_ref, wu_ref, wd_ref):
    y = None
    for e in range(N_PER_GROUP):
        hid = _silu(_dot(h, wg_ref[e])) * _dot(h, wu_ref[e])
        gcol = jnp.sum(jnp.where(lane == lane0 + e, gates, 0.0), axis=1, keepdims=True)
        ye = _dot((hid * gcol).astype(BF16), wd_ref[e])
        y = ye if y is None else y + ye
    return y


def _moe_dense_kernel(x_ref, mod_ref, g2_ref, wr_ref, br_ref, wg_ref, wu_ref, wd_ref, fg_ref,
                      o_ref, h_scr, gate_scr, acc_scr, *, final_norm):
    g = pl.program_id(1)
    tm = x_ref.shape[0]
    lane = lax.broadcasted_iota(jnp.int32, (tm, LANES), 1).astype(F32)

    @pl.when(g == 0)
    def _():
        h = _rms_rows(x_ref[...]) * g2_ref[...]
        h = (h * (1.0 + mod_ref[:, 4 * D:5 * D]) + mod_ref[:, 3 * D:4 * D]).astype(BF16)
        h_scr[...] = h
        logits = _dot(h, wr_ref[...]) + br_ref[...]
        gl = jnp.where(lane < N_GROUPS, logits, -jnp.inf)
        gmax = jnp.max(gl, axis=1, keepdims=True)
        gidx = jnp.min(jnp.where(gl == gmax, lane, jnp.float32(1 << 20)), axis=1, keepdims=True)
        gate_scr[...] = _top2_gates(logits, gidx, lane)
        acc_scr[...] = jnp.zeros_like(acc_scr)

    lane0 = (N_GROUPS + N_PER_GROUP * g).astype(F32)
    acc_scr[...] += _group_experts(h_scr[...], gate_scr[...], lane, lane0, wg_ref, wu_ref, wd_ref)

    @pl.when(g == N_GROUPS - 1)
    def _():
        xn = x_ref[...] + mod_ref[:, 5 * D:6 * D] * acc_scr[...]
        if final_norm:
            xn = _rms_rows(xn) * fg_ref[...]
        o_ref[...] = xn


def _expert_specs(lp, l):
    def spec(arr):
        return pl.BlockSpec((None, N_PER_GROUP) + arr.shape[2:], lambda i, g, *_: (l, g, 0, 0))
    return [spec(lp["w_gate"]), spec(lp["w_up"]), spec(lp["w_down"])]


def _moe_dense_call(l, x2, mod, lp, final_g, *, final_norm):
    tm = x2.shape[0]
    kern = functools.partial(_moe_dense_kernel, final_norm=final_norm)
    return pl.pallas_call(
        kern,
        grid=(1, N_GROUPS),
        in_specs=[pl.BlockSpec((tm, D), lambda i, g: (0, 0)),
                  pl.BlockSpec((None, tm, 6 * D), lambda i, g: (l, 0, 0)),
                  _layer_spec(lp["g2"], l), _layer_spec(lp["w_router"], l), _layer_spec(lp["b_router"], l)]
        + _expert_specs(lp, l) + [pl.BlockSpec((1, D), lambda i, g: (0, 0))],
        out_specs=pl.BlockSpec((tm, D), lambda i, g: (0, 0)),
        out_shape=jax.ShapeDtypeStruct((tm, D), F32),
        scratch_shapes=[pltpu.VMEM((tm, D), BF16), pltpu.VMEM((tm, LANES), F32), pltpu.VMEM((tm, D), F32)],
        compiler_params=pltpu.CompilerParams(
            dimension_semantics=("arbitrary", "arbitrary"), vmem_limit_bytes=VMEM_LIMIT),
        name="moe_sample",
    )(x2, mod, lp["g2"], lp["w_router"], lp["b_router"], lp["w_gate"], lp["w_up"], lp["w_down"], final_g)


def _route_kernel(x_ref, mod_ref, g2_ref, wrt_ref, brc_ref, tri_ref, pos_ref, offs_ref, *, tiles_per_seq):
    tt = x_ref.shape[0]
    b = pl.program_id(0) // tiles_per_seq
    h = _rms_rows(x_ref[...]) * g2_ref[...]
    h = (h * (1.0 + mod_ref[pl.ds(b, 1), 4 * D:5 * D]) + mod_ref[pl.ds(b, 1), 3 * D:4 * D]).astype(BF16)
    lt = _dot_nt(wrt_ref[...], h) + brc_ref[...]
    row = lax.broadcasted_iota(jnp.int32, (SUBLANES, tt), 0).astype(F32)
    gl = jnp.where(row < N_GROUPS, lt[0:SUBLANES, :], -jnp.inf)
    gmax = jnp.max(gl, axis=0, keepdims=True)
    gidx = jnp.min(jnp.where(gl == gmax, row, jnp.float32(1 << 20)), axis=0, keepdims=True)
    onehot = jnp.where(row == gidx, 1.0, 0.0)
    rank = _dot(onehot.astype(BF16), tri_ref[...])
    counts = jnp.sum(onehot, axis=1, keepdims=True)
    row1 = lax.broadcasted_iota(jnp.int32, (SUBLANES, 1), 0)
    off = jnp.zeros((SUBLANES, 1), F32)
    for g in range(N_GROUPS - 1):
        off = off + jnp.where(row1 > g, counts[g:g + 1, :], 0.0)
    pos_ref[0] = (jnp.sum(onehot * (rank + off), axis=0, keepdims=True) * float(SUBLANES)).astype(jnp.int32)
    lane = lax.broadcasted_iota(jnp.int32, (1, LANES), 1)
    offs = jnp.where(lane == N_GROUPS, float(tt), 0.0)
    for g in range(1, N_GROUPS):
        offs = offs + jnp.where(lane == g, off[g:g + 1, :], 0.0)
    offs_ref[0] = offs.astype(jnp.int32)


def _route_call(l, x2, mod, ns, nb, lp, *, tt):
    t = x2.shape[0]
    nt = t // tt
    tiles_per_seq = nt // nb
    tri = jnp.asarray(np.triu(np.ones((tt, tt), np.float32), 1), BF16)
    kern = functools.partial(_route_kernel, tiles_per_seq=tiles_per_seq)
    return pl.pallas_call(
        kern,
        grid=(nt,),
        in_specs=[pl.BlockSpec((tt, D), lambda i: (i, 0)),
                  pl.BlockSpec((None, nb, 6 * D), lambda i: (l, ns // nb, 0), pipeline_mode=pl.Buffered(1)),
                  _layer_spec(lp["g2"], l), _layer_spec(lp["w_router_t"], l), _layer_spec(lp["b_router_c"], l),
                  pl.BlockSpec((tt, tt), lambda i: (0, 0), pipeline_mode=pl.Buffered(1))],
        out_specs=(pl.BlockSpec((1, 1, tt), lambda i: (i, 0, 0)),
                   pl.BlockSpec((1, 1, LANES), lambda i: (i, 0, 0))),
        out_shape=(jax.ShapeDtypeStruct((nt, 1, tt), jnp.int32),
                   jax.ShapeDtypeStruct((nt, 1, LANES), jnp.int32)),
        compiler_params=pltpu.CompilerParams(
            dimension_semantics=("arbitrary",), vmem_limit_bytes=VMEM_LIMIT),
        name="moe_route",
    )(x2, mod, lp["g2"], lp["w_router_t"], lp["b_router_c"], tri)


def _top2_gates_t(lt, gidx):
    rows = -(-(N_GROUPS + N_EXP) // SUBLANES) * SUBLANES
    n = lt.shape[1]
    neg = jnp.float32(-jnp.inf)
    big = jnp.float32(1 << 20)
    l = lt[0:rows, :]
    row = lax.broadcasted_iota(jnp.int32, (rows, n), 0).astype(F32)
    own = jnp.sum(jnp.where(row == gidx, l, 0.0), axis=0, keepdims=True)
    p_group = 1.0 / jnp.sum(jnp.where(row < N_GROUPS, jnp.exp(l - own), 0.0), axis=0, keepdims=True)
    e0 = N_GROUPS + N_PER_GROUP * gidx
    el = jnp.where((row >= e0) & (row < e0 + N_PER_GROUP), l, neg)
    v1 = jnp.max(el, axis=0, keepdims=True)
    i1 = jnp.min(jnp.where(el == v1, row, big), axis=0, keepdims=True)
    el2 = jnp.where(row == i1, neg, el)
    v2 = jnp.max(el2, axis=0, keepdims=True)
    i2 = jnp.min(jnp.where(el2 == v2, row, big), axis=0, keepdims=True)
    ex = jnp.exp(v2 - v1)
    w1 = 1.0 / (1.0 + ex)
    gates = jnp.where(row == i1, p_group * w1, 0.0) + jnp.where(row == i2, p_group * (ex * w1), 0.0)
    return jnp.concatenate([gates, jnp.zeros((LANES - rows, n), F32)], axis=0)


def _moe_sorted_kernel(pos_sm, offs_sm, x_ref, mod_ref, g2_ref, wrt_ref, brc_ref, wg_ref, wu_ref, wd_ref, fg_ref,
                       o_ref, a_scr, b_scr, hs_scr, gate_scr, acc_scr, *, tiles_per_seq, final_norm):
    i = pl.program_id(0)
    g = pl.program_id(1)
    tt = x_ref.shape[0]
    base = i * tt
    b = i // tiles_per_seq

    def mod(k):
        return mod_ref[pl.ds(b, 1), k * D:(k + 1) * D]

    def tile_rows(t):
        return pl.ds(pl.multiple_of(t * SUBLANES, SUBLANES), SUBLANES)

    def slot_rows(t):
        return pl.ds(pl.multiple_of(pos_sm[base + t], SUBLANES), SUBLANES)

    def to_token_tiles(v):
        for s in range(SUBLANES):
            a_scr[s:tt * SUBLANES:SUBLANES, :] = v[:, s * LANES:(s + 1) * LANES]

    def from_token_tiles():
        return jnp.concatenate([b_scr[s:tt * SUBLANES:SUBLANES, :] for s in range(SUBLANES)], axis=1)

    @pl.when(g == 0)
    def _():
        h = _rms_rows(x_ref[...]) * g2_ref[...]
        to_token_tiles(h * (1.0 + mod(4)) + mod(3))

        def scatter(t, carry):
            b_scr[slot_rows(t), :] = a_scr[tile_rows(t), :]
            return carry
        lax.fori_loop(0, tt, scatter, 0, unroll=32)
        hs = from_token_tiles().astype(BF16)
        hs_scr[0:tt, :] = hs
        hs_scr[tt:tt + MOE_BLOCK, :] = jnp.zeros((MOE_BLOCK, D), BF16)
        lt = _dot_nt(wrt_ref[...], hs) + brc_ref[...]
        slot = lax.broadcasted_iota(jnp.int32, (1, tt), 1)
        gidx = jnp.zeros((1, tt), F32)
        for gg in range(1, N_GROUPS):
            gidx = gidx + jnp.where(slot >= offs_sm[i * LANES + gg], 1.0, 0.0)
        gate_scr[0:tt, :] = _top2_gates_t(lt, gidx).T
        gate_scr[tt:tt + MOE_BLOCK, :] = jnp.zeros((MOE_BLOCK, LANES), F32)
        acc_scr[...] = jnp.zeros_like(acc_scr)

    lo = offs_sm[i * LANES + g]
    hi = offs_sm[i * LANES + g + 1]
    start = (lo // BF16_ROWS) * BF16_ROWS
    span = jnp.where(hi > lo, hi - start, 0)
    lane0 = (N_GROUPS + N_PER_GROUP * g).astype(F32)

    def run_block(r0, m):
        rs = pl.ds(pl.multiple_of(r0, BF16_ROWS), m)
        blk_lane = lax.broadcasted_iota(jnp.int32, (m, LANES), 1).astype(F32)
        acc_scr[rs, :] += _group_experts(hs_scr[rs, :], gate_scr[rs, :], blk_lane, lane0, wg_ref, wu_ref, wd_ref)

    n_full = span // MOE_BLOCK

    def full_block(k, carry):
        run_block(start + k * MOE_BLOCK, MOE_BLOCK)
        return carry
    lax.fori_loop(0, n_full, full_block, 0)
    rem = span - n_full * MOE_BLOCK
    for m, below in zip(MOE_TAIL_BLOCKS, MOE_TAIL_BLOCKS[1:] + (0,)):
        @pl.when((rem > below) & (rem <= m))
        def _(m=m):
            run_block(start + n_full * MOE_BLOCK, m)

    @pl.when(g == N_GROUPS - 1)
    def _():
        to_token_tiles(acc_scr[0:tt, :])

        def gather(t, carry):
            b_scr[tile_rows(t), :] = a_scr[slot_rows(t), :]
            return carry
        lax.fori_loop(0, tt, gather, 0, unroll=32)
        xn = x_ref[...] + mod(5) * from_token_tiles()
        if final_norm:
            xn = _rms_rows(xn) * fg_ref[...]
        o_ref[...] = xn


def _moe_sorted_call(l, x2, mod, ns, nb, lp, final_g, pos, offs, *, tt, final_norm):
    t = x2.shape[0]
    nt = t // tt
    kern = functools.partial(_moe_sorted_kernel, tiles_per_seq=nt // nb, final_norm=final_norm)
    grid_spec = pltpu.PrefetchScalarGridSpec(
        num_scalar_prefetch=2,
        grid=(nt, N_GROUPS),
        in_specs=[pl.BlockSpec((tt, D), lambda i, g, *_: (i, 0)),
                  pl.BlockSpec((None, nb, 6 * D), lambda i, g, *_: (l, ns // nb, 0), pipeline_mode=pl.Buffered(1)),
                  _layer_spec(lp["g2"], l), _layer_spec(lp["w_router_t"], l), _layer_spec(lp["b_router_c"], l)]
        + _expert_specs(lp, l) + [pl.BlockSpec((1, D), lambda i, g, *_: (0, 0))],
        out_specs=pl.BlockSpec((tt, D), lambda i, g, *_: (i, 0)),
        scratch_shapes=[pltpu.VMEM((tt * SUBLANES, LANES), F32), pltpu.VMEM((tt * SUBLANES, LANES), F32),
                        pltpu.VMEM((tt + MOE_BLOCK, D), BF16), pltpu.VMEM((tt + MOE_BLOCK, LANES), F32),
                        pltpu.VMEM((tt + MOE_BLOCK, D), F32)])
    return pl.pallas_call(
        kern,
        grid_spec=grid_spec,
        out_shape=jax.ShapeDtypeStruct((t, D), F32),
        compiler_params=pltpu.CompilerParams(
            dimension_semantics=("arbitrary", "arbitrary"), vmem_limit_bytes=VMEM_LIMIT),
        name="moe_prompt",
    )(pos, offs, x2, mod, lp["g2"], lp["w_router_t"], lp["b_router_c"], lp["w_gate"], lp["w_up"], lp["w_down"],
      final_g)


def _pad_to(a, axis, size):
    pad = [(0, 0)] * a.ndim
    pad[axis] = (0, size - a.shape[axis])
    return jnp.pad(a, pad)


def _prepare_params(p):
    lb_cum = jnp.cumsum(jax.nn.softmax(p["hgrn_lb_logits"].astype(F32), axis=0), axis=0)
    lb = lb_cum - lb_cum[:1]
    lbv = _pad_to(jnp.stack([jnp.log(lb), jnp.log1p(-lb), 1.0 - lb], axis=1), 1, SUBLANES)
    a_re, a_im = p["s5_a_re"].astype(F32), p["s5_a_im"].astype(F32)
    dt = jnp.exp(p["s5_log_dt"].astype(F32))[..., None]
    mag = jnp.exp(a_re * dt)
    abar_re = mag * jnp.cos(a_im * dt)
    abar_im = mag * jnp.sin(a_im * dt)
    den = a_re * a_re + a_im * a_im
    nr, ni = abar_re - 1.0, abar_im
    coef_re = ((nr * a_re + ni * a_im) / den)[..., None]
    coef_im = ((ni * a_re - nr * a_im) / den)[..., None]
    b_re, b_im = p["s5_b_re"].astype(F32), p["s5_b_im"].astype(F32)
    bb_re = coef_re * b_re - coef_im * b_im
    bb_im = coef_re * b_im + coef_im * b_re
    eye = jnp.eye(C_G, dtype=F32)
    blk_b = lambda m: jnp.einsum("lgph,gk->lghkp", m, eye).reshape(DEPTH, C_W, C_S)
    blk_bt = lambda m: jnp.einsum("lgph,gk->lgpkh", m, eye).reshape(DEPTH, C_S, C_W)
    blk_c = lambda m: jnp.einsum("lghp,gk->lgpkh", m, eye).reshape(DEPTH, C_S, C_W)
    bb = jnp.concatenate([blk_b(bb_re), blk_b(bb_im)], axis=2).astype(BF16)
    bb_t = jnp.concatenate([blk_bt(bb_re), blk_bt(bb_im)], axis=1).astype(BF16)
    cc = jnp.concatenate([blk_c(p["s5_c_re"].astype(F32)), -blk_c(p["s5_c_im"].astype(F32))],
                         axis=1).astype(BF16)
    abar_rows = jnp.stack([abar_re.reshape(DEPTH, C_S), abar_im.reshape(DEPTH, C_S)], axis=1)
    router = jnp.concatenate([p["moe_w_group"], p["moe_w_expert"]], axis=2)
    b_router = _pad_to(jnp.concatenate([p["moe_b_group"], p["moe_b_expert"]], axis=1), 1, LANES)
    row = lambda a: a.astype(F32)[:, None, :]
    col = lambda a: a.astype(F32)[:, :, None]
    return dict(
        g1=row(p["norm1_g"]), g2=row(p["norm2_g"]),
        w_in=jnp.transpose(p["w_in"], (0, 2, 1)).astype(BF16), lbv=lbv,
        w_alpha=_pad_to(_pad_to(p["gla_w_alpha"], 1, LANES), 2, 256).astype(BF16),
        b_alpha=_pad_to(row(p["gla_b_alpha"]), 2, 256),
        hgrn_g=row(jnp.tile(p["hgrn_norm_g"], (1, A_H))), gla_g=row(jnp.tile(p["gla_norm_g"], (1, B_H))),
        hgrn_gc=col(p["hgrn_norm_g"]), gla_gc=col(p["gla_norm_g"]),
        abar=_pad_to(abar_rows, 1, SUBLANES), abar_c=jnp.transpose(abar_rows, (0, 2, 1)),
        bb=bb, bb_t=bb_t, cc=cc, d_skip=row(p["s5_d"]),
        w_glu=p["s5_w_glu"].astype(BF16), b_glu=row(p["s5_b_glu"]), w_out=p["w_out"].astype(BF16),
        w_gate=p["moe_w_gate"].astype(BF16), w_up=p["moe_w_up"].astype(BF16),
        w_down=p["moe_w_down"].astype(BF16),
        w_router=_pad_to(router, 2, LANES).astype(BF16), b_router=b_router[:, None, :],
        w_router_t=_pad_to(jnp.transpose(router, (0, 2, 1)), 1, LANES).astype(BF16),
        b_router_c=b_router[:, :, None])


def kernel(x_prompt, x_sample, state_hgrn, state_gla, state_s5_re, state_s5_im, c_prompt, c_sample, ada_w, ada_b, norm1_g, norm2_g, w_in, hgrn_lb_logits, hgrn_norm_g, gla_w_alpha, gla_b_alpha, gla_norm_g, s5_a_re, s5_a_im, s5_b_re, s5_b_im, s5_c_re, s5_c_im, s5_d, s5_log_dt, s5_w_glu, s5_b_glu, w_out, moe_w_group, moe_b_group, moe_w_expert, moe_b_expert, moe_w_gate, moe_w_up, moe_w_down, final_norm_g):
    lp = _prepare_params(dict(
        norm1_g=norm1_g, norm2_g=norm2_g, w_in=w_in, hgrn_lb_logits=hgrn_lb_logits, hgrn_norm_g=hgrn_norm_g,
        gla_w_alpha=gla_w_alpha, gla_b_alpha=gla_b_alpha, gla_norm_g=gla_norm_g, s5_a_re=s5_a_re,
        s5_a_im=s5_a_im, s5_b_re=s5_b_re, s5_b_im=s5_b_im, s5_c_re=s5_c_re, s5_c_im=s5_c_im, s5_d=s5_d,
        s5_log_dt=s5_log_dt, s5_w_glu=s5_w_glu, s5_b_glu=s5_b_glu, w_out=w_out, moe_w_group=moe_w_group,
        moe_b_group=moe_b_group, moe_w_expert=moe_w_expert, moe_b_expert=moe_b_expert,
        moe_w_gate=moe_w_gate, moe_w_up=moe_w_up, moe_w_down=moe_w_down))
    nb, seq, _ = x_prompt.shape
    ns = x_sample.shape[0]
    assert ns % nb == 0 and seq % CHUNK == 0
    tt = min(MOE_TILE, seq)
    assert seq % tt == 0 and tt % BF16_ROWS == 0
    final_g = final_norm_g.reshape(1, D)

    mod = _ada_call(jnp.concatenate([c_sample, c_prompt], axis=0), ada_w, ada_b)
    states = (jnp.transpose(state_hgrn, (0, 2, 3, 4, 1)), jnp.transpose(state_gla, (0, 2, 3, 4, 1)),
              jnp.transpose(state_s5_re, (0, 2, 3, 1)).reshape(DEPTH, C_S, ns),
              jnp.transpose(state_s5_im, (0, 2, 3, 1)).reshape(DEPTH, C_S, ns))
    new_states = tuple(jnp.zeros_like(s) for s in states)

    xp = x_prompt
    xs = x_sample.reshape(ns, D)
    outs = {k: [] for k in ("hg_p", "gl_p", "sr_p", "si_p")}
    for l in range(DEPTH):
        last = l == DEPTH - 1
        xp, sta, stb, s5r, s5i = _mixer_prompt_call(l, xp, mod, ns, lp)
        x2 = xp.reshape(nb * seq, D)
        pos, offs = _route_call(l, x2, mod, ns, nb, lp, tt=tt)
        xp = _moe_sorted_call(l, x2, mod, ns, nb, lp, final_g, pos.reshape(-1), offs.reshape(-1),
                              tt=tt, final_norm=last).reshape(nb, seq, D)
        sta = sta.reshape(nb, A_H // 2, 2, A_DV, 2, A_DK)
        outs["hg_p"].append(jnp.stack([sta[:, h // 2, h % 2, :, h % 2, :] for h in range(A_H)], axis=1)
                            .transpose(0, 1, 3, 2))
        stb = stb.reshape(nb, B_H, B_DV, 4, B_DK)
        outs["gl_p"].append(jnp.stack([stb[:, h, :, h % 4, :] for h in range(B_H)], axis=1)
                            .transpose(0, 1, 3, 2))
        outs["sr_p"].append(s5r.reshape(nb, C_G, C_P))
        outs["si_p"].append(s5i.reshape(nb, C_G, C_P))
        xs, *new_states = _mixer_sample_call(l, xs, mod, lp, states, new_states)
        xs = _moe_dense_call(l, xs, mod, lp, final_g, final_norm=last)

    hg_s, gl_s, sr_s, si_s = new_states
    return (xp, xs.reshape(ns, 1, D),
            jnp.stack(outs["hg_p"]), jnp.stack(outs["gl_p"]), jnp.stack(outs["sr_p"]), jnp.stack(outs["si_p"]),
            jnp.transpose(hg_s, (0, 4, 1, 2, 3)), jnp.transpose(gl_s, (0, 4, 1, 2, 3)),
            jnp.transpose(sr_s.reshape(DEPTH, C_G, C_P, ns), (0, 3, 1, 2)),
            jnp.transpose(si_s.reshape(DEPTH, C_G, C_P, ns), (0, 3, 1, 2)))
```
